```python
import math
import jax, jax.numpy as jnp
from jax import lax
import numpy as np

D_MODEL = 4096
BATCH = 4
SEQ = 2048
DEPTH = 1

N_MEM = 256
DIFF_WIDTH = D_MODEL // 2
DIFF_HEAD_DIM = 128
DIFF_V_DIM = 2 * DIFF_HEAD_DIM
DIFF_HEADS = DIFF_WIDTH // DIFF_V_DIM
DIFF_QK_WIDTH = DIFF_HEADS * 2 * DIFF_HEAD_DIM
CONV_WIDTH = D_MODEL // 4
CONV_TAPS = 31
MEM_WIDTH = D_MODEL // 4
MEM_HEADS = 4
MEM_HEAD_DIM = MEM_WIDTH // MEM_HEADS
MIX_WIDTH = DIFF_WIDTH + CONV_WIDTH + MEM_WIDTH
IN_SPLITS = [DIFF_QK_WIDTH, 2 * DIFF_QK_WIDTH, 2 * DIFF_QK_WIDTH + DIFF_WIDTH,
             2 * DIFF_QK_WIDTH + DIFF_WIDTH + 2 * CONV_WIDTH]
IN_COLS = IN_SPLITS[-1] + MEM_WIDTH
Q_BLOCK = 128
NUM_BUCKETS = 32
MAX_EXACT = NUM_BUCKETS // 2
MAX_DISTANCE = 128
N_EXPERTS = 32
TOP_K = 4
D_FF = 2048
SWIGLU_LIMIT = 7.0
SWIGLU_ALPHA = 1.702
EXPERT_BLOCK = 256
LN_EPS = 1e-5
DEEPNORM_ALPHA = (2 * DEPTH) ** 0.25
DEEPNORM_BETA = (8 * DEPTH) ** -0.25

kernel_name = "hymba_diffattn_conformer_memxattn_moe"


def layer_norm(x, g, b):
    xf = x.astype(jnp.float32)
    mu = jnp.mean(xf, -1, keepdims=True)
    var = jnp.mean(jnp.square(xf - mu), -1, keepdims=True)
    return ((xf - mu) * lax.rsqrt(var + LN_EPS) * g + b).astype(x.dtype)


def rms_norm(x, g):
    xf = x.astype(jnp.float32)
    return (xf * lax.rsqrt(jnp.mean(jnp.square(xf), -1, keepdims=True) + LN_EPS) * g).astype(x.dtype)


def t5_bucket(dist):
    n = jnp.maximum(dist, 0)
    nf = jnp.maximum(n, 1).astype(jnp.float32)
    large = MAX_EXACT + (jnp.log(nf / MAX_EXACT) / math.log(MAX_DISTANCE / MAX_EXACT)
                         * (NUM_BUCKETS - MAX_EXACT)).astype(jnp.int32)
    large = jnp.minimum(large, NUM_BUCKETS - 1)
    return jnp.where(n < MAX_EXACT, n, large)


def diff_attention(q, k, v, rel_table, lam, lam_init, subln_g):
    B, S, H = q.shape[:3]
    nb = S // Q_BLOCK
    qb = q.reshape(B, nb, Q_BLOCK, H, 2, DIFF_HEAD_DIM).transpose(1, 0, 2, 3, 4, 5)
    k_pos = jnp.arange(S)
    scale = DIFF_HEAD_DIM ** -0.5

    def block(args):
        q_blk, start = args
        dist = (start + jnp.arange(Q_BLOCK))[:, None] - k_pos[None, :]
        bias = rel_table[t5_bucket(dist)].transpose(2, 0, 1).astype(jnp.float32)
        logits = jnp.einsum('bqhcd,bkhcd->bchqk', q_blk, k).astype(jnp.float32) * scale + bias
        logits = jnp.where(dist >= 0, logits, -jnp.inf)
        probs = jax.nn.softmax(logits, axis=-1)
        w = probs[:, 0] - lam * probs[:, 1]
        return jnp.einsum('bhqk,bkhd->bqhd', w.astype(v.dtype), v)

    o = lax.map(block, (qb, jnp.arange(nb) * Q_BLOCK))
    o = o.transpose(1, 0, 2, 3, 4).reshape(B, S, H, DIFF_V_DIM)
    o = rms_norm(o, subln_g) * (1.0 - lam_init)
    return o.reshape(B, S, H * DIFF_V_DIM)


def conformer_conv(u, conv_w, conv_b, ln_g, ln_b):
    a, g = jnp.split(u, 2, axis=-1)
    h = a * jax.nn.sigmoid(g)
    h = lax.conv_general_dilated(h, conv_w, window_strides=(1,), padding=[(CONV_TAPS - 1, 0)],
                                 dimension_numbers=('NWC', 'WIO', 'NWC'),
                                 feature_group_count=CONV_WIDTH) + conv_b
    return jax.nn.silu(layer_norm(h, ln_g, ln_b))


def memory_attention(q, mem, w_mem_kv):
    B, S = q.shape[:2]
    M = mem.shape[1]
    k, v = jnp.split(mem @ w_mem_kv, 2, axis=-1)
    k = k.reshape(B, M, MEM_HEADS, MEM_HEAD_DIM)
    v = v.reshape(B, M, MEM_HEADS, MEM_HEAD_DIM)
    q = q.reshape(B, S, MEM_HEADS, MEM_HEAD_DIM)
    logits = jnp.einsum('bshd,bmhd->bhsm', q, k).astype(jnp.float32) * (MEM_HEAD_DIM ** -0.5)
    p = jax.nn.softmax(logits, axis=-1)
    return jnp.einsum('bhsm,bmhd->bshd', p.astype(v.dtype), v).reshape(B, S, MEM_WIDTH)


def moe(x, w_router, b_router, w_gu, b_gu, w_down, b_down):
    B, S, D = x.shape
    T = B * S
    xt = x.reshape(T, D)
    logits = (xt @ w_router + b_router).astype(jnp.float32)
    top_v, top_i = lax.top_k(logits, TOP_K)
    gates = jax.nn.softmax(top_v, axis=-1)
    n_assign = T * TOP_K
    n_blocks = -(-(n_assign + N_EXPERTS * (EXPERT_BLOCK - 1)) // EXPERT_BLOCK)
    flat_e = top_i.reshape(-1).astype(jnp.int32)
    order = jnp.argsort(flat_e)
    sorted_e = flat_e[order]
    counts = jnp.zeros((N_EXPERTS,), jnp.int32).at[flat_e].add(1)
    starts = jnp.cumsum(counts) - counts
    padded = (counts + EXPERT_BLOCK - 1) // EXPERT_BLOCK * EXPERT_BLOCK
    pad_ends = jnp.cumsum(padded)
    pad_starts = pad_ends - padded
    dest_sorted = pad_starts[sorted_e] + jnp.arange(n_assign, dtype=jnp.int32) - starts[sorted_e]
    buf_tok = jnp.full((n_blocks * EXPERT_BLOCK,), T, jnp.int32).at[dest_sorted].set(
        (order // TOP_K).astype(jnp.int32))
    dest = jnp.zeros((n_assign,), jnp.int32).at[order].set(dest_sorted)
    block_e = jnp.minimum(jnp.searchsorted(pad_ends, jnp.arange(n_blocks, dtype=jnp.int32) * EXPERT_BLOCK,
                                           side='right'), N_EXPERTS - 1)
    x_pad = jnp.concatenate([xt, jnp.zeros((1, D), xt.dtype)], axis=0)

    def expert_block(args):
        tok, e = args
        h = x_pad[tok] @ w_gu[e] + b_gu[e]
        gate, up = jnp.split(h, 2, axis=-1)
        gate = jnp.minimum(gate, SWIGLU_LIMIT)
        up = jnp.clip(up, -SWIGLU_LIMIT, SWIGLU_LIMIT)
        act = (up + 1.0) * (gate * jax.nn.sigmoid(SWIGLU_ALPHA * gate))
        return act @ w_down[e] + b_down[e]

    out = lax.map(expert_block, (buf_tok.reshape(n_blocks, EXPERT_BLOCK), block_e)).reshape(-1, D)
    y = jnp.einsum('tk,tkd->td', gates.astype(x.dtype), out[dest].reshape(T, TOP_K, D))
    return y.reshape(B, S, D)


def setup_inputs(seed: int = 0) -> dict:
    key = jax.random.key(seed)
    ks = iter(jax.random.split(key, 40))
    f32 = jnp.float32
    nrm = lambda shape, std: jax.random.normal(next(ks), shape, f32) * std
    D, L = D_MODEL, DEPTH
    sd = D ** -0.5
    w_in = jnp.concatenate([
        nrm((L, D, DIFF_QK_WIDTH), sd),
        nrm((L, D, DIFF_QK_WIDTH), sd),
        nrm((L, D, DIFF_WIDTH), sd * DEEPNORM_BETA),
        nrm((L, D, 2 * CONV_WIDTH), sd),
        nrm((L, D, MEM_WIDTH), sd),
    ], axis=-1)
    w_mem_kv = jnp.concatenate([nrm((L, D, MEM_WIDTH), sd),
                                nrm((L, D, MEM_WIDTH), sd * DEEPNORM_BETA)], axis=-1)
    return {
        "x": nrm((BATCH, SEQ, D), 1.0),
        "mem": nrm((BATCH, N_MEM, D), 1.0),
        "rel_table": nrm((NUM_BUCKETS, DIFF_HEADS), 0.5),
        "w_in": w_in,
        "w_mem_kv": w_mem_kv,
        "w_o": nrm((L, MIX_WIDTH, D), MIX_WIDTH ** -0.5 * DEEPNORM_BETA),
        "lambda_q1": nrm((L, DIFF_HEAD_DIM), 0.1),
        "lambda_k1": nrm((L, DIFF_HEAD_DIM), 0.1),
        "lambda_q2": nrm((L, DIFF_HEAD_DIM), 0.1),
        "lambda_k2": nrm((L, DIFF_HEAD_DIM), 0.1),
        "subln_g": 1.0 + nrm((L, DIFF_V_DIM), 0.02),
        "conv_w": nrm((L, CONV_TAPS, 1, CONV_WIDTH), CONV_TAPS ** -0.5),
        "conv_b": nrm((L, CONV_WIDTH), 0.02),
        "conv_ln_g": 1.0 + nrm((L, CONV_WIDTH), 0.02),
        "conv_ln_b": nrm((L, CONV_WIDTH), 0.02),
        "ln1_g": 1.0 + nrm((L, D), 0.02),
        "ln1_b": nrm((L, D), 0.02),
        "w_router": nrm((L, D, N_EXPERTS), sd),
        "b_router": nrm((L, N_EXPERTS), 0.01),
        "w_gate_up": nrm((L, N_EXPERTS, D, 2 * D_FF), sd),
        "b_gate_up": nrm((L, N_EXPERTS, 2 * D_FF), 0.02),
        "w_down": nrm((L, N_EXPERTS, D_FF, D), D_FF ** -0.5 * DEEPNORM_BETA),
        "b_down": nrm((L, N_EXPERTS, D), 0.02),
        "ln2_g": 1.0 + nrm((L, D), 0.02),
        "ln2_b": nrm((L, D), 0.02),
    }


def reference(x, mem, rel_table, w_in, w_mem_kv, w_o, lambda_q1, lambda_k1, lambda_q2, lambda_k2,
              subln_g, conv_w, conv_b, conv_ln_g, conv_ln_b, ln1_g, ln1_b, w_router, b_router,
              w_gate_up, b_gate_up, w_down, b_down, ln2_g, ln2_b):
    B, S, D = x.shape
    f32 = jnp.float32
    for l in range(DEPTH):
        lam_init = 0.8 - 0.6 * math.exp(-0.3 * l)
        lam = (jnp.exp(jnp.sum(lambda_q1[l].astype(f32) * lambda_k1[l].astype(f32)))
               - jnp.exp(jnp.sum(lambda_q2[l].astype(f32) * lambda_k2[l].astype(f32))) + lam_init)
        proj = x @ w_in[l]
        q_d, k_d, v_d, conv_in, q_m = jnp.split(proj, IN_SPLITS, axis=-1)
        a_out = diff_attention(q_d.reshape(B, S, DIFF_HEADS, 2, DIFF_HEAD_DIM),
                               k_d.reshape(B, S, DIFF_HEADS, 2, DIFF_HEAD_DIM),
                               v_d.reshape(B, S, DIFF_HEADS, DIFF_V_DIM),
                               rel_table, lam, lam_init, subln_g[l])
        c_out = conformer_conv(conv_in, conv_w[l], conv_b[l], conv_ln_g[l], conv_ln_b[l])
        m_out = memory_attention(q_m, mem, w_mem_kv[l])
        mix = jnp.concatenate([a_out, c_out, m_out], axis=-1) @ w_o[l]
        x = layer_norm(DEEPNORM_ALPHA * x + mix, ln1_g[l], ln1_b[l])
        ffn = moe(x, w_router[l], b_router[l], w_gate_up[l], b_gate_up[l], w_down[l], b_down[l])
        x = layer_norm(DEEPNORM_ALPHA * x + ffn, ln2_g[l], ln2_b[l])
    return x
```

```python
import functools
import math

import jax
import jax.numpy as jnp
from jax import lax
from jax.experimental import pallas as pl
from jax.experimental.pallas import tpu as pltpu

F32 = jnp.float32
BF16 = jnp.bfloat16
I32 = jnp.int32
U32 = jnp.uint32

HEAD_DIM = 128
V_DIM = 2 * HEAD_DIM
CONV_TAPS = 31
NUM_BUCKETS = 32
MAX_EXACT = NUM_BUCKETS // 2
MAX_DISTANCE = 128
N_EXPERTS = 32
TOP_K = 4
SWIGLU_LIMIT = 7.0
SWIGLU_ALPHA = 1.702
LN_EPS = 1e-5
DEPTH = 1
DEEPNORM_ALPHA = (2 * DEPTH) ** 0.25
LAM_INIT = 0.8 - 0.6 * math.exp(-0.3 * 0)

LANES = 128
V7X_VMEM_BYTES = 64 * 1024 * 1024
VMEM_LIMIT = 58 * 1024 * 1024

ATT_T = 256
CONV_TS = 256
CONV_HALO = 32
MEM_TQ = 512
MOE_TM = 1152
MOE_PAD = 128
MOE_TN_UP = 256
MOE_TN_DOWN = 512
RANK_TT = 512
DISPATCH_TA = 1024
COMBINE_TC = 128
FILL_ROWS = 1024


def _cparams(sem, vmem=VMEM_LIMIT):
    return pltpu.CompilerParams(dimension_semantics=sem, vmem_limit_bytes=vmem)


def _proj_kernel(a_ref, w_ref, cs_ref, o_ref):
    acc = jnp.dot(a_ref[...], w_ref[...].astype(BF16), preferred_element_type=F32)
    o_ref[...] = (acc * cs_ref[...]).astype(o_ref.dtype)


def _proj_matmul(a, w, col_scale, *, tm, tn, out_dtype):
    M, K = a.shape
    N = w.shape[1]
    return pl.pallas_call(
        _proj_kernel,
        grid=(M // tm, N // tn),
        in_specs=[pl.BlockSpec((tm, K), lambda i, j: (i, 0)),
                  pl.BlockSpec((K, tn), lambda i, j: (0, j)),
                  pl.BlockSpec((1, tn), lambda i, j: (0, j))],
        out_specs=pl.BlockSpec((tm, tn), lambda i, j: (i, j)),
        out_shape=jax.ShapeDtypeStruct((M, N), out_dtype),
        compiler_params=_cparams(("parallel", "arbitrary")),
        name="proj_matmul",
    )(a, w, col_scale)


def _bias_kernel(tbl_ref, o_ref, *, t, n_heads):
    h = pl.program_id(0)
    i = lax.broadcasted_iota(I32, (t, t), 0)
    j = lax.broadcasted_iota(I32, (t, t), 1)
    for o in range(2):
        dist = o * t + i - j
        n = jnp.maximum(dist, 0)
        nf = jnp.maximum(n, 1).astype(F32)
        large = MAX_EXACT + (jnp.log(nf / MAX_EXACT) / math.log(MAX_DISTANCE / MAX_EXACT)
                             * (NUM_BUCKETS - MAX_EXACT)).astype(I32)
        large = jnp.minimum(large, NUM_BUCKETS - 1)
        bucket = jnp.where(n < MAX_EXACT, n, large)
        val = jnp.zeros((t, t), F32)
        for b in range(NUM_BUCKETS):
            val = jnp.where(bucket == b, tbl_ref[b * n_heads + h], val)
        if o == 0:
            val = jnp.where(dist >= 0, val, -jnp.inf)
        o_ref[0, o] = val
    o_ref[0, 2] = jnp.full((t, t), tbl_ref[(NUM_BUCKETS - 1) * n_heads + h], F32)


def _bias_tiles(rel_table, t):
    n_heads = rel_table.shape[1]
    assert t + 1 >= MAX_DISTANCE
    return pl.pallas_call(
        functools.partial(_bias_kernel, t=t, n_heads=n_heads),
        grid=(n_heads,),
        in_specs=[pl.BlockSpec(memory_space=pltpu.SMEM)],
        out_specs=pl.BlockSpec((1, 3, t, t), lambda h: (h, 0, 0, 0)),
        out_shape=jax.ShapeDtypeStruct((n_heads, 3, t, t), F32),
        compiler_params=_cparams(("arbitrary",)),
        name="bias_tiles",
    )(rel_table.reshape(-1))


def _attn_kernel(lam_ref, q_ref, k_ref, v_ref, bias_ref, g_ref, o_ref, s_ref, acc_ref, *, t):
    qi = pl.program_id(2)
    nk = qi + 1
    q = q_ref[0]
    q1 = q[:, :HEAD_DIM]
    q2 = q[:, HEAD_DIM:]
    dn = (((1,), (1,)), ((), ()))

    def pass1(kb, carry):
        m1, m2 = carry
        k = k_ref[0, pl.ds(pl.multiple_of(kb * t, t), t), :]
        b = bias_ref[0, jnp.minimum(qi - kb, 2)]
        s1 = lax.dot_general(q1, k[:, :HEAD_DIM], dn, preferred_element_type=F32) + b
        s2 = lax.dot_general(q2, k[:, HEAD_DIM:], dn, preferred_element_type=F32) + b
        s_ref[0, kb] = s1
        s_ref[1, kb] = s2
        return (jnp.maximum(m1, jnp.max(s1, axis=-1, keepdims=True)),
                jnp.maximum(m2, jnp.max(s2, axis=-1, keepdims=True)))

    neg = jnp.full((t, 1), -jnp.inf, F32)
    m1, m2 = lax.fori_loop(0, nk, pass1, (neg, neg))

    def pass2(kb, carry):
        l1, l2 = carry
        p1 = jnp.exp(s_ref[0, kb] - m1)
        p2 = jnp.exp(s_ref[1, kb] - m2)
        s_ref[0, kb] = p1
        s_ref[1, kb] = p2
        return (l1 + jnp.sum(p1, axis=-1, keepdims=True),
                l2 + jnp.sum(p2, axis=-1, keepdims=True))

    zero = jnp.zeros((t, 1), F32)
    l1, l2 = lax.fori_loop(0, nk, pass2, (zero, zero))

    lp = lam_ref[...]
    lam = (jnp.exp(jnp.sum(lp[0:1] * lp[1:2], axis=-1, keepdims=True))
           - jnp.exp(jnp.sum(lp[2:3] * lp[3:4], axis=-1, keepdims=True)) + LAM_INIT)
    a1 = 1.0 / l1
    a2 = lam / l2

    acc_ref[...] = jnp.zeros_like(acc_ref)

    def pass3(kb, carry):
        w = s_ref[0, kb] * a1 - s_ref[1, kb] * a2
        v = v_ref[0, pl.ds(pl.multiple_of(kb * t, t), t), :]
        acc_ref[...] += jnp.dot(w.astype(BF16), v, preferred_element_type=F32)
        return carry

    lax.fori_loop(0, nk, pass3, 0)
    o = acc_ref[...]
    o = o * lax.rsqrt(jnp.mean(jnp.square(o), axis=-1, keepdims=True) + LN_EPS) * g_ref[...]
    o_ref[0] = (o * (1.0 - LAM_INIT)).astype(o_ref.dtype)


def _diff_attention(proj3, bias, lam_params, subln_g, *, n_heads, k_col, v_col):
    B, S, _ = proj3.shape
    t = ATT_T
    nq = S // t
    w = 2 * HEAD_DIM
    return pl.pallas_call(
        functools.partial(_attn_kernel, t=t),
        grid=(B, n_heads, nq),
        in_specs=[pl.BlockSpec((4, HEAD_DIM), lambda b, h, i: (0, 0)),
                  pl.BlockSpec((1, t, w), lambda b, h, i: (b, i, h)),
                  pl.BlockSpec((1, S, w), lambda b, h, i: (b, 0, k_col + h)),
                  pl.BlockSpec((1, S, w), lambda b, h, i: (b, 0, v_col + h)),
                  pl.BlockSpec((1, 3, t, t), lambda b, h, i: (h, 0, 0, 0)),
                  pl.BlockSpec((1, V_DIM), lambda b, h, i: (0, 0))],
        out_specs=pl.BlockSpec((1, t, V_DIM), lambda b, h, i: (b, i, h)),
        out_shape=jax.ShapeDtypeStruct((B, S, n_heads * V_DIM), BF16),
        scratch_shapes=[pltpu.VMEM((2, nq, t, t), F32), pltpu.VMEM((t, V_DIM), F32)],
        compiler_params=_cparams(("parallel", "parallel", "arbitrary")),
        name="diff_attention",
    )(lam_params, proj3, proj3, proj3, bias, subln_g)


def _conv_kernel(a_ref, g_ref, ap_ref, gp_ref, w_ref, cb_ref, lg_ref, lb_ref, o_ref,
                 hs_ref, cv_ref, *, ts, halo, taps, rc):
    i = pl.program_id(1)
    C = a_ref.shape[-1]
    hs_ref[halo:, :] = a_ref[0].astype(F32) * jax.nn.sigmoid(g_ref[0].astype(F32))
    hp = ap_ref[0].astype(F32) * jax.nn.sigmoid(gp_ref[0].astype(F32))
    hs_ref[:halo, :] = jnp.where(i > 0, hp, 0.0)
    base = halo - (taps - 1)
    for c in range(C // LANES):
        cs = slice(c * LANES, (c + 1) * LANES)
        for r in range(ts // rc):
            acc = jnp.zeros((rc, LANES), F32)
            for tp in range(taps):
                acc = acc + w_ref[tp:tp + 1, cs] * hs_ref[pl.ds(base + tp + r * rc, rc), cs]
            cv_ref[r * rc:(r + 1) * rc, cs] = acc + cb_ref[:, cs]
    h = cv_ref[...]
    mu = jnp.mean(h, axis=-1, keepdims=True)
    var = jnp.mean(jnp.square(h - mu), axis=-1, keepdims=True)
    y = (h - mu) * lax.rsqrt(var + LN_EPS) * lg_ref[...] + lb_ref[...]
    o_ref[0] = (y * jax.nn.sigmoid(y)).astype(o_ref.dtype)


def _conformer_conv(proj3, conv_w, conv_b, ln_g, ln_b, *, a_col, g_col):
    B, S, _ = proj3.shape
    taps, C = conv_w.shape
    ts, halo = CONV_TS, CONV_HALO
    assert halo >= taps - 1
    hb = ts // halo
    prev = lambda col: (lambda b, i: (b, jnp.maximum(i * hb - 1, 0), col))
    vec = pl.BlockSpec((1, C), lambda b, i: (0, 0))
    return pl.pallas_call(
        functools.partial(_conv_kernel, ts=ts, halo=halo, taps=taps, rc=128),
        grid=(B, S // ts),
        in_specs=[pl.BlockSpec((1, ts, C), lambda b, i: (b, i, a_col)),
                  pl.BlockSpec((1, ts, C), lambda b, i: (b, i, g_col)),
                  pl.BlockSpec((1, halo, C), prev(a_col)),
                  pl.BlockSpec((1, halo, C), prev(g_col)),
                  pl.BlockSpec((taps, C), lambda b, i: (0, 0)),
                  vec, vec, vec],
        out_specs=pl.BlockSpec((1, ts, C), lambda b, i: (b, i, 0)),
        out_shape=jax.ShapeDtypeStruct((B, S, C), BF16),
        scratch_shapes=[pltpu.VMEM((halo + ts, C), F32), pltpu.VMEM((ts, C), F32)],
        compiler_params=_cparams(("parallel", "arbitrary")),
        name="conformer_conv",
    )(proj3, proj3, proj3, proj3, conv_w, conv_b, ln_g, ln_b)


def _mem_attn_kernel(q_ref, k_ref, v_ref, o_ref):
    s = lax.dot_general(q_ref[0], k_ref[0], (((1,), (1,)), ((), ())), preferred_element_type=F32)
    m = jnp.max(s, axis=-1, keepdims=True)
    p = jnp.exp(s - m)
    p = p / jnp.sum(p, axis=-1, keepdims=True)
    o_ref[0] = jnp.dot(p.astype(BF16), v_ref[0], preferred_element_type=F32).astype(o_ref.dtype)


def _memory_attention(proj3, kv3, *, n_heads, q_col, hd):
    B, S, _ = proj3.shape
    M = kv3.shape[1]
    tq = MEM_TQ
    return pl.pallas_call(
        _mem_attn_kernel,
        grid=(B, n_heads, S // tq),
        in_specs=[pl.BlockSpec((1, tq, hd), lambda b, h, i: (b, i, q_col + h)),
                  pl.BlockSpec((1, M, hd), lambda b, h, i: (b, 0, h)),
                  pl.BlockSpec((1, M, hd), lambda b, h, i: (b, 0, n_heads + h))],
        out_specs=pl.BlockSpec((1, tq, hd), lambda b, h, i: (b, i, h)),
        out_shape=jax.ShapeDtypeStruct((B, S, n_heads * hd), BF16),
        compiler_params=_cparams(("parallel", "parallel", "arbitrary")),
        name="memory_attention",
    )(proj3, kv3, kv3)


def _oproj_kernel(a_ref, c_ref, m_ref, wa_ref, wc_ref, wm_ref, o_ref):
    acc = jnp.dot(a_ref[...], wa_ref[...].astype(BF16), preferred_element_type=F32)
    acc += jnp.dot(c_ref[...], wc_ref[...].astype(BF16), preferred_element_type=F32)
    acc += jnp.dot(m_ref[...], wm_ref[...].astype(BF16), preferred_element_type=F32)
    o_ref[...] = acc


def _out_projection(a, c, m, w_o, *, tm, tn):
    M, Ka = a.shape
    Kc, Km = c.shape[1], m.shape[1]
    assert Kc == Km and Ka % Kc == 0
    N = w_o.shape[1]
    return pl.pallas_call(
        _oproj_kernel,
        grid=(M // tm, N // tn),
        in_specs=[pl.BlockSpec((tm, Ka), lambda i, j: (i, 0)),
                  pl.BlockSpec((tm, Kc), lambda i, j: (i, 0)),
                  pl.BlockSpec((tm, Km), lambda i, j: (i, 0)),
                  pl.BlockSpec((Ka, tn), lambda i, j: (0, j)),
                  pl.BlockSpec((Kc, tn), lambda i, j: (Ka // Kc, j)),
                  pl.BlockSpec((Km, tn), lambda i, j: (Ka // Kc + 1, j))],
        out_specs=pl.BlockSpec((tm, tn), lambda i, j: (i, j)),
        out_shape=jax.ShapeDtypeStruct((M, N), F32),
        compiler_params=_cparams(("parallel", "arbitrary")),
        name="out_projection",
    )(a, c, m, w_o, w_o, w_o)


def _pack_pairs(lo, hi):
    lo_b = lax.bitcast_convert_type(lo.astype(BF16).astype(F32), U32)
    hi_b = lax.bitcast_convert_type(hi.astype(BF16).astype(F32), U32)
    return (lo_b >> 16) | (hi_b & jnp.uint32(0xFFFF0000))


def _unpack_lo(u):
    return lax.bitcast_convert_type(u << 16, F32)


def _unpack_hi(u):
    return lax.bitcast_convert_type(u & jnp.uint32(0xFFFF0000), F32)


def _layer_norm(y, g, b):
    mu = jnp.mean(y, axis=-1, keepdims=True)
    var = jnp.mean(jnp.square(y - mu), axis=-1, keepdims=True)
    return (y - mu) * lax.rsqrt(var + LN_EPS) * g + b


def _ln_router_kernel(x_ref, mix_ref, g_ref, b_ref, wr_ref, br_ref,
                      x1_ref, xp_ref, topi_ref, gate_ref):
    x1 = _layer_norm(DEEPNORM_ALPHA * x_ref[...] + mix_ref[...], g_ref[...], b_ref[...])
    x1_ref[...] = x1
    half = x1.shape[1] // 2
    xp_ref[...] = _pack_pairs(x1[:, :half], x1[:, half:])
    logits = jnp.dot(x1, wr_ref[...], preferred_element_type=F32,
                     precision=lax.Precision.HIGHEST) + br_ref[...]
    tm, ne = logits.shape
    lane = lax.broadcasted_iota(I32, (tm, ne), 1)
    kk = lax.broadcasted_iota(I32, (tm, TOP_K), 1)
    idx = jnp.zeros((tm, TOP_K), I32)
    val = jnp.zeros((tm, TOP_K), F32)
    cur = logits
    for k in range(TOP_K):
        mx = jnp.max(cur, axis=-1, keepdims=True)
        ix = jnp.min(jnp.where(cur == mx, lane, ne), axis=-1, keepdims=True)
        idx = jnp.where(kk == k, ix, idx)
        val = jnp.where(kk == k, mx, val)
        cur = jnp.where(lane == ix, -jnp.inf, cur)
    e = jnp.exp(val - jnp.max(val, axis=-1, keepdims=True))
    topi_ref[...] = idx
    gate_ref[...] = e / jnp.sum(e, axis=-1, keepdims=True)


def _ln_router(x, mix, g, b, w_router, b_router, *, tm):
    T, D = x.shape
    ne = w_router.shape[1]
    row = pl.BlockSpec((tm, D), lambda i: (i, 0))
    vec = pl.BlockSpec((1, D), lambda i: (0, 0))
    return pl.pallas_call(
        _ln_router_kernel,
        grid=(T // tm,),
        in_specs=[row, row, vec, vec,
                  pl.BlockSpec((D, ne), lambda i: (0, 0)),
                  pl.BlockSpec((1, ne), lambda i: (0, 0))],
        out_specs=[row,
                   pl.BlockSpec((tm, D // 2), lambda i: (i, 0)),
                   pl.BlockSpec((tm, TOP_K), lambda i: (i, 0)),
                   pl.BlockSpec((tm, TOP_K), lambda i: (i, 0))],
        out_shape=[jax.ShapeDtypeStruct((T, D), F32),
                   jax.ShapeDtypeStruct((T, D // 2), U32),
                   jax.ShapeDtypeStruct((T, TOP_K), I32),
                   jax.ShapeDtypeStruct((T, TOP_K), F32)],
        compiler_params=_cparams(("parallel",)),
        name="ln_router",
    )(x, mix, g, b, w_router, b_router)


def _rank_kernel(topi_ref, dest_ref, cnt_ref, ps_ref, rank_ref, carry_ref, *, tt, pad):
    ph = pl.program_id(0)
    i = pl.program_id(1)
    lane = lax.broadcasted_iota(I32, (tt, LANES), 1)
    kk = lax.broadcasted_iota(I32, (tt, TOP_K), 1)
    ti = topi_ref[...]
    sel = [lane == ti[:, k:k + 1] for k in range(TOP_K)]
    rows = pl.ds(pl.multiple_of(i * tt, tt), tt)

    @pl.when(ph == 0)
    def _():
        @pl.when(i == 0)
        def _():
            carry_ref[...] = jnp.zeros_like(carry_ref)

        oh = jnp.zeros((tt, LANES), F32)
        for k in range(TOP_K):
            oh = oh + sel[k].astype(F32)
        r = lax.broadcasted_iota(I32, (tt, tt), 0)
        c = lax.broadcasted_iota(I32, (tt, tt), 1)
        lower = (r > c).astype(BF16)
        before = jnp.dot(lower, oh.astype(BF16), preferred_element_type=F32) + carry_ref[0:1, :]
        rank = jnp.zeros((tt, TOP_K), F32)
        for k in range(TOP_K):
            rk = jnp.sum(jnp.where(sel[k], before, 0.0), axis=-1, keepdims=True)
            rank = jnp.where(kk == k, rk, rank)
        rank_ref[rows, :] = rank
        carry_ref[...] = carry_ref[...] + jnp.sum(oh, axis=0, keepdims=True)

    @pl.when(ph == 1)
    def _():
        cnt = carry_ref[...].astype(I32)
        padded = (cnt + (pad - 1)) & jnp.int32(-pad)
        l8 = lax.broadcasted_iota(I32, (8, LANES), 1)
        scan = padded
        for sh in (1, 2, 4, 8, 16, 32, 64):
            scan = scan + jnp.where(l8 >= sh, pltpu.roll(scan, sh, axis=1), 0)
        starts = scan - padded
        cnt_ref[...] = cnt
        ps_ref[...] = starts
        st = starts[0:1, :].astype(F32)
        rank = rank_ref[rows, :]
        dest = jnp.zeros((tt, TOP_K), F32)
        for k in range(TOP_K):
            base = jnp.sum(jnp.where(sel[k], st, 0.0), axis=-1, keepdims=True)
            dest = jnp.where(kk == k, base, dest)
        dest_ref[...] = (dest + rank).astype(I32)


def _rank_assignments(topi):
    T = topi.shape[0]
    tt = RANK_TT
    return pl.pallas_call(
        functools.partial(_rank_kernel, tt=tt, pad=MOE_PAD),
        grid=(2, T // tt),
        in_specs=[pl.BlockSpec((tt, TOP_K), lambda p, i: (i, 0))],
        out_specs=[pl.BlockSpec((tt, TOP_K), lambda p, i: (i * p, 0)),
                   pl.BlockSpec((8, LANES), lambda p, i: (0, 0)),
                   pl.BlockSpec((8, LANES), lambda p, i: (0, 0))],
        out_shape=[jax.ShapeDtypeStruct((T, TOP_K), I32),
                   jax.ShapeDtypeStruct((8, LANES), I32),
                   jax.ShapeDtypeStruct((8, LANES), I32)],
        scratch_shapes=[pltpu.VMEM((T, TOP_K), F32), pltpu.VMEM((8, LANES), F32)],
        compiler_params=_cparams(("arbitrary", "arbitrary")),
        name="rank_assignments",
    )(topi)


def _dispatch_kernel(dest_ref, xp_ref, xs_ref, zero_ref, fill_sem, row_sem, *, ta, n_fill):
    i = pl.program_id(0)

    def row_copy(t, d):
        return pltpu.make_async_copy(xp_ref.at[pl.ds(t, 1), :], xs_ref.at[pl.ds(d, 1), :], row_sem)

    @pl.when(i == 0)
    def _():
        zero_ref[...] = jnp.zeros_like(zero_ref)
        fr = zero_ref.shape[0]

        def fill(f):
            return pltpu.make_async_copy(zero_ref, xs_ref.at[pl.ds(f * fr, fr), :], fill_sem)

        for f in range(n_fill):
            fill(f).start()
        for f in range(n_fill):
            fill(f).wait()

    def issue(a, carry):
        row_copy((i * ta + a) // TOP_K, dest_ref[0, 0, a]).start()
        return carry

    lax.fori_loop(0, ta, issue, 0)

    def drain(a, carry):
        row_copy(0, 0).wait()
        return carry

    lax.fori_loop(0, ta, drain, 0)


def _dispatch(dest, xp, n_rows):
    T, W = xp.shape
    ta = DISPATCH_TA
    na = dest.size // ta
    assert n_rows % FILL_ROWS == 0
    return pl.pallas_call(
        functools.partial(_dispatch_kernel, ta=ta, n_fill=n_rows // FILL_ROWS),
        grid=(na,),
        in_specs=[pl.BlockSpec((1, 1, ta), lambda i: (i, 0, 0), memory_space=pltpu.SMEM),
                  pl.BlockSpec(memory_space=pl.ANY)],
        out_specs=pl.BlockSpec(memory_space=pl.ANY),
        out_shape=jax.ShapeDtypeStruct((n_rows, W), U32),
        scratch_shapes=[pltpu.VMEM((FILL_ROWS, W), U32),
                        pltpu.SemaphoreType.DMA(()), pltpu.SemaphoreType.DMA(())],
        compiler_params=_cparams(("arbitrary",)),
        name="dispatch_rows",
    )(dest.reshape(na, 1, ta), xp)


class _TileWriter:
    def __init__(self, wr_ref, wc_ref, stage_ref, out_ref, sem, tn):
        self.wr_ref, self.wc_ref = wr_ref, wc_ref
        self.stage_ref, self.out_ref, self.sem, self.tn = stage_ref, out_ref, sem, tn
        self.n_chunks = stage_ref.shape[1] // MOE_PAD

    def _copy(self, slot, w, j, c):
        row = pl.multiple_of(self.wr_ref[w] + c * MOE_PAD, MOE_PAD)
        col = pl.multiple_of(j * self.tn, self.tn)
        return pltpu.make_async_copy(
            self.stage_ref.at[slot, pl.ds(c * MOE_PAD, MOE_PAD), :],
            self.out_ref.at[pl.ds(row, MOE_PAD), pl.ds(col, self.tn)],
            self.sem.at[slot])

    def _each_chunk(self, w, fn):
        for c in range(self.n_chunks):
            pl.when(c < self.wc_ref[w])(functools.partial(fn, c))

    def start(self, slot, w, j):
        self._each_chunk(w, lambda c: self._copy(slot, w, j, c).start())

    def wait(self, slot, w, j):
        self._each_chunk(w, lambda c: self._copy(slot, w, j, c).wait())

    def retire_older(self, w, j, nj):
        step = w * nj + j

        @pl.when(step >= 2)
        def _():
            wrap = j < 2
            self.wait(step % 2, jnp.where(wrap, w - 1, w), jnp.where(wrap, j + nj - 2, j - 2))

    def drain(self, w, j, nw, nj):
        step = w * nj + j

        @pl.when(step == nw * nj - 1)
        def _():
            self.wait((step + 1) % 2, w, j - 1)
            self.wait(step % 2, w, j)


def _fill_tail(misc_ref, zero_ref, out_ref, sem):
    zero_ref[...] = jnp.zeros_like(zero_ref)
    n_total = out_ref.shape[0] // MOE_PAD

    def fill(c):
        row = pl.multiple_of(c * MOE_PAD, MOE_PAD)
        return pltpu.make_async_copy(zero_ref, out_ref.at[pl.ds(row, MOE_PAD), :], sem)

    def start(c, carry):
        fill(c).start()
        return carry

    def wait(c, carry):
        fill(c).wait()
        return carry

    lax.fori_loop(misc_ref[0], n_total, start, 0)
    lax.fori_loop(misc_ref[0], n_total, wait, 0)


def _moe_up_kernel(we_ref, wr_ref, wc_ref, misc_ref, xs_ref, wg_ref, wu_ref, bg_ref, bu_ref,
                   h_ref, xb_ref, stage_ref, zero_ref, sem, fill_sem):
    w = pl.program_id(0)
    j = pl.program_id(1)
    nw = pl.num_programs(0)
    nj = pl.num_programs(1)
    slot = (w * nj + j) % 2
    writer = _TileWriter(wr_ref, wc_ref, stage_ref, h_ref, sem, stage_ref.shape[2])

    @pl.when((w == 0) & (j == 0))
    def _():
        _fill_tail(misc_ref, zero_ref, h_ref, fill_sem)

    writer.retire_older(w, j, nj)

    @pl.when(wc_ref[w] > 0)
    def _():
        half = xs_ref.shape[1]

        @pl.when(j == 0)
        def _():
            u = xs_ref[...]
            xb_ref[:, :half] = _unpack_lo(u).astype(BF16)
            xb_ref[:, half:] = _unpack_hi(u).astype(BF16)

        x = xb_ref[...]
        gate = jnp.dot(x, wg_ref[0].astype(BF16), preferred_element_type=F32) + bg_ref[0]
        up = jnp.dot(x, wu_ref[0].astype(BF16), preferred_element_type=F32) + bu_ref[0]
        gate = jnp.minimum(gate, SWIGLU_LIMIT)
        up = jnp.clip(up, -SWIGLU_LIMIT, SWIGLU_LIMIT)
        act = (up + 1.0) * (gate * jax.nn.sigmoid(SWIGLU_ALPHA * gate))
        stage_ref[slot] = act.astype(stage_ref.dtype)
        writer.start(slot, w, j)

    writer.drain(w, j, nw, nj)


def _moe_up(xs, w_gu, b_gu, we, wr, wc, misc, *, d_ff):
    R, half = xs.shape
    E, D, _ = w_gu.shape
    tm, tn = MOE_TM, MOE_TN_UP
    nj = d_ff // tn
    nw = we.shape[0]
    assert nj >= 2 and R % MOE_PAD == 0
    jeff = lambda w, j, wc: jnp.where(wc[w] > 0, j, nj - 1)
    return pl.pallas_call(
        _moe_up_kernel,
        grid_spec=pltpu.PrefetchScalarGridSpec(
            num_scalar_prefetch=4,
            grid=(nw, nj),
            in_specs=[pl.BlockSpec((pl.Element(tm), pl.Element(half)),
                                   lambda w, j, we, wr, wc, ms: (pl.multiple_of(wr[w], MOE_PAD), 0)),
                      pl.BlockSpec((1, D, tn), lambda w, j, we, wr, wc, ms: (we[w], 0, jeff(w, j, wc))),
                      pl.BlockSpec((1, D, tn), lambda w, j, we, wr, wc, ms: (we[w], 0, nj + jeff(w, j, wc))),
                      pl.BlockSpec((1, 1, tn), lambda w, j, we, wr, wc, ms: (we[w], 0, jeff(w, j, wc))),
                      pl.BlockSpec((1, 1, tn), lambda w, j, we, wr, wc, ms: (we[w], 0, nj + jeff(w, j, wc)))],
            out_specs=pl.BlockSpec(memory_space=pl.ANY),
            scratch_shapes=[pltpu.VMEM((tm, 2 * half), BF16),
                            pltpu.VMEM((2, tm, tn), BF16),
                            pltpu.VMEM((MOE_PAD, d_ff), BF16),
                            pltpu.SemaphoreType.DMA((2,)), pltpu.SemaphoreType.DMA(())]),
        out_shape=jax.ShapeDtypeStruct((R, d_ff), BF16),
        compiler_params=_cparams(("arbitrary", "arbitrary")),
        name="moe_up",
    )(we, wr, wc, misc, xs, w_gu, w_gu, b_gu, b_gu)


def _moe_down_kernel(we_ref, wr_ref, wc_ref, misc_ref, h_ref, wlo_ref, whi_ref, blo_ref, bhi_ref,
                     o_ref, stage_ref, zero_ref, sem, fill_sem):
    w = pl.program_id(0)
    j = pl.program_id(1)
    nw = pl.num_programs(0)
    nj = pl.num_programs(1)
    slot = (w * nj + j) % 2
    writer = _TileWriter(wr_ref, wc_ref, stage_ref, o_ref, sem, stage_ref.shape[2])

    @pl.when((w == 0) & (j == 0))
    def _():
        _fill_tail(misc_ref, zero_ref, o_ref, fill_sem)

    writer.retire_older(w, j, nj)

    @pl.when(wc_ref[w] > 0)
    def _():
        h = h_ref[...]
        lo = jnp.dot(h, wlo_ref[0].astype(BF16), preferred_element_type=F32) + blo_ref[0]
        hi = jnp.dot(h, whi_ref[0].astype(BF16), preferred_element_type=F32) + bhi_ref[0]
        stage_ref[slot] = _pack_pairs(lo, hi)
        writer.start(slot, w, j)

    writer.drain(w, j, nw, nj)


def _moe_down(h, w_down, b_down, we, wr, wc, misc):
    R, d_ff = h.shape
    E, _, D = w_down.shape
    tm, tn = MOE_TM, MOE_TN_DOWN
    half = D // 2
    nj = half // tn
    nw = we.shape[0]
    assert nj >= 2 and R % MOE_PAD == 0
    jeff = lambda w, j, wc: jnp.where(wc[w] > 0, j, nj - 1)
    return pl.pallas_call(
        _moe_down_kernel,
        grid_spec=pltpu.PrefetchScalarGridSpec(
            num_scalar_prefetch=4,
            grid=(nw, nj),
            in_specs=[pl.BlockSpec((pl.Element(tm), pl.Element(d_ff)),
                                   lambda w, j, we, wr, wc, ms: (pl.multiple_of(wr[w], MOE_PAD), 0)),
                      pl.BlockSpec((1, d_ff, tn), lambda w, j, we, wr, wc, ms: (we[w], 0, jeff(w, j, wc))),
                      pl.BlockSpec((1, d_ff, tn), lambda w, j, we, wr, wc, ms: (we[w], 0, nj + jeff(w, j, wc))),
                      pl.BlockSpec((1, 1, tn), lambda w, j, we, wr, wc, ms: (we[w], 0, jeff(w, j, wc))),
                      pl.BlockSpec((1, 1, tn), lambda w, j, we, wr, wc, ms: (we[w], 0, nj + jeff(w, j, wc)))],
            out_specs=pl.BlockSpec(memory_space=pl.ANY),
            scratch_shapes=[pltpu.VMEM((2, tm, tn), U32),
                            pltpu.VMEM((MOE_PAD, half), U32),
                            pltpu.SemaphoreType.DMA((2,)), pltpu.SemaphoreType.DMA(())]),
        out_shape=jax.ShapeDtypeStruct((R, half), U32),
        compiler_params=_cparams(("arbitrary", "arbitrary")),
        name="moe_down",
    )(we, wr, wc, misc, h, w_down, w_down, b_down, b_down)


def _combine_kernel(dest_ref, eo_ref, gate_ref, x1_ref, g_ref, b_ref, o_ref, buf_ref, sem, *, tc):
    n = tc * TOP_K

    def row_copy(d, slot):
        return pltpu.make_async_copy(eo_ref.at[pl.ds(d, 1), :], buf_ref.at[pl.ds(slot, 1), :], sem)

    def issue(a, carry):
        row_copy(dest_ref[0, 0, a], (a % TOP_K) * tc + a // TOP_K).start()
        return carry

    lax.fori_loop(0, n, issue, 0)

    def drain(a, carry):
        row_copy(0, 0).wait()
        return carry

    lax.fori_loop(0, n, drain, 0)

    gates = gate_ref[...]
    half = buf_ref.shape[1]
    ylo = jnp.zeros((tc, half), F32)
    yhi = jnp.zeros((tc, half), F32)
    for k in range(TOP_K):
        u = buf_ref[k * tc:(k + 1) * tc, :]
        gk = gates[:, k:k + 1]
        ylo = ylo + gk * _unpack_lo(u)
        yhi = yhi + gk * _unpack_hi(u)
    y = DEEPNORM_ALPHA * x1_ref[...] + jnp.concatenate([ylo, yhi], axis=-1)
    o_ref[...] = _layer_norm(y, g_ref[...], b_ref[...])


def _combine(dest, eo, gates, x1, g, b):
    T, D = x1.shape
    tc = COMBINE_TC
    n = tc * TOP_K
    nt = T // tc
    half = eo.shape[1]
    row = pl.BlockSpec((tc, D), lambda i: (i, 0))
    vec = pl.BlockSpec((1, D), lambda i: (0, 0))
    return pl.pallas_call(
        functools.partial(_combine_kernel, tc=tc),
        grid=(nt,),
        in_specs=[pl.BlockSpec((1, 1, n), lambda i: (i, 0, 0), memory_space=pltpu.SMEM),
                  pl.BlockSpec(memory_space=pl.ANY),
                  pl.BlockSpec((tc, TOP_K), lambda i: (i, 0)),
                  row, vec, vec],
        out_specs=row,
        out_shape=jax.ShapeDtypeStruct((T, D), F32),
        scratch_shapes=[pltpu.VMEM((n, half), U32), pltpu.SemaphoreType.DMA(())],
        compiler_params=_cparams(("arbitrary",)),
        name="combine_rows",
    )(dest.reshape(nt, 1, n), eo, gates, x1, g, b)


def _work_items(cnt, starts, n_items):
    chunks_per_pass = MOE_TM // MOE_PAD
    gchunks = (cnt + MOE_PAD - 1) // MOE_PAD
    npass = (gchunks + chunks_per_pass - 1) // chunks_per_pass
    cum = jnp.cumsum(npass)
    total = cum[-1]
    w = jnp.arange(n_items, dtype=I32)
    e = jnp.minimum(jnp.sum((cum[None, :] <= w[:, None]).astype(I32), axis=1), N_EXPERTS - 1)
    p = w - (cum - npass)[e]
    row = starts[e] + p * MOE_TM
    nch = jnp.clip(gchunks[e] - p * chunks_per_pass, 0, chunks_per_pass)
    valid = w < total
    last = jnp.maximum(total - 1, 0)
    e = jnp.where(valid, e, e[last]).astype(I32)
    row = jnp.where(valid, row, row[last]).astype(I32)
    nch = jnp.where(valid, nch, 0).astype(I32)
    misc = jnp.sum(gchunks).astype(I32).reshape(1)
    return e, row, nch, misc


def kernel(x, mem, rel_table, w_in, w_mem_kv, w_o, lambda_q1, lambda_k1, lambda_q2, lambda_k2, subln_g, conv_w, conv_b, conv_ln_g, conv_ln_b, ln1_g, ln1_b, w_router, b_router, w_gate_up, b_gate_up, w_down, b_down, ln2_g, ln2_b):
    B, S, D = x.shape
    T = B * S
    n_heads = rel_table.shape[1]
    qk_w = n_heads * 2 * HEAD_DIM
    v_w = n_heads * V_DIM
    C = conv_w.shape[-1]
    mem_w = w_mem_kv.shape[-1] // 2
    mem_heads = 4
    mem_hd = mem_w // mem_heads
    in_cols = w_in.shape[-1]
    assert in_cols == 2 * qk_w + v_w + 2 * C + mem_w
    d_ff = w_down.shape[2]
    assert DEPTH == 1 and w_in.shape[0] == 1

    col_scale = jnp.concatenate([
        jnp.full((qk_w,), HEAD_DIM ** -0.5, F32),
        jnp.ones((in_cols - qk_w - mem_w,), F32),
        jnp.full((mem_w,), mem_hd ** -0.5, F32)]).reshape(1, in_cols)

    xt = x.reshape(T, D)
    proj = _proj_matmul(xt.astype(BF16), w_in[0], col_scale, tm=1024, tn=512, out_dtype=BF16)
    proj3 = proj.reshape(B, S, in_cols)

    bias = _bias_tiles(rel_table, ATT_T)
    lam_params = jnp.concatenate([lambda_q1, lambda_k1, lambda_q2, lambda_k2], axis=0).astype(F32)
    a_out = _diff_attention(proj3, bias, lam_params, subln_g, n_heads=n_heads,
                            k_col=qk_w // (2 * HEAD_DIM), v_col=2 * qk_w // V_DIM)

    conv_col = (2 * qk_w + v_w) // C
    c_out = _conformer_conv(proj3, conv_w[0, :, 0, :], conv_b, conv_ln_g, conv_ln_b,
                            a_col=conv_col, g_col=conv_col + 1)

    n_mem = mem.shape[1]
    kv = _proj_matmul(mem.reshape(B * n_mem, D).astype(BF16), w_mem_kv[0],
                      jnp.ones((1, 2 * mem_w), F32), tm=B * n_mem, tn=512, out_dtype=BF16)
    m_out = _memory_attention(proj3, kv.reshape(B, n_mem, 2 * mem_w), n_heads=mem_heads,
                              q_col=(in_cols - mem_w) // mem_hd, hd=mem_hd)

    mix = _out_projection(a_out.reshape(T, v_w), c_out.reshape(T, C), m_out.reshape(T, mem_w),
                          w_o[0], tm=1024, tn=512)

    x1, xp, topi, gates = _ln_router(xt, mix, ln1_g, ln1_b, w_router[0],
                                     b_router.reshape(1, -1), tm=256)

    dest, cnt8, starts8 = _rank_assignments(topi)
    cnt, starts = cnt8[0, :N_EXPERTS], starts8[0, :N_EXPERTS]

    n_assign = T * TOP_K
    n_items = n_assign // MOE_TM + N_EXPERTS
    max_rows = n_assign + N_EXPERTS * (MOE_PAD - 1) + MOE_TM
    n_rows = -(-max_rows // FILL_ROWS) * FILL_ROWS
    we, wr, wc, misc = _work_items(cnt, starts, n_items)

    xs = _dispatch(dest, xp, n_rows)
    h = _moe_up(xs, w_gate_up[0], b_gate_up[0].reshape(N_EXPERTS, 1, -1), we, wr, wc, misc, d_ff=d_ff)
    eo = _moe_down(h, w_down[0], b_down[0].reshape(N_EXPERTS, 1, -1), we, wr, wc, misc)
    out = _combine(dest, eo, gates, x1, ln2_g, ln2_b)
    return out.reshape(B, S, D)
```

```python
import functools
import math

import jax
import jax.numpy as jnp
from jax import lax
from jax.experimental import pallas as pl
from jax.experimental.pallas import tpu as pltpu

F32 = jnp.float32
BF16 = jnp.bfloat16
I32 = jnp.int32
U32 = jnp.uint32

HEAD_DIM = 128
V_DIM = 2 * HEAD_DIM
CONV_TAPS = 31
NUM_BUCKETS = 32
MAX_EXACT = NUM_BUCKETS // 2
MAX_DISTANCE = 128
N_EXPERTS = 32
TOP_K = 4
SWIGLU_LIMIT = 7.0
SWIGLU_ALPHA = 1.702
LN_EPS = 1e-5
DEPTH = 1
DEEPNORM_ALPHA = (2 * DEPTH) ** 0.25
LAM_INIT = 0.8 - 0.6 * math.exp(-0.3 * 0)

LANES = 128
V7X_VMEM_BYTES = 64 * 1024 * 1024
VMEM_LIMIT = 58 * 1024 * 1024

ATT_T = 256
CONV_TS = 256
CONV_HALO = 32
MEM_TQ = 512
MOE_TM = 1152
MOE_PAD = 128
MOE_TN_UP = 256
MOE_TN_DOWN = 512
RANK_TT = 512
DISPATCH_TA = 1024
COMBINE_TC = 128
FILL_ROWS = 1024


def _cparams(sem, vmem=VMEM_LIMIT):
    return pltpu.CompilerParams(dimension_semantics=sem, vmem_limit_bytes=vmem)


def _proj_kernel(a_ref, w_ref, cs_ref, o_ref):
    acc = jnp.dot(a_ref[...], w_ref[...].astype(BF16), preferred_element_type=F32)
    o_ref[...] = (acc * cs_ref[...]).astype(o_ref.dtype)


def _proj_matmul(a, w, col_scale, *, tm, tn, out_dtype):
    M, K = a.shape
    N = w.shape[1]
    return pl.pallas_call(
        _proj_kernel,
        grid=(M // tm, N // tn),
        in_specs=[pl.BlockSpec((tm, K), lambda i, j: (i, 0)),
                  pl.BlockSpec((K, tn), lambda i, j: (0, j)),
                  pl.BlockSpec((1, tn), lambda i, j: (0, j))],
        out_specs=pl.BlockSpec((tm, tn), lambda i, j: (i, j)),
        out_shape=jax.ShapeDtypeStruct((M, N), out_dtype),
        compiler_params=_cparams(("parallel", "arbitrary")),
        name="proj_matmul",
    )(a, w, col_scale)


def _bias_kernel(tbl_ref, o_ref, *, t, n_heads):
    h = pl.program_id(0)
    i = lax.broadcasted_iota(I32, (t, t), 0)
    j = lax.broadcasted_iota(I32, (t, t), 1)
    for o in range(2):
        dist = o * t + i - j
        n = jnp.maximum(dist, 0)
        nf = jnp.maximum(n, 1).astype(F32)
        large = MAX_EXACT + (jnp.log(nf / MAX_EXACT) / math.log(MAX_DISTANCE / MAX_EXACT)
                             * (NUM_BUCKETS - MAX_EXACT)).astype(I32)
        large = jnp.minimum(large, NUM_BUCKETS - 1)
        bucket = jnp.where(n < MAX_EXACT, n, large)
        val = jnp.zeros((t, t), F32)
        for b in range(NUM_BUCKETS):
            val = jnp.where(bucket == b, tbl_ref[b * n_heads + h], val)
        if o == 0:
            val = jnp.where(dist >= 0, val, -jnp.inf)
        o_ref[0, o] = val
    o_ref[0, 2] = jnp.full((t, t), tbl_ref[(NUM_BUCKETS - 1) * n_heads + h], F32)


def _bias_tiles(rel_table, t):
    n_heads = rel_table.shape[1]
    assert t + 1 >= MAX_DISTANCE
    return pl.pallas_call(
        functools.partial(_bias_kernel, t=t, n_heads=n_heads),
        grid=(n_heads,),
        in_specs=[pl.BlockSpec(memory_space=pltpu.SMEM)],
        out_specs=pl.BlockSpec((1, 3, t, t), lambda h: (h, 0, 0, 0)),
        out_shape=jax.ShapeDtypeStruct((n_heads, 3, t, t), F32),
        compiler_params=_cparams(("arbitrary",)),
        name="bias_tiles",
    )(rel_table.reshape(-1))


def _attn_kernel(lam_ref, q_ref, k_ref, v_ref, bias_ref, g_ref, o_ref, s_ref, acc_ref, *, t):
    qi = pl.program_id(2)
    nk = qi + 1
    q = q_ref[0]
    q1 = q[:, :HEAD_DIM]
    q2 = q[:, HEAD_DIM:]
    dn = (((1,), (1,)), ((), ()))

    def pass1(kb, carry):
        m1, m2 = carry
        k = k_ref[0, pl.ds(pl.multiple_of(kb * t, t), t), :]
        b = bias_ref[0, jnp.minimum(qi - kb, 2)]
        s1 = lax.dot_general(q1, k[:, :HEAD_DIM], dn, preferred_element_type=F32) + b
        s2 = lax.dot_general(q2, k[:, HEAD_DIM:], dn, preferred_element_type=F32) + b
        s_ref[0, kb] = s1
        s_ref[1, kb] = s2
        return (jnp.maximum(m1, jnp.max(s1, axis=-1, keepdims=True)),
                jnp.maximum(m2, jnp.max(s2, axis=-1, keepdims=True)))

    neg = jnp.full((t, 1), -jnp.inf, F32)
    m1, m2 = lax.fori_loop(0, nk, pass1, (neg, neg))

    def pass2(kb, carry):
        l1, l2 = carry
        p1 = jnp.exp(s_ref[0, kb] - m1)
        p2 = jnp.exp(s_ref[1, kb] - m2)
        s_ref[0, kb] = p1
        s_ref[1, kb] = p2
        return (l1 + jnp.sum(p1, axis=-1, keepdims=True),
                l2 + jnp.sum(p2, axis=-1, keepdims=True))

    zero = jnp.zeros((t, 1), F32)
    l1, l2 = lax.fori_loop(0, nk, pass2, (zero, zero))

    lp = lam_ref[...]
    lam = (jnp.exp(jnp.sum(lp[0:1] * lp[1:2], axis=-1, keepdims=True))
           - jnp.exp(jnp.sum(lp[2:3] * lp[3:4], axis=-1, keepdims=True)) + LAM_INIT)
    a1 = 1.0 / l1
    a2 = lam / l2

    acc_ref[...] = jnp.zeros_like(acc_ref)

    def pass3(kb, carry):
        w = s_ref[0, kb] * a1 - s_ref[1, kb] * a2
        v = v_ref[0, pl.ds(pl.multiple_of(kb * t, t), t), :]
        acc_ref[...] += jnp.dot(w.astype(BF16), v, preferred_element_type=F32)
        return carry

    lax.fori_loop(0, nk, pass3, 0)
    o = acc_ref[...]
    o = o * lax.rsqrt(jnp.mean(jnp.square(o), axis=-1, keepdims=True) + LN_EPS) * g_ref[...]
    o_ref[0] = (o * (1.0 - LAM_INIT)).astype(o_ref.dtype)


def _diff_attention(proj3, bias, lam_params, subln_g, *, n_heads, k_col, v_col):
    B, S, _ = proj3.shape
    t = ATT_T
    nq = S // t
    w = 2 * HEAD_DIM
    return pl.pallas_call(
        functools.partial(_attn_kernel, t=t),
        grid=(B, n_heads, nq),
        in_specs=[pl.BlockSpec((4, HEAD_DIM), lambda b, h, i: (0, 0)),
                  pl.BlockSpec((1, t, w), lambda b, h, i: (b, i, h)),
                  pl.BlockSpec((1, S, w), lambda b, h, i: (b, 0, k_col + h)),
                  pl.BlockSpec((1, S, w), lambda b, h, i: (b, 0, v_col + h)),
                  pl.BlockSpec((1, 3, t, t), lambda b, h, i: (h, 0, 0, 0)),
                  pl.BlockSpec((1, V_DIM), lambda b, h, i: (0, 0))],
        out_specs=pl.BlockSpec((1, t, V_DIM), lambda b, h, i: (b, i, h)),
        out_shape=jax.ShapeDtypeStruct((B, S, n_heads * V_DIM), BF16),
        scratch_shapes=[pltpu.VMEM((2, nq, t, t), F32), pltpu.VMEM((t, V_DIM), F32)],
        compiler_params=_cparams(("parallel", "parallel", "arbitrary")),
        name="diff_attention",
    )(lam_params, proj3, proj3, proj3, bias, subln_g)


def _conv_kernel(a_ref, g_ref, ap_ref, gp_ref, w_ref, cb_ref, lg_ref, lb_ref, o_ref,
                 hs_ref, cv_ref, *, ts, halo, taps, rc):
    i = pl.program_id(1)
    C = a_ref.shape[-1]
    hs_ref[halo:, :] = a_ref[0].astype(F32) * jax.nn.sigmoid(g_ref[0].astype(F32))
    hp = ap_ref[0].astype(F32) * jax.nn.sigmoid(gp_ref[0].astype(F32))
    hs_ref[:halo, :] = jnp.where(i > 0, hp, 0.0)
    base = halo - (taps - 1)
    for c in range(C // LANES):
        cs = slice(c * LANES, (c + 1) * LANES)
        for r in range(ts // rc):
            acc = jnp.zeros((rc, LANES), F32)
            for tp in range(taps):
                acc = acc + w_ref[tp:tp + 1, cs] * hs_ref[pl.ds(base + tp + r * rc, rc), cs]
            cv_ref[r * rc:(r + 1) * rc, cs] = acc + cb_ref[:, cs]
    h = cv_ref[...]
    mu = jnp.mean(h, axis=-1, keepdims=True)
    var = jnp.mean(jnp.square(h - mu), axis=-1, keepdims=True)
    y = (h - mu) * lax.rsqrt(var + LN_EPS) * lg_ref[...] + lb_ref[...]
    o_ref[0] = (y * jax.nn.sigmoid(y)).astype(o_ref.dtype)


def _conformer_conv(proj3, conv_w, conv_b, ln_g, ln_b, *, a_col, g_col):
    B, S, _ = proj3.shape
    taps, C = conv_w.shape
    ts, halo = CONV_TS, CONV_HALO
    assert halo >= taps - 1
    hb = ts // halo
    prev = lambda col: (lambda b, i: (b, jnp.maximum(i * hb - 1, 0), col))
    vec = pl.BlockSpec((1, C), lambda b, i: (0, 0))
    return pl.pallas_call(
        functools.partial(_conv_kernel, ts=ts, halo=halo, taps=taps, rc=128),
        grid=(B, S // ts),
        in_specs=[pl.BlockSpec((1, ts, C), lambda b, i: (b, i, a_col)),
                  pl.BlockSpec((1, ts, C), lambda b, i: (b, i, g_col)),
                  pl.BlockSpec((1, halo, C), prev(a_col)),
                  pl.BlockSpec((1, halo, C), prev(g_col)),
                  pl.BlockSpec((taps, C), lambda b, i: (0, 0)),
                  vec, vec, vec],
        out_specs=pl.BlockSpec((1, ts, C), lambda b, i: (b, i, 0)),
        out_shape=jax.ShapeDtypeStruct((B, S, C), BF16),
        scratch_shapes=[pltpu.VMEM((halo + ts, C), F32), pltpu.VMEM((ts, C), F32)],
        compiler_params=_cparams(("parallel", "arbitrary")),
        name="conformer_conv",
    )(proj3, proj3, proj3, proj3, conv_w, conv_b, ln_g, ln_b)


def _mem_attn_kernel(q_ref, k_ref, v_ref, o_ref):
    s = lax.dot_general(q_ref[0], k_ref[0], (((1,), (1,)), ((), ())), preferred_element_type=F32)
    m = jnp.max(s, axis=-1, keepdims=True)
    p = jnp.exp(s - m)
    p = p / jnp.sum(p, axis=-1, keepdims=True)
    o_ref[0] = jnp.dot(p.astype(BF16), v_ref[0], preferred_element_type=F32).astype(o_ref.dtype)


def _memory_attention(proj3, kv3, *, n_heads, q_col, hd):
    B, S, _ = proj3.shape
    M = kv3.shape[1]
    tq = MEM_TQ
    return pl.pallas_call(
        _mem_attn_kernel,
        grid=(B, n_heads, S // tq),
        in_specs=[pl.BlockSpec((1, tq, hd), lambda b, h, i: (b, i, q_col + h)),
                  pl.BlockSpec((1, M, hd), lambda b, h, i: (b, 0, h)),
                  pl.BlockSpec((1, M, hd), lambda b, h, i: (b, 0, n_heads + h))],
        out_specs=pl.BlockSpec((1, tq, hd), lambda b, h, i: (b, i, h)),
        out_shape=jax.ShapeDtypeStruct((B, S, n_heads * hd), BF16),
        compiler_params=_cparams(("parallel", "parallel", "arbitrary")),
        name="memory_attention",
    )(proj3, kv3, kv3)


def _oproj_kernel(a_ref, c_ref, m_ref, wa_ref, wc_ref, wm_ref, o_ref):
    acc = jnp.dot(a_ref[...], wa_ref[...].astype(BF16), preferred_element_type=F32)
    acc += jnp.dot(c_ref[...], wc_ref[...].astype(BF16), preferred_element_type=F32)
    acc += jnp.dot(m_ref[...], wm_ref[...].astype(BF16), preferred_element_type=F32)
    o_ref[...] = acc


def _out_projection(a, c, m, w_o, *, tm, tn):
    M, Ka = a.shape
    Kc, Km = c.shape[1], m.shape[1]
    assert Kc == Km and Ka % Kc == 0
    N = w_o.shape[1]
    return pl.pallas_call(
        _oproj_kernel,
        grid=(M // tm, N // tn),
        in_specs=[pl.BlockSpec((tm, Ka), lambda i, j: (i, 0)),
                  pl.BlockSpec((tm, Kc), lambda i, j: (i, 0)),
                  pl.BlockSpec((tm, Km), lambda i, j: (i, 0)),
                  pl.BlockSpec((Ka, tn), lambda i, j: (0, j)),
                  pl.BlockSpec((Kc, tn), lambda i, j: (Ka // Kc, j)),
                  pl.BlockSpec((Km, tn), lambda i, j: (Ka // Kc + 1, j))],
        out_specs=pl.BlockSpec((tm, tn), lambda i, j: (i, j)),
        out_shape=jax.ShapeDtypeStruct((M, N), F32),
        compiler_params=_cparams(("parallel", "arbitrary")),
        name="out_projection",
    )(a, c, m, w_o, w_o, w_o)


def _pack_pairs(lo, hi):
    lo_b = lax.bitcast_convert_type(lo.astype(BF16).astype(F32), U32)
    hi_b = lax.bitcast_convert_type(hi.astype(BF16).astype(F32), U32)
    return (lo_b >> 16) | (hi_b & jnp.uint32(0xFFFF0000))


def _unpack_lo(u):
    return lax.bitcast_convert_type(u << 16, F32)


def _unpack_hi(u):
    return lax.bitcast_convert_type(u & jnp.uint32(0xFFFF0000), F32)


def _layer_norm(y, g, b):
    mu = jnp.mean(y, axis=-1, keepdims=True)
    var = jnp.mean(jnp.square(y - mu), axis=-1, keepdims=True)
    return (y - mu) * lax.rsqrt(var + LN_EPS) * g + b


def _ln_router_kernel(x_ref, mix_ref, g_ref, b_ref, wr_ref, br_ref,
                      x1_ref, xp_ref, topi_ref, gate_ref):
    x1 = _layer_norm(DEEPNORM_ALPHA * x_ref[...] + mix_ref[...], g_ref[...], b_ref[...])
    x1_ref[...] = x1
    half = x1.shape[1] // 2
    xp_ref[...] = _pack_pairs(x1[:, :half], x1[:, half:])
    logits = jnp.dot(x1, wr_ref[...], preferred_element_type=F32,
                     precision=lax.Precision.HIGHEST) + br_ref[...]
    tm, ne = logits.shape
    lane = lax.broadcasted_iota(I32, (tm, ne), 1)
    kk = lax.broadcasted_iota(I32, (tm, TOP_K), 1)
    idx = jnp.zeros((tm, TOP_K), I32)
    val = jnp.zeros((tm, TOP_K), F32)
    cur = logits
    for k in range(TOP_K):
        mx = jnp.max(cur, axis=-1, keepdims=True)
        ix = jnp.min(jnp.where(cur == mx, lane, ne), axis=-1, keepdims=True)
        idx = jnp.where(kk == k, ix, idx)
        val = jnp.where(kk == k, mx, val)
        cur = jnp.where(lane == ix, -jnp.inf, cur)
    e = jnp.exp(val - jnp.max(val, axis=-1, keepdims=True))
    topi_ref[...] = idx
    gate_ref[...] = e / jnp.sum(e, axis=-1, keepdims=True)


def _ln_router(x, mix, g, b, w_router, b_router, *, tm):
    T, D = x.shape
    ne = w_router.shape[1]
    row = pl.BlockSpec((tm, D), lambda i: (i, 0))
    vec = pl.BlockSpec((1, D), lambda i: (0, 0))
    return pl.pallas_call(
        _ln_router_kernel,
        grid=(T // tm,),
        in_specs=[row, row, vec, vec,
                  pl.BlockSpec((D, ne), lambda i: (0, 0)),
                  pl.BlockSpec((1, ne), lambda i: (0, 0))],
        out_specs=[row,
                   pl.BlockSpec((tm, D // 2), lambda i: (i, 0)),
                   pl.BlockSpec((tm, TOP_K), lambda i: (i, 0)),
                   pl.BlockSpec((tm, TOP_K), lambda i: (i, 0))],
        out_shape=[jax.ShapeDtypeStruct((T, D), F32),
                   jax.ShapeDtypeStruct((T, D // 2), U32),
                   jax.ShapeDtypeStruct((T, TOP_K), I32),
                   jax.ShapeDtypeStruct((T, TOP_K), F32)],
        compiler_params=_cparams(("parallel",)),
        name="ln_router",
    )(x, mix, g, b, w_router, b_router)


def _rank_kernel(topi_ref, dest_ref, cnt_ref, ps_ref, rank_ref, carry_ref, *, tt, pad):
    ph = pl.program_id(0)
    i = pl.program_id(1)
    lane = lax.broadcasted_iota(I32, (tt, LANES), 1)
    kk = lax.broadcasted_iota(I32, (tt, TOP_K), 1)
    ti = topi_ref[...]
    sel = [lane == ti[:, k:k + 1] for k in range(TOP_K)]
    rows = pl.ds(pl.multiple_of(i * tt, tt), tt)

    @pl.when(ph == 0)
    def _():
        @pl.when(i == 0)
        def _():
            carry_ref[...] = jnp.zeros_like(carry_ref)

        oh = jnp.zeros((tt, LANES), F32)
        for k in range(TOP_K):
            oh = oh + sel[k].astype(F32)
        r = lax.broadcasted_iota(I32, (tt, tt), 0)
        c = lax.broadcasted_iota(I32, (tt, tt), 1)
        lower = (r > c).astype(BF16)
        before = jnp.dot(lower, oh.astype(BF16), preferred_element_type=F32) + carry_ref[0:1, :]
        rank = jnp.zeros((tt, TOP_K), F32)
        for k in range(TOP_K):
            rk = jnp.sum(jnp.where(sel[k], before, 0.0), axis=-1, keepdims=True)
            rank = jnp.where(kk == k, rk, rank)
        rank_ref[rows, :] = rank
        carry_ref[...] = carry_ref[...] + jnp.sum(oh, axis=0, keepdims=True)

    @pl.when(ph == 1)
    def _():
        cnt = carry_ref[...].astype(I32)
        padded = (cnt + (pad - 1)) & jnp.int32(-pad)
        l8 = lax.broadcasted_iota(I32, (8, LANES), 1)
        scan = padded
        for sh in (1, 2, 4, 8, 16, 32, 64):
            scan = scan + jnp.where(l8 >= sh, pltpu.roll(scan, sh, axis=1), 0)
        starts = scan - padded
        cnt_ref[...] = cnt
        ps_ref[...] = starts
        st = starts[0:1, :].astype(F32)
        rank = rank_ref[rows, :]
        dest = jnp.zeros((tt, TOP_K), F32)
        for k in range(TOP_K):
            base = jnp.sum(jnp.where(sel[k], st, 0.0), axis=-1, keepdims=True)
            dest = jnp.where(kk == k, base, dest)
        dest_ref[...] = (dest + rank).astype(I32)


def _rank_assignments(topi):
    T = topi.shape[0]
    tt = RANK_TT
    return pl.pallas_call(
        functools.partial(_rank_kernel, tt=tt, pad=MOE_PAD),
        grid=(2, T // tt),
        in_specs=[pl.BlockSpec((tt, TOP_K), lambda p, i: (i, 0))],
        out_specs=[pl.BlockSpec((tt, TOP_K), lambda p, i: (i * p, 0)),
                   pl.BlockSpec((8, LANES), lambda p, i: (0, 0)),
                   pl.BlockSpec((8, LANES), lambda p, i: (0, 0))],
        out_shape=[jax.ShapeDtypeStruct((T, TOP_K), I32),
                   jax.ShapeDtypeStruct((8, LANES), I32),
                   jax.ShapeDtypeStruct((8, LANES), I32)],
        scratch_shapes=[pltpu.VMEM((T, TOP_K), F32), pltpu.VMEM((8, LANES), F32)],
        compiler_params=_cparams(("arbitrary", "arbitrary")),
        name="rank_assignments",
    )(topi)


def _dispatch_kernel(dest_ref, xp_ref, xs_ref, zero_ref, fill_sem, row_sem, *, ta, n_fill):
    i = pl.program_id(0)
    tt = xp_ref.shape[0]

    @pl.when(i == 0)
    def _():
        zero_ref[...] = jnp.zeros_like(zero_ref)
        fr = zero_ref.shape[0]

        def fill(f):
            return pltpu.make_async_copy(zero_ref, xs_ref.at[pl.ds(f * fr, fr), :], fill_sem)

        for f in range(n_fill):
            fill(f).start()
        for f in range(n_fill):
            fill(f).wait()

    def issue(a, carry):
        pltpu.make_async_copy(xp_ref.at[pl.ds(a // TOP_K, 1), :],
                              xs_ref.at[pl.ds(dest_ref[0, 0, a], 1), :], row_sem).start()
        return carry

    lax.fori_loop(0, ta, issue, 0, unroll=8)
    for _ in range(ta // tt):
        pltpu.make_async_copy(xp_ref, xs_ref.at[pl.ds(0, tt), :], row_sem).wait()


def _dispatch(dest, xp, n_rows):
    T, W = xp.shape
    ta = DISPATCH_TA
    na = dest.size // ta
    assert n_rows % FILL_ROWS == 0 and ta % TOP_K == 0
    return pl.pallas_call(
        functools.partial(_dispatch_kernel, ta=ta, n_fill=n_rows // FILL_ROWS),
        grid=(na,),
        in_specs=[pl.BlockSpec((1, 1, ta), lambda i: (i, 0, 0), memory_space=pltpu.SMEM),
                  pl.BlockSpec((ta // TOP_K, W), lambda i: (i, 0))],
        out_specs=pl.BlockSpec(memory_space=pl.ANY),
        out_shape=jax.ShapeDtypeStruct((n_rows, W), U32),
        scratch_shapes=[pltpu.VMEM((FILL_ROWS, W), U32),
                        pltpu.SemaphoreType.DMA(()), pltpu.SemaphoreType.DMA(())],
        compiler_params=_cparams(("arbitrary",)),
        name="dispatch_rows",
    )(dest.reshape(na, 1, ta), xp)


class _TileWriter:
    def __init__(self, wr_ref, wc_ref, stage_ref, out_ref, sem, tn):
        self.wr_ref, self.wc_ref = wr_ref, wc_ref
        self.stage_ref, self.out_ref, self.sem, self.tn = stage_ref, out_ref, sem, tn
        self.n_chunks = stage_ref.shape[1] // MOE_PAD

    def _copy(self, slot, w, j, c):
        row = pl.multiple_of(self.wr_ref[w] + c * MOE_PAD, MOE_PAD)
        col = pl.multiple_of(j * self.tn, self.tn)
        return pltpu.make_async_copy(
            self.stage_ref.at[slot, pl.ds(c * MOE_PAD, MOE_PAD), :],
            self.out_ref.at[pl.ds(row, MOE_PAD), pl.ds(col, self.tn)],
            self.sem.at[slot])

    def _each_chunk(self, w, fn):
        for c in range(self.n_chunks):
            pl.when(c < self.wc_ref[w])(functools.partial(fn, c))

    def start(self, slot, w, j):
        self._each_chunk(w, lambda c: self._copy(slot, w, j, c).start())

    def wait(self, slot, w, j):
        self._each_chunk(w, lambda c: self._copy(slot, w, j, c).wait())

    def retire_older(self, w, j, nj):
        step = w * nj + j

        @pl.when(step >= 2)
        def _():
            wrap = j < 2
            self.wait(step % 2, jnp.where(wrap, w - 1, w), jnp.where(wrap, j + nj - 2, j - 2))

    def drain(self, w, j, nw, nj):
        step = w * nj + j

        @pl.when(step == nw * nj - 1)
        def _():
            self.wait((step + 1) % 2, w, j - 1)
            self.wait(step % 2, w, j)


def _fill_tail(misc_ref, zero_ref, out_ref, sem):
    zero_ref[...] = jnp.zeros_like(zero_ref)
    n_total = out_ref.shape[0] // MOE_PAD

    def fill(c):
        row = pl.multiple_of(c * MOE_PAD, MOE_PAD)
        return pltpu.make_async_copy(zero_ref, out_ref.at[pl.ds(row, MOE_PAD), :], sem)

    def start(c, carry):
        fill(c).start()
        return carry

    def wait(c, carry):
        fill(c).wait()
        return carry

    lax.fori_loop(misc_ref[0], n_total, start, 0)
    lax.fori_loop(misc_ref[0], n_total, wait, 0)


def _moe_up_kernel(we_ref, wr_ref, wc_ref, misc_ref, xs_ref, wg_ref, wu_ref, bg_ref, bu_ref,
                   h_ref, xb_ref, wcat_ref, stage_ref, zero_ref, sem, fill_sem):
    w = pl.program_id(0)
    j = pl.program_id(1)
    nw = pl.num_programs(0)
    nj = pl.num_programs(1)
    slot = (w * nj + j) % 2
    writer = _TileWriter(wr_ref, wc_ref, stage_ref, h_ref, sem, stage_ref.shape[2])

    @pl.when((w == 0) & (j == 0))
    def _():
        _fill_tail(misc_ref, zero_ref, h_ref, fill_sem)

    writer.retire_older(w, j, nj)

    @pl.when(wc_ref[w] > 0)
    def _():
        half = xs_ref.shape[1]

        @pl.when(j == 0)
        def _():
            u = xs_ref[...]
            xb_ref[:, :half] = _unpack_lo(u).astype(BF16)
            xb_ref[:, half:] = _unpack_hi(u).astype(BF16)

        tn = wg_ref.shape[2]
        wcat_ref[:, :tn] = wg_ref[0].astype(BF16)
        wcat_ref[:, tn:] = wu_ref[0].astype(BF16)
        r = jnp.dot(xb_ref[...], wcat_ref[...], preferred_element_type=F32)
        gate = r[:, :tn] + bg_ref[0]
        up = r[:, tn:] + bu_ref[0]
        gate = jnp.minimum(gate, SWIGLU_LIMIT)
        up = jnp.clip(up, -SWIGLU_LIMIT, SWIGLU_LIMIT)
        act = (up + 1.0) * (gate * jax.nn.sigmoid(SWIGLU_ALPHA * gate))
        stage_ref[slot] = act.astype(stage_ref.dtype)
        writer.start(slot, w, j)

    writer.drain(w, j, nw, nj)


def _moe_up(xs, w_gu, b_gu, we, wr, wc, misc, *, d_ff):
    R, half = xs.shape
    E, D, _ = w_gu.shape
    tm, tn = MOE_TM, MOE_TN_UP
    nj = d_ff // tn
    nw = we.shape[0]
    assert nj >= 2 and R % MOE_PAD == 0
    jeff = lambda w, j, wc: jnp.where(wc[w] > 0, j, nj - 1)
    return pl.pallas_call(
        _moe_up_kernel,
        grid_spec=pltpu.PrefetchScalarGridSpec(
            num_scalar_prefetch=4,
            grid=(nw, nj),
            in_specs=[pl.BlockSpec((pl.Element(tm), pl.Element(half)),
                                   lambda w, j, we, wr, wc, ms: (pl.multiple_of(wr[w], MOE_PAD), 0)),
                      pl.BlockSpec((1, D, tn), lambda w, j, we, wr, wc, ms: (we[w], 0, jeff(w, j, wc))),
                      pl.BlockSpec((1, D, tn), lambda w, j, we, wr, wc, ms: (we[w], 0, nj + jeff(w, j, wc))),
                      pl.BlockSpec((1, 1, tn), lambda w, j, we, wr, wc, ms: (we[w], 0, jeff(w, j, wc))),
                      pl.BlockSpec((1, 1, tn), lambda w, j, we, wr, wc, ms: (we[w], 0, nj + jeff(w, j, wc)))],
            out_specs=pl.BlockSpec(memory_space=pl.ANY),
            scratch_shapes=[pltpu.VMEM((tm, 2 * half), BF16),
                            pltpu.VMEM((D, 2 * tn), BF16),
                            pltpu.VMEM((2, tm, tn), BF16),
                            pltpu.VMEM((MOE_PAD, d_ff), BF16),
                            pltpu.SemaphoreType.DMA((2,)), pltpu.SemaphoreType.DMA(())]),
        out_shape=jax.ShapeDtypeStruct((R, d_ff), BF16),
        compiler_params=_cparams(("arbitrary", "arbitrary")),
        name="moe_up",
    )(we, wr, wc, misc, xs, w_gu, w_gu, b_gu, b_gu)


def _moe_down_kernel(we_ref, wr_ref, wc_ref, misc_ref, h_ref, wlo_ref, whi_ref, blo_ref, bhi_ref,
                     o_ref, wcat_ref, stage_ref, zero_ref, sem, fill_sem):
    w = pl.program_id(0)
    j = pl.program_id(1)
    nw = pl.num_programs(0)
    nj = pl.num_programs(1)
    slot = (w * nj + j) % 2
    writer = _TileWriter(wr_ref, wc_ref, stage_ref, o_ref, sem, stage_ref.shape[2])

    @pl.when((w == 0) & (j == 0))
    def _():
        _fill_tail(misc_ref, zero_ref, o_ref, fill_sem)

    writer.retire_older(w, j, nj)

    @pl.when(wc_ref[w] > 0)
    def _():
        tn = wlo_ref.shape[2]
        wcat_ref[:, :tn] = wlo_ref[0].astype(BF16)
        wcat_ref[:, tn:] = whi_ref[0].astype(BF16)
        r = jnp.dot(h_ref[...], wcat_ref[...], preferred_element_type=F32)
        stage_ref[slot] = _pack_pairs(r[:, :tn] + blo_ref[0], r[:, tn:] + bhi_ref[0])
        writer.start(slot, w, j)

    writer.drain(w, j, nw, nj)


def _moe_down(h, w_down, b_down, we, wr, wc, misc):
    R, d_ff = h.shape
    E, _, D = w_down.shape
    tm, tn = MOE_TM, MOE_TN_DOWN
    half = D // 2
    nj = half // tn
    nw = we.shape[0]
    assert nj >= 2 and R % MOE_PAD == 0
    jeff = lambda w, j, wc: jnp.where(wc[w] > 0, j, nj - 1)
    return pl.pallas_call(
        _moe_down_kernel,
        grid_spec=pltpu.PrefetchScalarGridSpec(
            num_scalar_prefetch=4,
            grid=(nw, nj),
            in_specs=[pl.BlockSpec((pl.Element(tm), pl.Element(d_ff)),
                                   lambda w, j, we, wr, wc, ms: (pl.multiple_of(wr[w], MOE_PAD), 0)),
                      pl.BlockSpec((1, d_ff, tn), lambda w, j, we, wr, wc, ms: (we[w], 0, jeff(w, j, wc))),
                      pl.BlockSpec((1, d_ff, tn), lambda w, j, we, wr, wc, ms: (we[w], 0, nj + jeff(w, j, wc))),
                      pl.BlockSpec((1, 1, tn), lambda w, j, we, wr, wc, ms: (we[w], 0, jeff(w, j, wc))),
                      pl.BlockSpec((1, 1, tn), lambda w, j, we, wr, wc, ms: (we[w], 0, nj + jeff(w, j, wc)))],
            out_specs=pl.BlockSpec(memory_space=pl.ANY),
            scratch_shapes=[pltpu.VMEM((d_ff, 2 * tn), BF16),
                            pltpu.VMEM((2, tm, tn), U32),
                            pltpu.VMEM((MOE_PAD, half), U32),
                            pltpu.SemaphoreType.DMA((2,)), pltpu.SemaphoreType.DMA(())]),
        out_shape=jax.ShapeDtypeStruct((R, half), U32),
        compiler_params=_cparams(("arbitrary", "arbitrary")),
        name="moe_down",
    )(we, wr, wc, misc, h, w_down, w_down, b_down, b_down)


def _combine_kernel(dest_ref, eo_ref, gate_ref, x1_ref, g_ref, b_ref, o_ref, buf_ref, sem, *, tc):
    n = tc * TOP_K

    def row_copy(d, slot):
        return pltpu.make_async_copy(eo_ref.at[pl.ds(d, 1), :], buf_ref.at[pl.ds(slot, 1), :], sem)

    def issue(a, carry):
        row_copy(dest_ref[0, 0, a], (a % TOP_K) * tc + a // TOP_K).start()
        return carry

    lax.fori_loop(0, n, issue, 0, unroll=8)
    pltpu.make_async_copy(eo_ref.at[pl.ds(0, n), :], buf_ref, sem).wait()

    gates = gate_ref[...]
    half = buf_ref.shape[1]
    ylo = jnp.zeros((tc, half), F32)
    yhi = jnp.zeros((tc, half), F32)
    for k in range(TOP_K):
        u = buf_ref[k * tc:(k + 1) * tc, :]
        gk = gates[:, k:k + 1]
        ylo = ylo + gk * _unpack_lo(u)
        yhi = yhi + gk * _unpack_hi(u)
    y = DEEPNORM_ALPHA * x1_ref[...] + jnp.concatenate([ylo, yhi], axis=-1)
    o_ref[...] = _layer_norm(y, g_ref[...], b_ref[...])


def _combine(dest, eo, gates, x1, g, b):
    T, D = x1.shape
    tc = COMBINE_TC
    n = tc * TOP_K
    nt = T // tc
    half = eo.shape[1]
    row = pl.BlockSpec((tc, D), lambda i: (i, 0))
    vec = pl.BlockSpec((1, D), lambda i: (0, 0))
    return pl.pallas_call(
        functools.partial(_combine_kernel, tc=tc),
        grid=(nt,),
        in_specs=[pl.BlockSpec((1, 1, n), lambda i: (i, 0, 0), memory_space=pltpu.SMEM),
                  pl.BlockSpec(memory_space=pl.ANY),
                  pl.BlockSpec((tc, TOP_K), lambda i: (i, 0)),
                  row, vec, vec],
        out_specs=row,
        out_shape=jax.ShapeDtypeStruct((T, D), F32),
        scratch_shapes=[pltpu.VMEM((n, half), U32), pltpu.SemaphoreType.DMA(())],
        compiler_params=_cparams(("arbitrary",)),
        name="combine_rows",
    )(dest.reshape(nt, 1, n), eo, gates, x1, g, b)


def _work_items(cnt, starts, n_items):
    chunks_per_pass = MOE_TM // MOE_PAD
    gchunks = (cnt + MOE_PAD - 1) // MOE_PAD
    npass = (gchunks + chunks_per_pass - 1) // chunks_per_pass
    cum = jnp.cumsum(npass)
    total = cum[-1]
    w = jnp.arange(n_items, dtype=I32)
    e = jnp.minimum(jnp.sum((cum[None, :] <= w[:, None]).astype(I32), axis=1), N_EXPERTS - 1)
    p = w - (cum - npass)[e]
    row = starts[e] + p * MOE_TM
    nch = jnp.clip(gchunks[e] - p * chunks_per_pass, 0, chunks_per_pass)
    valid = w < total
    last = jnp.maximum(total - 1, 0)
    e = jnp.where(valid, e, e[last]).astype(I32)
    row = jnp.where(valid, row, row[last]).astype(I32)
    nch = jnp.where(valid, nch, 0).astype(I32)
    misc = jnp.sum(gchunks).astype(I32).reshape(1)
    return e, row, nch, misc


def kernel(x, mem, rel_table, w_in, w_mem_kv, w_o, lambda_q1, lambda_k1, lambda_q2, lambda_k2, subln_g, conv_w, conv_b, conv_ln_g, conv_ln_b, ln1_g, ln1_b, w_router, b_router, w_gate_up, b_gate_up, w_down, b_down, ln2_g, ln2_b):
    B, S, D = x.shape
    T = B * S
    n_heads = rel_table.shape[1]
    qk_w = n_heads * 2 * HEAD_DIM
    v_w = n_heads * V_DIM
    C = conv_w.shape[-1]
    mem_w = w_mem_kv.shape[-1] // 2
    mem_heads = 4
    mem_hd = mem_w // mem_heads
    in_cols = w_in.shape[-1]
    assert in_cols == 2 * qk_w + v_w + 2 * C + mem_w
    d_ff = w_down.shape[2]
    assert DEPTH == 1 and w_in.shape[0] == 1

    col_scale = jnp.concatenate([
        jnp.full((qk_w,), HEAD_DIM ** -0.5, F32),
        jnp.ones((in_cols - qk_w - mem_w,), F32),
        jnp.full((mem_w,), mem_hd ** -0.5, F32)]).reshape(1, in_cols)

    xt = x.reshape(T, D)
    proj = _proj_matmul(xt.astype(BF16), w_in[0], col_scale, tm=1024, tn=512, out_dtype=BF16)
    proj3 = proj.reshape(B, S, in_cols)

    bias = _bias_tiles(rel_table, ATT_T)
    lam_params = jnp.concatenate([lambda_q1, lambda_k1, lambda_q2, lambda_k2], axis=0).astype(F32)
    a_out = _diff_attention(proj3, bias, lam_params, subln_g, n_heads=n_heads,
                            k_col=qk_w // (2 * HEAD_DIM), v_col=2 * qk_w // V_DIM)

    conv_col = (2 * qk_w + v_w) // C
    c_out = _conformer_conv(proj3, conv_w[0, :, 0, :], conv_b, conv_ln_g, conv_ln_b,
                            a_col=conv_col, g_col=conv_col + 1)

    n_mem = mem.shape[1]
    kv = _proj_matmul(mem.reshape(B * n_mem, D).astype(BF16), w_mem_kv[0],
                      jnp.ones((1, 2 * mem_w), F32), tm=B * n_mem, tn=512, out_dtype=BF16)
    m_out = _memory_attention(proj3, kv.reshape(B, n_mem, 2 * mem_w), n_heads=mem_heads,
                              q_col=(in_cols - mem_w) // mem_hd, hd=mem_hd)

    mix = _out_projection(a_out.reshape(T, v_w), c_out.reshape(T, C), m_out.reshape(T, mem_w),
                          w_o[0], tm=1024, tn=512)

    x1, xp, topi, gates = _ln_router(xt, mix, ln1_g, ln1_b, w_router[0],
                                     b_router.reshape(1, -1), tm=256)

    dest, cnt8, starts8 = _rank_assignments(topi)
    cnt, starts = cnt8[0, :N_EXPERTS], starts8[0, :N_EXPERTS]

    n_assign = T * TOP_K
    n_items = n_assign // MOE_TM + N_EXPERTS
    max_rows = n_assign + N_EXPERTS * (MOE_PAD - 1) + MOE_TM
    n_rows = -(-max_rows // FILL_ROWS) * FILL_ROWS
    we, wr, wc, misc = _work_items(cnt, starts, n_items)

    xs = _dispatch(dest, xp, n_rows)
    h = _moe_up(xs, w_gate_up[0], b_gate_up[0].reshape(N_EXPERTS, 1, -1), we, wr, wc, misc, d_ff=d_ff)
    eo = _moe_down(h, w_down[0], b_down[0].reshape(N_EXPERTS, 1, -1), we, wr, wc, misc)
    out = _combine(dest, eo, gates, x1, ln2_g, ln2_b)
    return out.reshape(B, S, D)
```

```python
import functools
import math

import jax
import jax.numpy as jnp
from jax import lax
from jax.experimental import pallas as pl
from jax.experimental.pallas import tpu as pltpu

F32 = jnp.float32
BF16 = jnp.bfloat16
I32 = jnp.int32
U32 = jnp.uint32

HEAD_DIM = 128
V_DIM = 2 * HEAD_DIM
CONV_TAPS = 31
NUM_BUCKETS = 32
MAX_EXACT = NUM_BUCKETS // 2
MAX_DISTANCE = 128
N_EXPERTS = 32
TOP_K = 4
SWIGLU_LIMIT = 7.0
SWIGLU_ALPHA = 1.702
LN_EPS = 1e-5
DEPTH = 1
DEEPNORM_ALPHA = (2 * DEPTH) ** 0.25
LAM_INIT = 0.8 - 0.6 * math.exp(-0.3 * 0)

LANES = 128
V7X_VMEM_BYTES = 64 * 1024 * 1024
VMEM_LIMIT = 58 * 1024 * 1024

ATT_T = 256
CONV_TS = 256
CONV_HALO = 32
MEM_TQ = 512
MOE_TM = 1152
MOE_PAD = 128
MOE_TN_UP = 256
MOE_TN_DOWN = 512
RANK_TT = 512
DISPATCH_TA = 1024
COMBINE_TC = 128


def _cparams(sem, vmem=VMEM_LIMIT):
    return pltpu.CompilerParams(dimension_semantics=sem, vmem_limit_bytes=vmem)


def _proj_kernel(a_ref, w_ref, cs_ref, o_ref):
    acc = jnp.dot(a_ref[...], w_ref[...].astype(BF16), preferred_element_type=F32)
    o_ref[...] = (acc * cs_ref[...]).astype(o_ref.dtype)


def _proj_matmul(a, w, col_scale, *, tm, tn, out_dtype):
    M, K = a.shape
    N = w.shape[1]
    return pl.pallas_call(
        _proj_kernel,
        grid=(M // tm, N // tn),
        in_specs=[pl.BlockSpec((tm, K), lambda i, j: (i, 0)),
                  pl.BlockSpec((K, tn), lambda i, j: (0, j)),
                  pl.BlockSpec((1, tn), lambda i, j: (0, j))],
        out_specs=pl.BlockSpec((tm, tn), lambda i, j: (i, j)),
        out_shape=jax.ShapeDtypeStruct((M, N), out_dtype),
        compiler_params=_cparams(("parallel", "arbitrary")),
        name="proj_matmul",
    )(a, w, col_scale)


def _bias_kernel(tbl_ref, o_ref, *, t, n_heads):
    h = pl.program_id(0)
    i = lax.broadcasted_iota(I32, (t, t), 0)
    j = lax.broadcasted_iota(I32, (t, t), 1)
    for o in range(2):
        dist = o * t + i - j
        n = jnp.maximum(dist, 0)
        nf = jnp.maximum(n, 1).astype(F32)
        large = MAX_EXACT + (jnp.log(nf / MAX_EXACT) / math.log(MAX_DISTANCE / MAX_EXACT)
                             * (NUM_BUCKETS - MAX_EXACT)).astype(I32)
        large = jnp.minimum(large, NUM_BUCKETS - 1)
        bucket = jnp.where(n < MAX_EXACT, n, large)
        val = jnp.zeros((t, t), F32)
        for b in range(NUM_BUCKETS):
            val = jnp.where(bucket == b, tbl_ref[b * n_heads + h], val)
        if o == 0:
            val = jnp.where(dist >= 0, val, -jnp.inf)
        o_ref[0, o] = val
    o_ref[0, 2] = jnp.full((t, t), tbl_ref[(NUM_BUCKETS - 1) * n_heads + h], F32)


def _bias_tiles(rel_table, t):
    n_heads = rel_table.shape[1]
    assert t + 1 >= MAX_DISTANCE
    return pl.pallas_call(
        functools.partial(_bias_kernel, t=t, n_heads=n_heads),
        grid=(n_heads,),
        in_specs=[pl.BlockSpec(memory_space=pltpu.SMEM)],
        out_specs=pl.BlockSpec((1, 3, t, t), lambda h: (h, 0, 0, 0)),
        out_shape=jax.ShapeDtypeStruct((n_heads, 3, t, t), F32),
        compiler_params=_cparams(("arbitrary",)),
        name="bias_tiles",
    )(rel_table.reshape(-1))


def _attn_kernel(lam_ref, q_ref, k_ref, v_ref, bias_ref, g_ref, o_ref, acc_ref, *, t):
    qi = pl.program_id(2)
    q = q_ref[0]
    qs = (q[:, :HEAD_DIM], q[:, HEAD_DIM:])
    dn = (((1,), (1,)), ((), ()))
    acc_ref[...] = jnp.zeros_like(acc_ref)

    def step(kb, carry):
        rows = pl.ds(pl.multiple_of(kb * t, t), t)
        k = k_ref[0, rows, :]
        v = v_ref[0, rows, :]
        b = bias_ref[0, jnp.minimum(qi - kb, 2)]
        out = []
        for c in range(2):
            m, l = carry[2 * c], carry[2 * c + 1]
            s = lax.dot_general(qs[c], k[:, c * HEAD_DIM:(c + 1) * HEAD_DIM], dn,
                                preferred_element_type=F32) + b
            m_new = jnp.maximum(m, jnp.max(s, axis=-1, keepdims=True))
            alpha = jnp.exp(m - m_new)
            p = jnp.exp(s - m_new)
            acc_ref[c] = alpha * acc_ref[c] + jnp.dot(p.astype(BF16), v, preferred_element_type=F32)
            out += [m_new, alpha * l + jnp.sum(p, axis=-1, keepdims=True)]
        return tuple(out)

    neg = jnp.full((t, 1), -jnp.inf, F32)
    zero = jnp.zeros((t, 1), F32)
    _, l1, _, l2 = lax.fori_loop(0, qi + 1, step, (neg, zero, neg, zero))

    lp = lam_ref[...]
    lam = (jnp.exp(jnp.sum(lp[0:1] * lp[1:2], axis=-1, keepdims=True))
           - jnp.exp(jnp.sum(lp[2:3] * lp[3:4], axis=-1, keepdims=True)) + LAM_INIT)
    o = acc_ref[0] * (1.0 / l1) - acc_ref[1] * (lam / l2)
    o = o * lax.rsqrt(jnp.mean(jnp.square(o), axis=-1, keepdims=True) + LN_EPS) * g_ref[...]
    o_ref[0] = (o * (1.0 - LAM_INIT)).astype(o_ref.dtype)


def _diff_attention(proj3, bias, lam_params, subln_g, *, n_heads, k_col, v_col):
    B, S, _ = proj3.shape
    t = ATT_T
    nq = S // t
    w = 2 * HEAD_DIM
    return pl.pallas_call(
        functools.partial(_attn_kernel, t=t),
        grid=(B, n_heads, nq),
        in_specs=[pl.BlockSpec((4, HEAD_DIM), lambda b, h, i: (0, 0)),
                  pl.BlockSpec((1, t, w), lambda b, h, i: (b, i, h)),
                  pl.BlockSpec((1, S, w), lambda b, h, i: (b, 0, k_col + h)),
                  pl.BlockSpec((1, S, w), lambda b, h, i: (b, 0, v_col + h)),
                  pl.BlockSpec((1, 3, t, t), lambda b, h, i: (h, 0, 0, 0)),
                  pl.BlockSpec((1, V_DIM), lambda b, h, i: (0, 0))],
        out_specs=pl.BlockSpec((1, t, V_DIM), lambda b, h, i: (b, i, h)),
        out_shape=jax.ShapeDtypeStruct((B, S, n_heads * V_DIM), BF16),
        scratch_shapes=[pltpu.VMEM((2, t, V_DIM), F32)],
        compiler_params=_cparams(("parallel", "parallel", "arbitrary")),
        name="diff_attention",
    )(lam_params, proj3, proj3, proj3, bias, subln_g)


def _conv_kernel(a_ref, g_ref, ap_ref, gp_ref, w_ref, cb_ref, lg_ref, lb_ref, o_ref,
                 hs_ref, cv_ref, *, ts, halo, taps, rc):
    i = pl.program_id(1)
    C = a_ref.shape[-1]
    hs_ref[halo:, :] = a_ref[0].astype(F32) * jax.nn.sigmoid(g_ref[0].astype(F32))
    hp = ap_ref[0].astype(F32) * jax.nn.sigmoid(gp_ref[0].astype(F32))
    hs_ref[:halo, :] = jnp.where(i > 0, hp, 0.0)
    base = halo - (taps - 1)
    for c in range(C // LANES):
        cs = slice(c * LANES, (c + 1) * LANES)
        for r in range(ts // rc):
            acc = jnp.zeros((rc, LANES), F32)
            for tp in range(taps):
                acc = acc + w_ref[tp:tp + 1, cs] * hs_ref[pl.ds(base + tp + r * rc, rc), cs]
            cv_ref[r * rc:(r + 1) * rc, cs] = acc + cb_ref[:, cs]
    h = cv_ref[...]
    mu = jnp.mean(h, axis=-1, keepdims=True)
    var = jnp.mean(jnp.square(h - mu), axis=-1, keepdims=True)
    y = (h - mu) * lax.rsqrt(var + LN_EPS) * lg_ref[...] + lb_ref[...]
    o_ref[0] = (y * jax.nn.sigmoid(y)).astype(o_ref.dtype)


def _conformer_conv(proj3, conv_w, conv_b, ln_g, ln_b, *, a_col, g_col):
    B, S, _ = proj3.shape
    taps, C = conv_w.shape
    ts, halo = CONV_TS, CONV_HALO
    assert halo >= taps - 1
    hb = ts // halo
    prev = lambda col: (lambda b, i: (b, jnp.maximum(i * hb - 1, 0), col))
    vec = pl.BlockSpec((1, C), lambda b, i: (0, 0))
    return pl.pallas_call(
        functools.partial(_conv_kernel, ts=ts, halo=halo, taps=taps, rc=128),
        grid=(B, S // ts),
        in_specs=[pl.BlockSpec((1, ts, C), lambda b, i: (b, i, a_col)),
                  pl.BlockSpec((1, ts, C), lambda b, i: (b, i, g_col)),
                  pl.BlockSpec((1, halo, C), prev(a_col)),
                  pl.BlockSpec((1, halo, C), prev(g_col)),
                  pl.BlockSpec((taps, C), lambda b, i: (0, 0)),
                  vec, vec, vec],
        out_specs=pl.BlockSpec((1, ts, C), lambda b, i: (b, i, 0)),
        out_shape=jax.ShapeDtypeStruct((B, S, C), BF16),
        scratch_shapes=[pltpu.VMEM((halo + ts, C), F32), pltpu.VMEM((ts, C), F32)],
        compiler_params=_cparams(("parallel", "arbitrary")),
        name="conformer_conv",
    )(proj3, proj3, proj3, proj3, conv_w, conv_b, ln_g, ln_b)


def _mem_attn_kernel(q_ref, k_ref, v_ref, o_ref):
    s = lax.dot_general(q_ref[0], k_ref[0], (((1,), (1,)), ((), ())), preferred_element_type=F32)
    m = jnp.max(s, axis=-1, keepdims=True)
    p = jnp.exp(s - m)
    p = p / jnp.sum(p, axis=-1, keepdims=True)
    o_ref[0] = jnp.dot(p.astype(BF16), v_ref[0], preferred_element_type=F32).astype(o_ref.dtype)


def _memory_attention(proj3, kv3, *, n_heads, q_col, hd):
    B, S, _ = proj3.shape
    M = kv3.shape[1]
    tq = MEM_TQ
    return pl.pallas_call(
        _mem_attn_kernel,
        grid=(B, n_heads, S // tq),
        in_specs=[pl.BlockSpec((1, tq, hd), lambda b, h, i: (b, i, q_col + h)),
                  pl.BlockSpec((1, M, hd), lambda b, h, i: (b, 0, h)),
                  pl.BlockSpec((1, M, hd), lambda b, h, i: (b, 0, n_heads + h))],
        out_specs=pl.BlockSpec((1, tq, hd), lambda b, h, i: (b, i, h)),
        out_shape=jax.ShapeDtypeStruct((B, S, n_heads * hd), BF16),
        compiler_params=_cparams(("parallel", "parallel", "arbitrary")),
        name="memory_attention",
    )(proj3, kv3, kv3)


def _oproj_kernel(a_ref, c_ref, m_ref, wa_ref, wc_ref, wm_ref, o_ref):
    acc = jnp.dot(a_ref[...], wa_ref[...].astype(BF16), preferred_element_type=F32)
    acc += jnp.dot(c_ref[...], wc_ref[...].astype(BF16), preferred_element_type=F32)
    acc += jnp.dot(m_ref[...], wm_ref[...].astype(BF16), preferred_element_type=F32)
    o_ref[...] = acc


def _out_projection(a, c, m, w_o, *, tm, tn):
    M, Ka = a.shape
    Kc, Km = c.shape[1], m.shape[1]
    assert Kc == Km and Ka % Kc == 0
    N = w_o.shape[1]
    return pl.pallas_call(
        _oproj_kernel,
        grid=(M // tm, N // tn),
        in_specs=[pl.BlockSpec((tm, Ka), lambda i, j: (i, 0)),
                  pl.BlockSpec((tm, Kc), lambda i, j: (i, 0)),
                  pl.BlockSpec((tm, Km), lambda i, j: (i, 0)),
                  pl.BlockSpec((Ka, tn), lambda i, j: (0, j)),
                  pl.BlockSpec((Kc, tn), lambda i, j: (Ka // Kc, j)),
                  pl.BlockSpec((Km, tn), lambda i, j: (Ka // Kc + 1, j))],
        out_specs=pl.BlockSpec((tm, tn), lambda i, j: (i, j)),
        out_shape=jax.ShapeDtypeStruct((M, N), F32),
        compiler_params=_cparams(("parallel", "arbitrary")),
        name="out_projection",
    )(a, c, m, w_o, w_o, w_o)


def _pack_pairs(lo, hi):
    lo_b = lax.bitcast_convert_type(lo.astype(BF16).astype(F32), U32)
    hi_b = lax.bitcast_convert_type(hi.astype(BF16).astype(F32), U32)
    return (lo_b >> 16) | (hi_b & jnp.uint32(0xFFFF0000))


def _unpack_lo(u):
    return lax.bitcast_convert_type(u << 16, F32)


def _unpack_hi(u):
    return lax.bitcast_convert_type(u & jnp.uint32(0xFFFF0000), F32)


def _layer_norm(y, g, b):
    mu = jnp.mean(y, axis=-1, keepdims=True)
    var = jnp.mean(jnp.square(y - mu), axis=-1, keepdims=True)
    return (y - mu) * lax.rsqrt(var + LN_EPS) * g + b


def _ln_router_kernel(x_ref, mix_ref, g_ref, b_ref, wr_ref, br_ref,
                      x1_ref, xp_ref, topi_ref, gate_ref):
    x1 = _layer_norm(DEEPNORM_ALPHA * x_ref[...] + mix_ref[...], g_ref[...], b_ref[...])
    x1_ref[...] = x1
    half = x1.shape[1] // 2
    xp_ref[...] = _pack_pairs(x1[:, :half], x1[:, half:])
    logits = jnp.dot(x1, wr_ref[...], preferred_element_type=F32,
                     precision=lax.Precision.HIGHEST) + br_ref[...]
    tm, ne = logits.shape
    lane = lax.broadcasted_iota(I32, (tm, ne), 1)
    kk = lax.broadcasted_iota(I32, (tm, TOP_K), 1)
    idx = jnp.zeros((tm, TOP_K), I32)
    val = jnp.zeros((tm, TOP_K), F32)
    cur = logits
    for k in range(TOP_K):
        mx = jnp.max(cur, axis=-1, keepdims=True)
        ix = jnp.min(jnp.where(cur == mx, lane, ne), axis=-1, keepdims=True)
        idx = jnp.where(kk == k, ix, idx)
        val = jnp.where(kk == k, mx, val)
        cur = jnp.where(lane == ix, -jnp.inf, cur)
    e = jnp.exp(val - jnp.max(val, axis=-1, keepdims=True))
    topi_ref[...] = idx
    gate_ref[...] = e / jnp.sum(e, axis=-1, keepdims=True)


def _ln_router(x, mix, g, b, w_router, b_router, *, tm):
    T, D = x.shape
    ne = w_router.shape[1]
    row = pl.BlockSpec((tm, D), lambda i: (i, 0))
    vec = pl.BlockSpec((1, D), lambda i: (0, 0))
    return pl.pallas_call(
        _ln_router_kernel,
        grid=(T // tm,),
        in_specs=[row, row, vec, vec,
                  pl.BlockSpec((D, ne), lambda i: (0, 0)),
                  pl.BlockSpec((1, ne), lambda i: (0, 0))],
        out_specs=[row,
                   pl.BlockSpec((tm, D // 2), lambda i: (i, 0)),
                   pl.BlockSpec((tm, TOP_K), lambda i: (i, 0)),
                   pl.BlockSpec((tm, TOP_K), lambda i: (i, 0))],
        out_shape=[jax.ShapeDtypeStruct((T, D), F32),
                   jax.ShapeDtypeStruct((T, D // 2), U32),
                   jax.ShapeDtypeStruct((T, TOP_K), I32),
                   jax.ShapeDtypeStruct((T, TOP_K), F32)],
        compiler_params=_cparams(("parallel",)),
        name="ln_router",
    )(x, mix, g, b, w_router, b_router)


def _rank_kernel(topi_ref, dest_ref, cnt_ref, ps_ref, rank_ref, carry_ref, *, tt, pad):
    ph = pl.program_id(0)
    i = pl.program_id(1)
    lane = lax.broadcasted_iota(I32, (tt, LANES), 1)
    kk = lax.broadcasted_iota(I32, (tt, TOP_K), 1)
    ti = topi_ref[...]
    sel = [lane == ti[:, k:k + 1] for k in range(TOP_K)]
    rows = pl.ds(pl.multiple_of(i * tt, tt), tt)

    @pl.when(ph == 0)
    def _():
        @pl.when(i == 0)
        def _():
            carry_ref[...] = jnp.zeros_like(carry_ref)

        oh = jnp.zeros((tt, LANES), F32)
        for k in range(TOP_K):
            oh = oh + sel[k].astype(F32)
        r = lax.broadcasted_iota(I32, (tt, tt), 0)
        c = lax.broadcasted_iota(I32, (tt, tt), 1)
        lower = (r > c).astype(BF16)
        before = jnp.dot(lower, oh.astype(BF16), preferred_element_type=F32) + carry_ref[0:1, :]
        rank = jnp.zeros((tt, TOP_K), F32)
        for k in range(TOP_K):
            rk = jnp.sum(jnp.where(sel[k], before, 0.0), axis=-1, keepdims=True)
            rank = jnp.where(kk == k, rk, rank)
        rank_ref[rows, :] = rank
        carry_ref[...] = carry_ref[...] + jnp.sum(oh, axis=0, keepdims=True)

    @pl.when(ph == 1)
    def _():
        cnt = carry_ref[...].astype(I32)
        padded = (cnt + (pad - 1)) & jnp.int32(-pad)
        l8 = lax.broadcasted_iota(I32, (8, LANES), 1)
        scan = padded
        for sh in (1, 2, 4, 8, 16, 32, 64):
            scan = scan + jnp.where(l8 >= sh, pltpu.roll(scan, sh, axis=1), 0)
        starts = scan - padded
        cnt_ref[...] = cnt
        ps_ref[...] = starts
        st = starts[0:1, :].astype(F32)
        rank = rank_ref[rows, :]
        dest = jnp.zeros((tt, TOP_K), F32)
        for k in range(TOP_K):
            base = jnp.sum(jnp.where(sel[k], st, 0.0), axis=-1, keepdims=True)
            dest = jnp.where(kk == k, base, dest)
        dest_ref[...] = (dest + rank).astype(I32)


def _rank_assignments(topi):
    T = topi.shape[0]
    tt = RANK_TT
    return pl.pallas_call(
        functools.partial(_rank_kernel, tt=tt, pad=MOE_PAD),
        grid=(2, T // tt),
        in_specs=[pl.BlockSpec((tt, TOP_K), lambda p, i: (i, 0))],
        out_specs=[pl.BlockSpec((tt, TOP_K), lambda p, i: (i * p, 0)),
                   pl.BlockSpec((8, LANES), lambda p, i: (0, 0)),
                   pl.BlockSpec((8, LANES), lambda p, i: (0, 0))],
        out_shape=[jax.ShapeDtypeStruct((T, TOP_K), I32),
                   jax.ShapeDtypeStruct((8, LANES), I32),
                   jax.ShapeDtypeStruct((8, LANES), I32)],
        scratch_shapes=[pltpu.VMEM((T, TOP_K), F32), pltpu.VMEM((8, LANES), F32)],
        compiler_params=_cparams(("arbitrary", "arbitrary")),
        name="rank_assignments",
    )(topi)


def _dispatch_kernel(fillc_ref, misc_ref, dest_ref, xp_ref, xs_ref, zero_ref, fill_sem, row_sem, *, ta):
    i = pl.program_id(0)
    tt = xp_ref.shape[0]

    @pl.when(i == 0)
    def _():
        def fill(e):
            row = pl.multiple_of(fillc_ref[e] * MOE_PAD, MOE_PAD)
            return pltpu.make_async_copy(zero_ref, xs_ref.at[pl.ds(row, MOE_PAD), :], fill_sem)

        def each_group(fn):
            for e in range(N_EXPERTS):
                pl.when(fillc_ref[e] >= 0)(functools.partial(fn, e))

        zero_ref[...] = jnp.zeros_like(zero_ref)
        each_group(lambda e: fill(e).start())
        _fill_tail(misc_ref, zero_ref, xs_ref, fill_sem)
        each_group(lambda e: fill(e).wait())

    def issue(a, carry):
        pltpu.make_async_copy(xp_ref.at[pl.ds(a // TOP_K, 1), :],
                              xs_ref.at[pl.ds(dest_ref[0, 0, a], 1), :], row_sem).start()
        return carry

    lax.fori_loop(0, ta, issue, 0, unroll=8)
    for _ in range(ta // tt):
        pltpu.make_async_copy(xp_ref, xs_ref.at[pl.ds(0, tt), :], row_sem).wait()


def _dispatch(dest, xp, fillc, misc, n_rows):
    T, W = xp.shape
    ta = DISPATCH_TA
    na = dest.size // ta
    assert n_rows % MOE_PAD == 0 and ta % TOP_K == 0
    return pl.pallas_call(
        functools.partial(_dispatch_kernel, ta=ta),
        grid_spec=pltpu.PrefetchScalarGridSpec(
            num_scalar_prefetch=2,
            grid=(na,),
            in_specs=[pl.BlockSpec((1, 1, ta), lambda i, fc, ms: (i, 0, 0), memory_space=pltpu.SMEM),
                      pl.BlockSpec((ta // TOP_K, W), lambda i, fc, ms: (i, 0))],
            out_specs=pl.BlockSpec(memory_space=pl.ANY),
            scratch_shapes=[pltpu.VMEM((MOE_PAD, W), U32),
                            pltpu.SemaphoreType.DMA(()), pltpu.SemaphoreType.DMA(())]),
        out_shape=jax.ShapeDtypeStruct((n_rows, W), U32),
        compiler_params=_cparams(("arbitrary",)),
        name="dispatch_rows",
    )(fillc, misc, dest.reshape(na, 1, ta), xp)


class _TileWriter:
    def __init__(self, wr_ref, wc_ref, stage_ref, out_ref, sem, tn):
        self.wr_ref, self.wc_ref = wr_ref, wc_ref
        self.stage_ref, self.out_ref, self.sem, self.tn = stage_ref, out_ref, sem, tn
        self.n_chunks = stage_ref.shape[1] // MOE_PAD

    def _copy(self, slot, w, j, c):
        row = pl.multiple_of(self.wr_ref[w] + c * MOE_PAD, MOE_PAD)
        col = pl.multiple_of(j * self.tn, self.tn)
        return pltpu.make_async_copy(
            self.stage_ref.at[slot, pl.ds(c * MOE_PAD, MOE_PAD), :],
            self.out_ref.at[pl.ds(row, MOE_PAD), pl.ds(col, self.tn)],
            self.sem.at[slot])

    def _each_chunk(self, w, fn):
        for c in range(self.n_chunks):
            pl.when(c < self.wc_ref[w])(functools.partial(fn, c))

    def start(self, slot, w, j):
        self._each_chunk(w, lambda c: self._copy(slot, w, j, c).start())

    def wait(self, slot, w, j):
        self._each_chunk(w, lambda c: self._copy(slot, w, j, c).wait())

    def retire_older(self, w, j, nj):
        step = w * nj + j

        @pl.when(step >= 2)
        def _():
            wrap = j < 2
            self.wait(step % 2, jnp.where(wrap, w - 1, w), jnp.where(wrap, j + nj - 2, j - 2))

    def drain(self, w, j, nw, nj):
        step = w * nj + j

        @pl.when(step == nw * nj - 1)
        def _():
            self.wait((step + 1) % 2, w, j - 1)
            self.wait(step % 2, w, j)


def _fill_tail(misc_ref, zero_ref, out_ref, sem):
    n_total = out_ref.shape[0] // MOE_PAD

    def fill(c):
        row = pl.multiple_of(c * MOE_PAD, MOE_PAD)
        return pltpu.make_async_copy(zero_ref, out_ref.at[pl.ds(row, MOE_PAD), :], sem)

    def start(c, carry):
        fill(c).start()
        return carry

    def wait(c, carry):
        fill(c).wait()
        return carry

    lax.fori_loop(misc_ref[0], n_total, start, 0)
    lax.fori_loop(misc_ref[0], n_total, wait, 0)


def _moe_up_kernel(we_ref, wr_ref, wc_ref, misc_ref, xs_ref, wg_ref, wu_ref, bg_ref, bu_ref,
                   h_ref, xb_ref, wcat_ref, stage_ref, zero_ref, sem, fill_sem):
    w = pl.program_id(0)
    j = pl.program_id(1)
    nw = pl.num_programs(0)
    nj = pl.num_programs(1)
    slot = (w * nj + j) % 2
    writer = _TileWriter(wr_ref, wc_ref, stage_ref, h_ref, sem, stage_ref.shape[2])

    @pl.when((w == 0) & (j == 0))
    def _():
        zero_ref[...] = jnp.zeros_like(zero_ref)
        _fill_tail(misc_ref, zero_ref, h_ref, fill_sem)

    writer.retire_older(w, j, nj)

    @pl.when(wc_ref[w] > 0)
    def _():
        half = xs_ref.shape[1]

        @pl.when(j == 0)
        def _():
            u = xs_ref[...]
            xb_ref[:, :half] = _unpack_lo(u).astype(BF16)
            xb_ref[:, half:] = _unpack_hi(u).astype(BF16)

        tn = wg_ref.shape[2]
        wcat_ref[:, :tn] = wg_ref[0].astype(BF16)
        wcat_ref[:, tn:] = wu_ref[0].astype(BF16)
        r = jnp.dot(xb_ref[...], wcat_ref[...], preferred_element_type=F32)
        gate = r[:, :tn] + bg_ref[0]
        up = r[:, tn:] + bu_ref[0]
        gate = jnp.minimum(gate, SWIGLU_LIMIT)
        up = jnp.clip(up, -SWIGLU_LIMIT, SWIGLU_LIMIT)
        act = (up + 1.0) * (gate * jax.nn.sigmoid(SWIGLU_ALPHA * gate))
        stage_ref[slot] = act.astype(stage_ref.dtype)
        writer.start(slot, w, j)

    writer.drain(w, j, nw, nj)


def _moe_up(xs, w_gu, b_gu, we, wr, wc, misc, *, d_ff):
    R, half = xs.shape
    E, D, _ = w_gu.shape
    tm, tn = MOE_TM, MOE_TN_UP
    nj = d_ff // tn
    nw = we.shape[0]
    assert nj >= 2 and R % MOE_PAD == 0
    jeff = lambda w, j, wc: jnp.where(wc[w] > 0, j, nj - 1)
    return pl.pallas_call(
        _moe_up_kernel,
        grid_spec=pltpu.PrefetchScalarGridSpec(
            num_scalar_prefetch=4,
            grid=(nw, nj),
            in_specs=[pl.BlockSpec((pl.Element(tm), pl.Element(half)),
                                   lambda w, j, we, wr, wc, ms: (pl.multiple_of(wr[w], MOE_PAD), 0)),
                      pl.BlockSpec((1, D, tn), lambda w, j, we, wr, wc, ms: (we[w], 0, jeff(w, j, wc))),
                      pl.BlockSpec((1, D, tn), lambda w, j, we, wr, wc, ms: (we[w], 0, nj + jeff(w, j, wc))),
                      pl.BlockSpec((1, 1, tn), lambda w, j, we, wr, wc, ms: (we[w], 0, jeff(w, j, wc))),
                      pl.BlockSpec((1, 1, tn), lambda w, j, we, wr, wc, ms: (we[w], 0, nj + jeff(w, j, wc)))],
            out_specs=pl.BlockSpec(memory_space=pl.ANY),
            scratch_shapes=[pltpu.VMEM((tm, 2 * half), BF16),
                            pltpu.VMEM((D, 2 * tn), BF16),
                            pltpu.VMEM((2, tm, tn), BF16),
                            pltpu.VMEM((MOE_PAD, d_ff), BF16),
                            pltpu.SemaphoreType.DMA((2,)), pltpu.SemaphoreType.DMA(())]),
        out_shape=jax.ShapeDtypeStruct((R, d_ff), BF16),
        compiler_params=_cparams(("arbitrary", "arbitrary")),
        name="moe_up",
    )(we, wr, wc, misc, xs, w_gu, w_gu, b_gu, b_gu)


def _moe_down_kernel(we_ref, wr_ref, wc_ref, misc_ref, h_ref, wlo_ref, whi_ref, blo_ref, bhi_ref,
                     o_ref, wcat_ref, stage_ref, zero_ref, sem, fill_sem):
    w = pl.program_id(0)
    j = pl.program_id(1)
    nw = pl.num_programs(0)
    nj = pl.num_programs(1)
    slot = (w * nj + j) % 2
    writer = _TileWriter(wr_ref, wc_ref, stage_ref, o_ref, sem, stage_ref.shape[2])

    @pl.when((w == 0) & (j == 0))
    def _():
        zero_ref[...] = jnp.zeros_like(zero_ref)
        _fill_tail(misc_ref, zero_ref, o_ref, fill_sem)

    writer.retire_older(w, j, nj)

    @pl.when(wc_ref[w] > 0)
    def _():
        tn = wlo_ref.shape[2]
        wcat_ref[:, :tn] = wlo_ref[0].astype(BF16)
        wcat_ref[:, tn:] = whi_ref[0].astype(BF16)
        r = jnp.dot(h_ref[...], wcat_ref[...], preferred_element_type=F32)
        stage_ref[slot] = _pack_pairs(r[:, :tn] + blo_ref[0], r[:, tn:] + bhi_ref[0])
        writer.start(slot, w, j)

    writer.drain(w, j, nw, nj)


def _moe_down(h, w_down, b_down, we, wr, wc, misc):
    R, d_ff = h.shape
    E, _, D = w_down.shape
    tm, tn = MOE_TM, MOE_TN_DOWN
    half = D // 2
    nj = half // tn
    nw = we.shape[0]
    assert nj >= 2 and R % MOE_PAD == 0
    jeff = lambda w, j, wc: jnp.where(wc[w] > 0, j, nj - 1)
    return pl.pallas_call(
        _moe_down_kernel,
        grid_spec=pltpu.PrefetchScalarGridSpec(
            num_scalar_prefetch=4,
            grid=(nw, nj),
            in_specs=[pl.BlockSpec((pl.Element(tm), pl.Element(d_ff)),
                                   lambda w, j, we, wr, wc, ms: (pl.multiple_of(wr[w], MOE_PAD), 0)),
                      pl.BlockSpec((1, d_ff, tn), lambda w, j, we, wr, wc, ms: (we[w], 0, jeff(w, j, wc))),
                      pl.BlockSpec((1, d_ff, tn), lambda w, j, we, wr, wc, ms: (we[w], 0, nj + jeff(w, j, wc))),
                      pl.BlockSpec((1, 1, tn), lambda w, j, we, wr, wc, ms: (we[w], 0, jeff(w, j, wc))),
                      pl.BlockSpec((1, 1, tn), lambda w, j, we, wr, wc, ms: (we[w], 0, nj + jeff(w, j, wc)))],
            out_specs=pl.BlockSpec(memory_space=pl.ANY),
            scratch_shapes=[pltpu.VMEM((d_ff, 2 * tn), BF16),
                            pltpu.VMEM((2, tm, tn), U32),
                            pltpu.VMEM((MOE_PAD, half), U32),
                            pltpu.SemaphoreType.DMA((2,)), pltpu.SemaphoreType.DMA(())]),
        out_shape=jax.ShapeDtypeStruct((R, half), U32),
        compiler_params=_cparams(("arbitrary", "arbitrary")),
        name="moe_down",
    )(we, wr, wc, misc, h, w_down, w_down, b_down, b_down)


def _combine_kernel(dest_ref, dnext_ref, eo_ref, gate_ref, x1_ref, g_ref, b_ref, o_ref, buf_ref, sem,
                    *, tc):
    i = pl.program_id(0)
    n = tc * TOP_K
    slot = i % 2

    def gather(d_ref, slot):
        def issue(a, carry):
            pltpu.make_async_copy(eo_ref.at[pl.ds(d_ref[0, 0, a], 1), :],
                                  buf_ref.at[slot, pl.ds((a % TOP_K) * tc + a // TOP_K, 1), :],
                                  sem.at[slot]).start()
            return carry

        lax.fori_loop(0, n, issue, 0, unroll=8)

    @pl.when(i == 0)
    def _():
        gather(dest_ref, 0)

    @pl.when(i + 1 < pl.num_programs(0))
    def _():
        gather(dnext_ref, 1 - slot)

    pltpu.make_async_copy(eo_ref.at[pl.ds(0, n), :], buf_ref.at[slot], sem.at[slot]).wait()

    gates = gate_ref[...]
    half = buf_ref.shape[2]
    ylo = jnp.zeros((tc, half), F32)
    yhi = jnp.zeros((tc, half), F32)
    for k in range(TOP_K):
        u = buf_ref[slot, k * tc:(k + 1) * tc, :]
        gk = gates[:, k:k + 1]
        ylo = ylo + gk * _unpack_lo(u)
        yhi = yhi + gk * _unpack_hi(u)
    y = DEEPNORM_ALPHA * x1_ref[...] + jnp.concatenate([ylo, yhi], axis=-1)
    o_ref[...] = _layer_norm(y, g_ref[...], b_ref[...])


def _combine(dest, eo, gates, x1, g, b):
    T, D = x1.shape
    tc = COMBINE_TC
    n = tc * TOP_K
    nt = T // tc
    half = eo.shape[1]
    row = pl.BlockSpec((tc, D), lambda i: (i, 0))
    vec = pl.BlockSpec((1, D), lambda i: (0, 0))
    dest3 = dest.reshape(nt, 1, n)
    return pl.pallas_call(
        functools.partial(_combine_kernel, tc=tc),
        grid=(nt,),
        in_specs=[pl.BlockSpec((1, 1, n), lambda i: (i, 0, 0), memory_space=pltpu.SMEM),
                  pl.BlockSpec((1, 1, n), lambda i: (jnp.minimum(i + 1, nt - 1), 0, 0),
                               memory_space=pltpu.SMEM),
                  pl.BlockSpec(memory_space=pl.ANY),
                  pl.BlockSpec((tc, TOP_K), lambda i: (i, 0)),
                  row, vec, vec],
        out_specs=row,
        out_shape=jax.ShapeDtypeStruct((T, D), F32),
        scratch_shapes=[pltpu.VMEM((2, n, half), U32), pltpu.SemaphoreType.DMA((2,))],
        compiler_params=_cparams(("arbitrary",)),
        name="combine_rows",
    )(dest3, dest3, eo, gates, x1, g, b)


def _work_items(cnt, starts, n_items):
    chunks_per_pass = MOE_TM // MOE_PAD
    gchunks = (cnt + MOE_PAD - 1) // MOE_PAD
    npass = (gchunks + chunks_per_pass - 1) // chunks_per_pass
    cum = jnp.cumsum(npass)
    total = cum[-1]
    w = jnp.arange(n_items, dtype=I32)
    e = jnp.minimum(jnp.sum((cum[None, :] <= w[:, None]).astype(I32), axis=1), N_EXPERTS - 1)
    p = w - (cum - npass)[e]
    row = starts[e] + p * MOE_TM
    nch = jnp.clip(gchunks[e] - p * chunks_per_pass, 0, chunks_per_pass)
    valid = w < total
    last = jnp.maximum(total - 1, 0)
    e = jnp.where(valid, e, e[last]).astype(I32)
    row = jnp.where(valid, row, row[last]).astype(I32)
    nch = jnp.where(valid, nch, 0).astype(I32)
    misc = jnp.sum(gchunks).astype(I32).reshape(1)
    fillc = jnp.where(cnt % MOE_PAD != 0, starts // MOE_PAD + gchunks - 1, -1).astype(I32)
    return e, row, nch, misc, fillc


def kernel(x, mem, rel_table, w_in, w_mem_kv, w_o, lambda_q1, lambda_k1, lambda_q2, lambda_k2, subln_g, conv_w, conv_b, conv_ln_g, conv_ln_b, ln1_g, ln1_b, w_router, b_router, w_gate_up, b_gate_up, w_down, b_down, ln2_g, ln2_b):
    B, S, D = x.shape
    T = B * S
    n_heads = rel_table.shape[1]
    qk_w = n_heads * 2 * HEAD_DIM
    v_w = n_heads * V_DIM
    C = conv_w.shape[-1]
    mem_w = w_mem_kv.shape[-1] // 2
    mem_heads = 4
    mem_hd = mem_w // mem_heads
    in_cols = w_in.shape[-1]
    assert in_cols == 2 * qk_w + v_w + 2 * C + mem_w
    d_ff = w_down.shape[2]
    assert DEPTH == 1 and w_in.shape[0] == 1

    col_scale = jnp.concatenate([
        jnp.full((qk_w,), HEAD_DIM ** -0.5, F32),
        jnp.ones((in_cols - qk_w - mem_w,), F32),
        jnp.full((mem_w,), mem_hd ** -0.5, F32)]).reshape(1, in_cols)

    xt = x.reshape(T, D)
    proj = _proj_matmul(xt.astype(BF16), w_in[0], col_scale, tm=1024, tn=512, out_dtype=BF16)
    proj3 = proj.reshape(B, S, in_cols)

    bias = _bias_tiles(rel_table, ATT_T)
    lam_params = jnp.concatenate([lambda_q1, lambda_k1, lambda_q2, lambda_k2], axis=0).astype(F32)
    a_out = _diff_attention(proj3, bias, lam_params, subln_g, n_heads=n_heads,
                            k_col=qk_w // (2 * HEAD_DIM), v_col=2 * qk_w // V_DIM)

    conv_col = (2 * qk_w + v_w) // C
    c_out = _conformer_conv(proj3, conv_w[0, :, 0, :], conv_b, conv_ln_g, conv_ln_b,
                            a_col=conv_col, g_col=conv_col + 1)

    n_mem = mem.shape[1]
    kv = _proj_matmul(mem.reshape(B * n_mem, D).astype(BF16), w_mem_kv[0],
                      jnp.ones((1, 2 * mem_w), F32), tm=B * n_mem, tn=512, out_dtype=BF16)
    m_out = _memory_attention(proj3, kv.reshape(B, n_mem, 2 * mem_w), n_heads=mem_heads,
                              q_col=(in_cols - mem_w) // mem_hd, hd=mem_hd)

    mix = _out_projection(a_out.reshape(T, v_w), c_out.reshape(T, C), m_out.reshape(T, mem_w),
                          w_o[0], tm=1024, tn=512)

    x1, xp, topi, gates = _ln_router(xt, mix, ln1_g, ln1_b, w_router[0],
                                     b_router.reshape(1, -1), tm=256)

    dest, cnt8, starts8 = _rank_assignments(topi)
    cnt, starts = cnt8[0, :N_EXPERTS], starts8[0, :N_EXPERTS]

    n_assign = T * TOP_K
    n_items = n_assign // MOE_TM + N_EXPERTS
    max_rows = n_assign + N_EXPERTS * (MOE_PAD - 1) + MOE_TM
    n_rows = -(-max_rows // MOE_PAD) * MOE_PAD
    we, wr, wc, misc, fillc = _work_items(cnt, starts, n_items)

    xs = _dispatch(dest, xp, fillc, misc, n_rows)
    h = _moe_up(xs, w_gate_up[0], b_gate_up[0].reshape(N_EXPERTS, 1, -1), we, wr, wc, misc, d_ff=d_ff)
    eo = _moe_down(h, w_down[0], b_down[0].reshape(N_EXPERTS, 1, -1), we, wr, wc, misc)
    out = _combine(dest, eo, gates, x1, ln2_g, ln2_b)
    return out.reshape(B, S, D)
```

```python
import functools
import math

import jax
import jax.numpy as jnp
from jax import lax
from jax.experimental import pallas as pl
from jax.experimental.pallas import tpu as pltpu

F32 = jnp.float32
BF16 = jnp.bfloat16
I32 = jnp.int32
U32 = jnp.uint32

HEAD_DIM = 128
V_DIM = 2 * HEAD_DIM
CONV_TAPS = 31
NUM_BUCKETS = 32
MAX_EXACT = NUM_BUCKETS // 2
MAX_DISTANCE = 128
N_EXPERTS = 32
TOP_K = 4
SWIGLU_LIMIT = 7.0
SWIGLU_ALPHA = 1.702
LN_EPS = 1e-5
DEPTH = 1
DEEPNORM_ALPHA = (2 * DEPTH) ** 0.25
LAM_INIT = 0.8 - 0.6 * math.exp(-0.3 * 0)

LANES = 128
V7X_VMEM_BYTES = 64 * 1024 * 1024
VMEM_LIMIT = 58 * 1024 * 1024

ATT_T = 256
CONV_TS = 256
CONV_HALO = 32
MEM_TQ = 512
MOE_TM = 1152
MOE_PAD = 128
MOE_TN_UP = 256
MOE_TN_DOWN = 512
RANK_TT = 512
DISPATCH_TA = 1024
COMBINE_TC = 128


def _cparams(sem, vmem=VMEM_LIMIT):
    return pltpu.CompilerParams(dimension_semantics=sem, vmem_limit_bytes=vmem)


def _proj_kernel(a_ref, w_ref, cs_ref, o_ref):
    acc = jnp.dot(a_ref[...], w_ref[...].astype(BF16), preferred_element_type=F32)
    o_ref[...] = (acc * cs_ref[...]).astype(o_ref.dtype)


def _proj_matmul(a, w, col_scale, *, tm, tn, out_dtype):
    M, K = a.shape
    N = w.shape[1]
    return pl.pallas_call(
        _proj_kernel,
        grid=(M // tm, N // tn),
        in_specs=[pl.BlockSpec((tm, K), lambda i, j: (i, 0)),
                  pl.BlockSpec((K, tn), lambda i, j: (0, j)),
                  pl.BlockSpec((1, tn), lambda i, j: (0, j))],
        out_specs=pl.BlockSpec((tm, tn), lambda i, j: (i, j)),
        out_shape=jax.ShapeDtypeStruct((M, N), out_dtype),
        compiler_params=_cparams(("parallel", "arbitrary")),
        name="proj_matmul",
    )(a, w, col_scale)


def _bias_kernel(tbl_ref, o_ref, *, t, n_heads):
    h = pl.program_id(0)
    i = lax.broadcasted_iota(I32, (t, t), 0)
    j = lax.broadcasted_iota(I32, (t, t), 1)
    for o in range(2):
        dist = o * t + i - j
        n = jnp.maximum(dist, 0)
        nf = jnp.maximum(n, 1).astype(F32)
        large = MAX_EXACT + (jnp.log(nf / MAX_EXACT) / math.log(MAX_DISTANCE / MAX_EXACT)
                             * (NUM_BUCKETS - MAX_EXACT)).astype(I32)
        large = jnp.minimum(large, NUM_BUCKETS - 1)
        bucket = jnp.where(n < MAX_EXACT, n, large)
        val = jnp.zeros((t, t), F32)
        for b in range(NUM_BUCKETS):
            val = jnp.where(bucket == b, tbl_ref[b * n_heads + h], val)
        if o == 0:
            val = jnp.where(dist >= 0, val, -jnp.inf)
        o_ref[0, o] = val
    o_ref[0, 2] = jnp.full((t, t), tbl_ref[(NUM_BUCKETS - 1) * n_heads + h], F32)


def _bias_tiles(rel_table, t):
    n_heads = rel_table.shape[1]
    assert t + 1 >= MAX_DISTANCE
    return pl.pallas_call(
        functools.partial(_bias_kernel, t=t, n_heads=n_heads),
        grid=(n_heads,),
        in_specs=[pl.BlockSpec(memory_space=pltpu.SMEM)],
        out_specs=pl.BlockSpec((1, 3, t, t), lambda h: (h, 0, 0, 0)),
        out_shape=jax.ShapeDtypeStruct((n_heads, 3, t, t), F32),
        compiler_params=_cparams(("arbitrary",)),
        name="bias_tiles",
    )(rel_table.reshape(-1))


def _attn_kernel(lam_ref, q_ref, k_ref, v_ref, bias_ref, g_ref, o_ref, acc_ref, *, t):
    qi = pl.program_id(2)
    q = q_ref[0]
    qs = (q[:, :HEAD_DIM], q[:, HEAD_DIM:])
    dn = (((1,), (1,)), ((), ()))
    acc_ref[...] = jnp.zeros_like(acc_ref)

    def step(kb, carry):
        rows = pl.ds(pl.multiple_of(kb * t, t), t)
        k = k_ref[0, rows, :]
        v = v_ref[0, rows, :]
        b = bias_ref[0, jnp.minimum(qi - kb, 2)]
        out = []
        for c in range(2):
            m, l = carry[2 * c], carry[2 * c + 1]
            s = lax.dot_general(qs[c], k[:, c * HEAD_DIM:(c + 1) * HEAD_DIM], dn,
                                preferred_element_type=F32) + b
            m_new = jnp.maximum(m, jnp.max(s, axis=-1, keepdims=True))
            alpha = jnp.exp(m - m_new)
            p = jnp.exp(s - m_new)
            acc_ref[c] = alpha * acc_ref[c] + jnp.dot(p.astype(BF16), v, preferred_element_type=F32)
            out += [m_new, alpha * l + jnp.sum(p, axis=-1, keepdims=True)]
        return tuple(out)

    neg = jnp.full((t, 1), -jnp.inf, F32)
    zero = jnp.zeros((t, 1), F32)
    _, l1, _, l2 = lax.fori_loop(0, qi + 1, step, (neg, zero, neg, zero))

    lp = lam_ref[...]
    lam = (jnp.exp(jnp.sum(lp[0:1] * lp[1:2], axis=-1, keepdims=True))
           - jnp.exp(jnp.sum(lp[2:3] * lp[3:4], axis=-1, keepdims=True)) + LAM_INIT)
    o = acc_ref[0] * (1.0 / l1) - acc_ref[1] * (lam / l2)
    o = o * lax.rsqrt(jnp.mean(jnp.square(o), axis=-1, keepdims=True) + LN_EPS) * g_ref[...]
    o_ref[0] = (o * (1.0 - LAM_INIT)).astype(o_ref.dtype)


def _diff_attention(proj3, bias, lam_params, subln_g, *, n_heads, k_col, v_col):
    B, S, _ = proj3.shape
    t = ATT_T
    nq = S // t
    w = 2 * HEAD_DIM
    return pl.pallas_call(
        functools.partial(_attn_kernel, t=t),
        grid=(B, n_heads, nq),
        in_specs=[pl.BlockSpec((4, HEAD_DIM), lambda b, h, i: (0, 0)),
                  pl.BlockSpec((1, t, w), lambda b, h, i: (b, i, h)),
                  pl.BlockSpec((1, S, w), lambda b, h, i: (b, 0, k_col + h)),
                  pl.BlockSpec((1, S, w), lambda b, h, i: (b, 0, v_col + h)),
                  pl.BlockSpec((1, 3, t, t), lambda b, h, i: (h, 0, 0, 0)),
                  pl.BlockSpec((1, V_DIM), lambda b, h, i: (0, 0))],
        out_specs=pl.BlockSpec((1, t, V_DIM), lambda b, h, i: (b, i, h)),
        out_shape=jax.ShapeDtypeStruct((B, S, n_heads * V_DIM), BF16),
        scratch_shapes=[pltpu.VMEM((2, t, V_DIM), F32)],
        compiler_params=_cparams(("parallel", "parallel", "arbitrary")),
        name="diff_attention",
    )(lam_params, proj3, proj3, proj3, bias, subln_g)


def _conv_kernel(a_ref, g_ref, ap_ref, gp_ref, w_ref, cb_ref, lg_ref, lb_ref, o_ref,
                 hs_ref, cv_ref, *, ts, halo, taps, rc):
    i = pl.program_id(1)
    C = a_ref.shape[-1]
    hs_ref[halo:, :] = a_ref[0].astype(F32) * jax.nn.sigmoid(g_ref[0].astype(F32))
    hp = ap_ref[0].astype(F32) * jax.nn.sigmoid(gp_ref[0].astype(F32))
    hs_ref[:halo, :] = jnp.where(i > 0, hp, 0.0)
    base = halo - (taps - 1)
    for c in range(C // LANES):
        cs = slice(c * LANES, (c + 1) * LANES)
        for r in range(ts // rc):
            acc = jnp.zeros((rc, LANES), F32)
            for tp in range(taps):
                acc = acc + w_ref[tp:tp + 1, cs] * hs_ref[pl.ds(base + tp + r * rc, rc), cs]
            cv_ref[r * rc:(r + 1) * rc, cs] = acc + cb_ref[:, cs]
    h = cv_ref[...]
    mu = jnp.mean(h, axis=-1, keepdims=True)
    var = jnp.mean(jnp.square(h - mu), axis=-1, keepdims=True)
    y = (h - mu) * lax.rsqrt(var + LN_EPS) * lg_ref[...] + lb_ref[...]
    o_ref[0] = (y * jax.nn.sigmoid(y)).astype(o_ref.dtype)


def _conformer_conv(proj3, conv_w, conv_b, ln_g, ln_b, *, a_col, g_col):
    B, S, _ = proj3.shape
    taps, C = conv_w.shape
    ts, halo = CONV_TS, CONV_HALO
    assert halo >= taps - 1
    hb = ts // halo
    prev = lambda col: (lambda b, i: (b, jnp.maximum(i * hb - 1, 0), col))
    vec = pl.BlockSpec((1, C), lambda b, i: (0, 0))
    return pl.pallas_call(
        functools.partial(_conv_kernel, ts=ts, halo=halo, taps=taps, rc=128),
        grid=(B, S // ts),
        in_specs=[pl.BlockSpec((1, ts, C), lambda b, i: (b, i, a_col)),
                  pl.BlockSpec((1, ts, C), lambda b, i: (b, i, g_col)),
                  pl.BlockSpec((1, halo, C), prev(a_col)),
                  pl.BlockSpec((1, halo, C), prev(g_col)),
                  pl.BlockSpec((taps, C), lambda b, i: (0, 0)),
                  vec, vec, vec],
        out_specs=pl.BlockSpec((1, ts, C), lambda b, i: (b, i, 0)),
        out_shape=jax.ShapeDtypeStruct((B, S, C), BF16),
        scratch_shapes=[pltpu.VMEM((halo + ts, C), F32), pltpu.VMEM((ts, C), F32)],
        compiler_params=_cparams(("parallel", "arbitrary")),
        name="conformer_conv",
    )(proj3, proj3, proj3, proj3, conv_w, conv_b, ln_g, ln_b)


def _mem_attn_kernel(q_ref, k_ref, v_ref, o_ref):
    s = lax.dot_general(q_ref[0], k_ref[0], (((1,), (1,)), ((), ())), preferred_element_type=F32)
    m = jnp.max(s, axis=-1, keepdims=True)
    p = jnp.exp(s - m)
    p = p / jnp.sum(p, axis=-1, keepdims=True)
    o_ref[0] = jnp.dot(p.astype(BF16), v_ref[0], preferred_element_type=F32).astype(o_ref.dtype)


def _memory_attention(proj3, kv3, *, n_heads, q_col, hd):
    B, S, _ = proj3.shape
    M = kv3.shape[1]
    tq = MEM_TQ
    return pl.pallas_call(
        _mem_attn_kernel,
        grid=(B, n_heads, S // tq),
        in_specs=[pl.BlockSpec((1, tq, hd), lambda b, h, i: (b, i, q_col + h)),
                  pl.BlockSpec((1, M, hd), lambda b, h, i: (b, 0, h)),
                  pl.BlockSpec((1, M, hd), lambda b, h, i: (b, 0, n_heads + h))],
        out_specs=pl.BlockSpec((1, tq, hd), lambda b, h, i: (b, i, h)),
        out_shape=jax.ShapeDtypeStruct((B, S, n_heads * hd), BF16),
        compiler_params=_cparams(("parallel", "parallel", "arbitrary")),
        name="memory_attention",
    )(proj3, kv3, kv3)


def _oproj_kernel(a_ref, c_ref, m_ref, wa_ref, wc_ref, wm_ref, o_ref):
    acc = jnp.dot(a_ref[...], wa_ref[...].astype(BF16), preferred_element_type=F32)
    acc += jnp.dot(c_ref[...], wc_ref[...].astype(BF16), preferred_element_type=F32)
    acc += jnp.dot(m_ref[...], wm_ref[...].astype(BF16), preferred_element_type=F32)
    o_ref[...] = acc


def _out_projection(a, c, m, w_o, *, tm, tn):
    M, Ka = a.shape
    Kc, Km = c.shape[1], m.shape[1]
    assert Kc == Km and Ka % Kc == 0
    N = w_o.shape[1]
    return pl.pallas_call(
        _oproj_kernel,
        grid=(M // tm, N // tn),
        in_specs=[pl.BlockSpec((tm, Ka), lambda i, j: (i, 0)),
                  pl.BlockSpec((tm, Kc), lambda i, j: (i, 0)),
                  pl.BlockSpec((tm, Km), lambda i, j: (i, 0)),
                  pl.BlockSpec((Ka, tn), lambda i, j: (0, j)),
                  pl.BlockSpec((Kc, tn), lambda i, j: (Ka // Kc, j)),
                  pl.BlockSpec((Km, tn), lambda i, j: (Ka // Kc + 1, j))],
        out_specs=pl.BlockSpec((tm, tn), lambda i, j: (i, j)),
        out_shape=jax.ShapeDtypeStruct((M, N), F32),
        compiler_params=_cparams(("parallel", "arbitrary")),
        name="out_projection",
    )(a, c, m, w_o, w_o, w_o)


def _pack_pairs(lo, hi):
    lo_b = lax.bitcast_convert_type(lo.astype(BF16).astype(F32), U32)
    hi_b = lax.bitcast_convert_type(hi.astype(BF16).astype(F32), U32)
    return (lo_b >> 16) | (hi_b & jnp.uint32(0xFFFF0000))


def _unpack_lo(u):
    return lax.bitcast_convert_type(u << 16, F32)


def _unpack_hi(u):
    return lax.bitcast_convert_type(u & jnp.uint32(0xFFFF0000), F32)


def _layer_norm(y, g, b):
    mu = jnp.mean(y, axis=-1, keepdims=True)
    var = jnp.mean(jnp.square(y - mu), axis=-1, keepdims=True)
    return (y - mu) * lax.rsqrt(var + LN_EPS) * g + b


def _ln_router_kernel(x_ref, mix_ref, g_ref, b_ref, wr_ref, br_ref,
                      x1_ref, xp_ref, topi_ref, gate_ref):
    x1 = _layer_norm(DEEPNORM_ALPHA * x_ref[...] + mix_ref[...], g_ref[...], b_ref[...])
    x1_ref[...] = x1
    half = x1.shape[1] // 2
    xp_ref[...] = _pack_pairs(x1[:, :half], x1[:, half:])
    logits = jnp.dot(x1, wr_ref[...], preferred_element_type=F32,
                     precision=lax.Precision.HIGHEST) + br_ref[...]
    tm, ne = logits.shape
    lane = lax.broadcasted_iota(I32, (tm, ne), 1)
    kk = lax.broadcasted_iota(I32, (tm, TOP_K), 1)
    idx = jnp.zeros((tm, TOP_K), I32)
    val = jnp.zeros((tm, TOP_K), F32)
    cur = logits
    for k in range(TOP_K):
        mx = jnp.max(cur, axis=-1, keepdims=True)
        ix = jnp.min(jnp.where(cur == mx, lane, ne), axis=-1, keepdims=True)
        idx = jnp.where(kk == k, ix, idx)
        val = jnp.where(kk == k, mx, val)
        cur = jnp.where(lane == ix, -jnp.inf, cur)
    e = jnp.exp(val - jnp.max(val, axis=-1, keepdims=True))
    topi_ref[...] = idx
    gate_ref[...] = e / jnp.sum(e, axis=-1, keepdims=True)


def _ln_router(x, mix, g, b, w_router, b_router, *, tm):
    T, D = x.shape
    ne = w_router.shape[1]
    row = pl.BlockSpec((tm, D), lambda i: (i, 0))
    vec = pl.BlockSpec((1, D), lambda i: (0, 0))
    return pl.pallas_call(
        _ln_router_kernel,
        grid=(T // tm,),
        in_specs=[row, row, vec, vec,
                  pl.BlockSpec((D, ne), lambda i: (0, 0)),
                  pl.BlockSpec((1, ne), lambda i: (0, 0))],
        out_specs=[row,
                   pl.BlockSpec((tm, D // 2), lambda i: (i, 0)),
                   pl.BlockSpec((tm, TOP_K), lambda i: (i, 0)),
                   pl.BlockSpec((tm, TOP_K), lambda i: (i, 0))],
        out_shape=[jax.ShapeDtypeStruct((T, D), F32),
                   jax.ShapeDtypeStruct((T, D // 2), U32),
                   jax.ShapeDtypeStruct((T, TOP_K), I32),
                   jax.ShapeDtypeStruct((T, TOP_K), F32)],
        compiler_params=_cparams(("parallel",)),
        name="ln_router",
    )(x, mix, g, b, w_router, b_router)


def _rank_kernel(topi_ref, dest_ref, cnt_ref, ps_ref, rank_ref, carry_ref, *, tt, pad):
    ph = pl.program_id(0)
    i = pl.program_id(1)
    lane = lax.broadcasted_iota(I32, (tt, LANES), 1)
    kk = lax.broadcasted_iota(I32, (tt, TOP_K), 1)
    ti = topi_ref[...]
    sel = [lane == ti[:, k:k + 1] for k in range(TOP_K)]
    rows = pl.ds(pl.multiple_of(i * tt, tt), tt)

    @pl.when(ph == 0)
    def _():
        @pl.when(i == 0)
        def _():
            carry_ref[...] = jnp.zeros_like(carry_ref)

        oh = jnp.zeros((tt, LANES), F32)
        for k in range(TOP_K):
            oh = oh + sel[k].astype(F32)
        r = lax.broadcasted_iota(I32, (tt, tt), 0)
        c = lax.broadcasted_iota(I32, (tt, tt), 1)
        lower = (r > c).astype(BF16)
        before = jnp.dot(lower, oh.astype(BF16), preferred_element_type=F32) + carry_ref[0:1, :]
        rank = jnp.zeros((tt, TOP_K), F32)
        for k in range(TOP_K):
            rk = jnp.sum(jnp.where(sel[k], before, 0.0), axis=-1, keepdims=True)
            rank = jnp.where(kk == k, rk, rank)
        rank_ref[rows, :] = rank
        carry_ref[...] = carry_ref[...] + jnp.sum(oh, axis=0, keepdims=True)

    @pl.when(ph == 1)
    def _():
        cnt = carry_ref[...].astype(I32)
        padded = (cnt + (pad - 1)) & jnp.int32(-pad)
        l8 = lax.broadcasted_iota(I32, (8, LANES), 1)
        scan = padded
        for sh in (1, 2, 4, 8, 16, 32, 64):
            scan = scan + jnp.where(l8 >= sh, pltpu.roll(scan, sh, axis=1), 0)
        starts = scan - padded
        cnt_ref[...] = cnt
        ps_ref[...] = starts
        st = starts[0:1, :].astype(F32)
        rank = rank_ref[rows, :]
        dest = jnp.zeros((tt, TOP_K), F32)
        for k in range(TOP_K):
            base = jnp.sum(jnp.where(sel[k], st, 0.0), axis=-1, keepdims=True)
            dest = jnp.where(kk == k, base, dest)
        dest_ref[...] = (dest + rank).astype(I32)


def _rank_assignments(topi):
    T = topi.shape[0]
    tt = RANK_TT
    return pl.pallas_call(
        functools.partial(_rank_kernel, tt=tt, pad=MOE_PAD),
        grid=(2, T // tt),
        in_specs=[pl.BlockSpec((tt, TOP_K), lambda p, i: (i, 0))],
        out_specs=[pl.BlockSpec((tt, TOP_K), lambda p, i: (i * p, 0)),
                   pl.BlockSpec((8, LANES), lambda p, i: (0, 0)),
                   pl.BlockSpec((8, LANES), lambda p, i: (0, 0))],
        out_shape=[jax.ShapeDtypeStruct((T, TOP_K), I32),
                   jax.ShapeDtypeStruct((8, LANES), I32),
                   jax.ShapeDtypeStruct((8, LANES), I32)],
        scratch_shapes=[pltpu.VMEM((T, TOP_K), F32), pltpu.VMEM((8, LANES), F32)],
        compiler_params=_cparams(("arbitrary", "arbitrary")),
        name="rank_assignments",
    )(topi)


def _dispatch_kernel(fillc_ref, misc_ref, dest_ref, xp_ref, xs_ref, zero_ref, fill_sem, row_sem, *, ta):
    i = pl.program_id(0)
    tt = xp_ref.shape[0]

    @pl.when(i == 0)
    def _():
        def fill(e):
            row = pl.multiple_of(fillc_ref[e] * MOE_PAD, MOE_PAD)
            return pltpu.make_async_copy(zero_ref, xs_ref.at[pl.ds(row, MOE_PAD), :], fill_sem)

        def each_group(fn):
            for e in range(N_EXPERTS):
                pl.when(fillc_ref[e] >= 0)(functools.partial(fn, e))

        zero_ref[...] = jnp.zeros_like(zero_ref)
        each_group(lambda e: fill(e).start())
        _fill_tail(misc_ref, zero_ref, xs_ref, fill_sem)
        each_group(lambda e: fill(e).wait())

    def issue(t, carry):
        for k in range(TOP_K):
            pltpu.make_async_copy(xp_ref.at[pl.ds(t, 1), :],
                                  xs_ref.at[pl.ds(dest_ref[0, 0, t * TOP_K + k], 1), :],
                                  row_sem).start()
        return carry

    lax.fori_loop(0, tt, issue, 0, unroll=2)
    for _ in range(ta // tt):
        pltpu.make_async_copy(xp_ref, xs_ref.at[pl.ds(0, tt), :], row_sem).wait()


def _dispatch(dest, xp, fillc, misc, n_rows):
    T, W = xp.shape
    ta = DISPATCH_TA
    na = dest.size // ta
    assert n_rows % MOE_PAD == 0 and ta % TOP_K == 0
    return pl.pallas_call(
        functools.partial(_dispatch_kernel, ta=ta),
        grid_spec=pltpu.PrefetchScalarGridSpec(
            num_scalar_prefetch=2,
            grid=(na,),
            in_specs=[pl.BlockSpec((1, 1, ta), lambda i, fc, ms: (i, 0, 0), memory_space=pltpu.SMEM),
                      pl.BlockSpec((ta // TOP_K, W), lambda i, fc, ms: (i, 0))],
            out_specs=pl.BlockSpec(memory_space=pl.ANY),
            scratch_shapes=[pltpu.VMEM((MOE_PAD, W), U32),
                            pltpu.SemaphoreType.DMA(()), pltpu.SemaphoreType.DMA(())]),
        out_shape=jax.ShapeDtypeStruct((n_rows, W), U32),
        compiler_params=_cparams(("arbitrary",)),
        name="dispatch_rows",
    )(fillc, misc, dest.reshape(na, 1, ta), xp)


class _TileWriter:
    def __init__(self, wr_ref, wc_ref, stage_ref, out_ref, sem, tn):
        self.wr_ref, self.wc_ref = wr_ref, wc_ref
        self.stage_ref, self.out_ref, self.sem, self.tn = stage_ref, out_ref, sem, tn
        self.n_chunks = stage_ref.shape[1] // MOE_PAD

    def _copy(self, slot, w, j, c):
        row = pl.multiple_of(self.wr_ref[w] + c * MOE_PAD, MOE_PAD)
        col = pl.multiple_of(j * self.tn, self.tn)
        return pltpu.make_async_copy(
            self.stage_ref.at[slot, pl.ds(c * MOE_PAD, MOE_PAD), :],
            self.out_ref.at[pl.ds(row, MOE_PAD), pl.ds(col, self.tn)],
            self.sem.at[slot])

    def _each_chunk(self, w, fn):
        for c in range(self.n_chunks):
            pl.when(c < self.wc_ref[w])(functools.partial(fn, c))

    def start(self, slot, w, j):
        self._each_chunk(w, lambda c: self._copy(slot, w, j, c).start())

    def wait(self, slot, w, j):
        self._each_chunk(w, lambda c: self._copy(slot, w, j, c).wait())

    def retire_older(self, w, j, nj):
        step = w * nj + j

        @pl.when(step >= 2)
        def _():
            wrap = j < 2
            self.wait(step % 2, jnp.where(wrap, w - 1, w), jnp.where(wrap, j + nj - 2, j - 2))

    def drain(self, w, j, nw, nj):
        step = w * nj + j

        @pl.when(step == nw * nj - 1)
        def _():
            self.wait((step + 1) % 2, w, j - 1)
            self.wait(step % 2, w, j)


def _fill_tail(misc_ref, zero_ref, out_ref, sem):
    n_total = out_ref.shape[0] // MOE_PAD

    def fill(c):
        row = pl.multiple_of(c * MOE_PAD, MOE_PAD)
        return pltpu.make_async_copy(zero_ref, out_ref.at[pl.ds(row, MOE_PAD), :], sem)

    def start(c, carry):
        fill(c).start()
        return carry

    def wait(c, carry):
        fill(c).wait()
        return carry

    lax.fori_loop(misc_ref[0], n_total, start, 0)
    lax.fori_loop(misc_ref[0], n_total, wait, 0)


def _moe_up_kernel(we_ref, wr_ref, wc_ref, misc_ref, xs_ref, wg_ref, wu_ref, bg_ref, bu_ref,
                   h_ref, xb_ref, wcat_ref, stage_ref, zero_ref, sem, fill_sem):
    w = pl.program_id(0)
    j = pl.program_id(1)
    nw = pl.num_programs(0)
    nj = pl.num_programs(1)
    slot = (w * nj + j) % 2
    writer = _TileWriter(wr_ref, wc_ref, stage_ref, h_ref, sem, stage_ref.shape[2])

    @pl.when((w == 0) & (j == 0))
    def _():
        zero_ref[...] = jnp.zeros_like(zero_ref)
        _fill_tail(misc_ref, zero_ref, h_ref, fill_sem)

    writer.retire_older(w, j, nj)

    @pl.when(wc_ref[w] > 0)
    def _():
        half = xs_ref.shape[1]

        @pl.when(j == 0)
        def _():
            u = xs_ref[...]
            xb_ref[:, :half] = _unpack_lo(u).astype(BF16)
            xb_ref[:, half:] = _unpack_hi(u).astype(BF16)

        tn = wg_ref.shape[2]
        wcat_ref[:, :tn] = wg_ref[0].astype(BF16)
        wcat_ref[:, tn:] = wu_ref[0].astype(BF16)
        r = jnp.dot(xb_ref[...], wcat_ref[...], preferred_element_type=F32)
        gate = r[:, :tn] + bg_ref[0]
        up = r[:, tn:] + bu_ref[0]
        gate = jnp.minimum(gate, SWIGLU_LIMIT)
        up = jnp.clip(up, -SWIGLU_LIMIT, SWIGLU_LIMIT)
        act = (up + 1.0) * (gate * jax.nn.sigmoid(SWIGLU_ALPHA * gate))
        stage_ref[slot] = act.astype(stage_ref.dtype)
        writer.start(slot, w, j)

    writer.drain(w, j, nw, nj)


def _moe_up(xs, w_gu, b_gu, we, wr, wc, misc, *, d_ff):
    R, half = xs.shape
    E, D, _ = w_gu.shape
    tm, tn = MOE_TM, MOE_TN_UP
    nj = d_ff // tn
    nw = we.shape[0]
    assert nj >= 2 and R % MOE_PAD == 0
    jeff = lambda w, j, wc: jnp.where(wc[w] > 0, j, nj - 1)
    return pl.pallas_call(
        _moe_up_kernel,
        grid_spec=pltpu.PrefetchScalarGridSpec(
            num_scalar_prefetch=4,
            grid=(nw, nj),
            in_specs=[pl.BlockSpec((pl.Element(tm), pl.Element(half)),
                                   lambda w, j, we, wr, wc, ms: (pl.multiple_of(wr[w], MOE_PAD), 0)),
                      pl.BlockSpec((1, D, tn), lambda w, j, we, wr, wc, ms: (we[w], 0, jeff(w, j, wc))),
                      pl.BlockSpec((1, D, tn), lambda w, j, we, wr, wc, ms: (we[w], 0, nj + jeff(w, j, wc))),
                      pl.BlockSpec((1, 1, tn), lambda w, j, we, wr, wc, ms: (we[w], 0, jeff(w, j, wc))),
                      pl.BlockSpec((1, 1, tn), lambda w, j, we, wr, wc, ms: (we[w], 0, nj + jeff(w, j, wc)))],
            out_specs=pl.BlockSpec(memory_space=pl.ANY),
            scratch_shapes=[pltpu.VMEM((tm, 2 * half), BF16),
                            pltpu.VMEM((D, 2 * tn), BF16),
                            pltpu.VMEM((2, tm, tn), BF16),
                            pltpu.VMEM((MOE_PAD, d_ff), BF16),
                            pltpu.SemaphoreType.DMA((2,)), pltpu.SemaphoreType.DMA(())]),
        out_shape=jax.ShapeDtypeStruct((R, d_ff), BF16),
        compiler_params=_cparams(("arbitrary", "arbitrary")),
        name="moe_up",
    )(we, wr, wc, misc, xs, w_gu, w_gu, b_gu, b_gu)


def _moe_down_kernel(we_ref, wr_ref, wc_ref, misc_ref, h_ref, wlo_ref, whi_ref, blo_ref, bhi_ref,
                     o_ref, wcat_ref, stage_ref, zero_ref, sem, fill_sem):
    w = pl.program_id(0)
    j = pl.program_id(1)
    nw = pl.num_programs(0)
    nj = pl.num_programs(1)
    slot = (w * nj + j) % 2
    writer = _TileWriter(wr_ref, wc_ref, stage_ref, o_ref, sem, stage_ref.shape[2])

    @pl.when((w == 0) & (j == 0))
    def _():
        zero_ref[...] = jnp.zeros_like(zero_ref)
        _fill_tail(misc_ref, zero_ref, o_ref, fill_sem)

    writer.retire_older(w, j, nj)

    @pl.when(wc_ref[w] > 0)
    def _():
        tn = wlo_ref.shape[2]
        wcat_ref[:, :tn] = wlo_ref[0].astype(BF16)
        wcat_ref[:, tn:] = whi_ref[0].astype(BF16)
        r = jnp.dot(h_ref[...], wcat_ref[...], preferred_element_type=F32)
        stage_ref[slot] = _pack_pairs(r[:, :tn] + blo_ref[0], r[:, tn:] + bhi_ref[0])
        writer.start(slot, w, j)

    writer.drain(w, j, nw, nj)


def _moe_down(h, w_down, b_down, we, wr, wc, misc):
    R, d_ff = h.shape
    E, _, D = w_down.shape
    tm, tn = MOE_TM, MOE_TN_DOWN
    half = D // 2
    nj = half // tn
    nw = we.shape[0]
    assert nj >= 2 and R % MOE_PAD == 0
    jeff = lambda w, j, wc: jnp.where(wc[w] > 0, j, nj - 1)
    return pl.pallas_call(
        _moe_down_kernel,
        grid_spec=pltpu.PrefetchScalarGridSpec(
            num_scalar_prefetch=4,
            grid=(nw, nj),
            in_specs=[pl.BlockSpec((pl.Element(tm), pl.Element(d_ff)),
                                   lambda w, j, we, wr, wc, ms: (pl.multiple_of(wr[w], MOE_PAD), 0)),
                      pl.BlockSpec((1, d_ff, tn), lambda w, j, we, wr, wc, ms: (we[w], 0, jeff(w, j, wc))),
                      pl.BlockSpec((1, d_ff, tn), lambda w, j, we, wr, wc, ms: (we[w], 0, nj + jeff(w, j, wc))),
                      pl.BlockSpec((1, 1, tn), lambda w, j, we, wr, wc, ms: (we[w], 0, jeff(w, j, wc))),
                      pl.BlockSpec((1, 1, tn), lambda w, j, we, wr, wc, ms: (we[w], 0, nj + jeff(w, j, wc)))],
            out_specs=pl.BlockSpec(memory_space=pl.ANY),
            scratch_shapes=[pltpu.VMEM((d_ff, 2 * tn), BF16),
                            pltpu.VMEM((2, tm, tn), U32),
                            pltpu.VMEM((MOE_PAD, half), U32),
                            pltpu.SemaphoreType.DMA((2,)), pltpu.SemaphoreType.DMA(())]),
        out_shape=jax.ShapeDtypeStruct((R, half), U32),
        compiler_params=_cparams(("arbitrary", "arbitrary")),
        name="moe_down",
    )(we, wr, wc, misc, h, w_down, w_down, b_down, b_down)


def _combine_kernel(dest_ref, dnext_ref, eo_ref, gate_ref, x1_ref, g_ref, b_ref, o_ref, buf_ref, sem,
                    *, tc):
    i = pl.program_id(0)
    n = tc * TOP_K
    slot = i % 2

    def gather(d_ref, slot):
        def issue(t, carry):
            for k in range(TOP_K):
                pltpu.make_async_copy(eo_ref.at[pl.ds(d_ref[0, 0, t * TOP_K + k], 1), :],
                                      buf_ref.at[slot, pl.ds(k * tc + t, 1), :],
                                      sem.at[slot]).start()
            return carry

        lax.fori_loop(0, tc, issue, 0, unroll=2)

    @pl.when(i == 0)
    def _():
        gather(dest_ref, 0)

    @pl.when(i + 1 < pl.num_programs(0))
    def _():
        gather(dnext_ref, 1 - slot)

    pltpu.make_async_copy(eo_ref.at[pl.ds(0, n), :], buf_ref.at[slot], sem.at[slot]).wait()

    gates = gate_ref[...]
    half = buf_ref.shape[2]
    cw = 2 * LANES
    row_sum = jnp.zeros((tc, 1), F32)
    for c in range(half // cw):
        lo_cols = slice(c * cw, (c + 1) * cw)
        hi_cols = slice(half + c * cw, half + (c + 1) * cw)
        lo = DEEPNORM_ALPHA * x1_ref[:, lo_cols]
        hi = DEEPNORM_ALPHA * x1_ref[:, hi_cols]
        for k in range(TOP_K):
            u = buf_ref[slot, k * tc:(k + 1) * tc, lo_cols]
            gk = gates[:, k:k + 1]
            lo = lo + gk * _unpack_lo(u)
            hi = hi + gk * _unpack_hi(u)
        o_ref[:, lo_cols] = lo
        o_ref[:, hi_cols] = hi
        row_sum = row_sum + jnp.sum(lo, axis=-1, keepdims=True) + jnp.sum(hi, axis=-1, keepdims=True)
    inv_d = 1.0 / (2 * half)
    mu = row_sum * inv_d
    sq_sum = jnp.zeros((tc, 1), F32)
    for c in range(2 * half // cw):
        d = o_ref[:, c * cw:(c + 1) * cw] - mu
        sq_sum = sq_sum + jnp.sum(d * d, axis=-1, keepdims=True)
    rstd = lax.rsqrt(sq_sum * inv_d + LN_EPS)
    for c in range(2 * half // cw):
        cols = slice(c * cw, (c + 1) * cw)
        o_ref[:, cols] = (o_ref[:, cols] - mu) * rstd * g_ref[:, cols] + b_ref[:, cols]


def _combine(dest, eo, gates, x1, g, b):
    T, D = x1.shape
    tc = COMBINE_TC
    n = tc * TOP_K
    nt = T // tc
    half = eo.shape[1]
    row = pl.BlockSpec((tc, D), lambda i: (i, 0))
    vec = pl.BlockSpec((1, D), lambda i: (0, 0))
    dest3 = dest.reshape(nt, 1, n)
    return pl.pallas_call(
        functools.partial(_combine_kernel, tc=tc),
        grid=(nt,),
        in_specs=[pl.BlockSpec((1, 1, n), lambda i: (i, 0, 0), memory_space=pltpu.SMEM),
                  pl.BlockSpec((1, 1, n), lambda i: (jnp.minimum(i + 1, nt - 1), 0, 0),
                               memory_space=pltpu.SMEM),
                  pl.BlockSpec(memory_space=pl.ANY),
                  pl.BlockSpec((tc, TOP_K), lambda i: (i, 0)),
                  row, vec, vec],
        out_specs=row,
        out_shape=jax.ShapeDtypeStruct((T, D), F32),
        scratch_shapes=[pltpu.VMEM((2, n, half), U32), pltpu.SemaphoreType.DMA((2,))],
        compiler_params=_cparams(("arbitrary",)),
        name="combine_rows",
    )(dest3, dest3, eo, gates, x1, g, b)


def _work_items(cnt, starts, n_items):
    chunks_per_pass = MOE_TM // MOE_PAD
    gchunks = (cnt + MOE_PAD - 1) // MOE_PAD
    npass = (gchunks + chunks_per_pass - 1) // chunks_per_pass
    cum = jnp.cumsum(npass)
    total = cum[-1]
    w = jnp.arange(n_items, dtype=I32)
    e = jnp.minimum(jnp.sum((cum[None, :] <= w[:, None]).astype(I32), axis=1), N_EXPERTS - 1)
    p = w - (cum - npass)[e]
    row = starts[e] + p * MOE_TM
    nch = jnp.clip(gchunks[e] - p * chunks_per_pass, 0, chunks_per_pass)
    valid = w < total
    last = jnp.maximum(total - 1, 0)
    e = jnp.where(valid, e, e[last]).astype(I32)
    row = jnp.where(valid, row, row[last]).astype(I32)
    nch = jnp.where(valid, nch, 0).astype(I32)
    misc = jnp.sum(gchunks).astype(I32).reshape(1)
    fillc = jnp.where(cnt % MOE_PAD != 0, starts // MOE_PAD + gchunks - 1, -1).astype(I32)
    return e, row, nch, misc, fillc


def kernel(x, mem, rel_table, w_in, w_mem_kv, w_o, lambda_q1, lambda_k1, lambda_q2, lambda_k2, subln_g, conv_w, conv_b, conv_ln_g, conv_ln_b, ln1_g, ln1_b, w_router, b_router, w_gate_up, b_gate_up, w_down, b_down, ln2_g, ln2_b):
    B, S, D = x.shape
    T = B * S
    n_heads = rel_table.shape[1]
    qk_w = n_heads * 2 * HEAD_DIM
    v_w = n_heads * V_DIM
    C = conv_w.shape[-1]
    mem_w = w_mem_kv.shape[-1] // 2
    mem_heads = 4
    mem_hd = mem_w // mem_heads
    in_cols = w_in.shape[-1]
    assert in_cols == 2 * qk_w + v_w + 2 * C + mem_w
    d_ff = w_down.shape[2]
    assert DEPTH == 1 and w_in.shape[0] == 1

    col_scale = jnp.concatenate([
        jnp.full((qk_w,), HEAD_DIM ** -0.5, F32),
        jnp.ones((in_cols - qk_w - mem_w,), F32),
        jnp.full((mem_w,), mem_hd ** -0.5, F32)]).reshape(1, in_cols)

    xt = x.reshape(T, D)
    proj = _proj_matmul(xt.astype(BF16), w_in[0], col_scale, tm=1024, tn=512, out_dtype=BF16)
    proj3 = proj.reshape(B, S, in_cols)

    bias = _bias_tiles(rel_table, ATT_T)
    lam_params = jnp.concatenate([lambda_q1, lambda_k1, lambda_q2, lambda_k2], axis=0).astype(F32)
    a_out = _diff_attention(proj3, bias, lam_params, subln_g, n_heads=n_heads,
                            k_col=qk_w // (2 * HEAD_DIM), v_col=2 * qk_w // V_DIM)

    conv_col = (2 * qk_w + v_w) // C
    c_out = _conformer_conv(proj3, conv_w[0, :, 0, :], conv_b, conv_ln_g, conv_ln_b,
                            a_col=conv_col, g_col=conv_col + 1)

    n_mem = mem.shape[1]
    kv = _proj_matmul(mem.reshape(B * n_mem, D).astype(BF16), w_mem_kv[0],
                      jnp.ones((1, 2 * mem_w), F32), tm=B * n_mem, tn=512, out_dtype=BF16)
    m_out = _memory_attention(proj3, kv.reshape(B, n_mem, 2 * mem_w), n_heads=mem_heads,
                              q_col=(in_cols - mem_w) // mem_hd, hd=mem_hd)

    mix = _out_projection(a_out.reshape(T, v_w), c_out.reshape(T, C), m_out.reshape(T, mem_w),
                          w_o[0], tm=1024, tn=512)

    x1, xp, topi, gates = _ln_router(xt, mix, ln1_g, ln1_b, w_router[0],
                                     b_router.reshape(1, -1), tm=256)

    dest, cnt8, starts8 = _rank_assignments(topi)
    cnt, starts = cnt8[0, :N_EXPERTS], starts8[0, :N_EXPERTS]

    n_assign = T * TOP_K
    n_items = n_assign // MOE_TM + N_EXPERTS
    max_rows = n_assign + N_EXPERTS * (MOE_PAD - 1) + MOE_TM
    n_rows = -(-max_rows // MOE_PAD) * MOE_PAD
    we, wr, wc, misc, fillc = _work_items(cnt, starts, n_items)

    xs = _dispatch(dest, xp, fillc, misc, n_rows)
    h = _moe_up(xs, w_gate_up[0], b_gate_up[0].reshape(N_EXPERTS, 1, -1), we, wr, wc, misc, d_ff=d_ff)
    eo = _moe_down(h, w_down[0], b_down[0].reshape(N_EXPERTS, 1, -1), we, wr, wc, misc)
    out = _combine(dest, eo, gates, x1, ln2_g, ln2_b)
    return out.reshape(B, S, D)
```

```python
import functools
import math

import jax
import jax.numpy as jnp
from jax import lax
from jax.experimental import pallas as pl
from jax.experimental.pallas import tpu as pltpu

F32 = jnp.float32
BF16 = jnp.bfloat16
I32 = jnp.int32
U32 = jnp.uint32

HEAD_DIM = 128
V_DIM = 2 * HEAD_DIM
CONV_TAPS = 31
NUM_BUCKETS = 32
MAX_EXACT = NUM_BUCKETS // 2
MAX_DISTANCE = 128
N_EXPERTS = 32
TOP_K = 4
SWIGLU_LIMIT = 7.0
SWIGLU_ALPHA = 1.702
LN_EPS = 1e-5
DEPTH = 1
DEEPNORM_ALPHA = (2 * DEPTH) ** 0.25
LAM_INIT = 0.8 - 0.6 * math.exp(-0.3 * 0)

LANES = 128
V7X_VMEM_BYTES = 64 * 1024 * 1024
VMEM_LIMIT = 58 * 1024 * 1024

ATT_T = 256
CONV_TS = 256
CONV_HALO = 32
MEM_TQ = 512
MOE_TM = 1152
MOE_PAD = 128
MOE_TN_UP = 256
MOE_UP_ROW_CHUNKS = 3
MOE_TN_DOWN = 512
RANK_TT = 512
DISPATCH_TA = 1024
COMBINE_TC = 128


def _cparams(sem, vmem=VMEM_LIMIT):
    return pltpu.CompilerParams(dimension_semantics=sem, vmem_limit_bytes=vmem)


def _proj_kernel(a_ref, w_ref, cs_ref, o_ref):
    acc = jnp.dot(a_ref[...], w_ref[...].astype(BF16), preferred_element_type=F32)
    o_ref[...] = (acc * cs_ref[...]).astype(o_ref.dtype)


def _proj_matmul(a, w, col_scale, *, tm, tn, out_dtype):
    M, K = a.shape
    N = w.shape[1]
    return pl.pallas_call(
        _proj_kernel,
        grid=(M // tm, N // tn),
        in_specs=[pl.BlockSpec((tm, K), lambda i, j: (i, 0)),
                  pl.BlockSpec((K, tn), lambda i, j: (0, j)),
                  pl.BlockSpec((1, tn), lambda i, j: (0, j))],
        out_specs=pl.BlockSpec((tm, tn), lambda i, j: (i, j)),
        out_shape=jax.ShapeDtypeStruct((M, N), out_dtype),
        compiler_params=_cparams(("parallel", "arbitrary")),
        name="proj_matmul",
    )(a, w, col_scale)


def _bias_kernel(tbl_ref, o_ref, *, t, n_heads):
    h = pl.program_id(0)
    j = lax.broadcasted_iota(I32, (t, t), 0)
    i = lax.broadcasted_iota(I32, (t, t), 1)
    for o in range(2):
        dist = o * t + i - j
        n = jnp.maximum(dist, 0)
        nf = jnp.maximum(n, 1).astype(F32)
        large = MAX_EXACT + (jnp.log(nf / MAX_EXACT) / math.log(MAX_DISTANCE / MAX_EXACT)
                             * (NUM_BUCKETS - MAX_EXACT)).astype(I32)
        large = jnp.minimum(large, NUM_BUCKETS - 1)
        bucket = jnp.where(n < MAX_EXACT, n, large)
        val = jnp.zeros((t, t), F32)
        for b in range(NUM_BUCKETS):
            val = jnp.where(bucket == b, tbl_ref[b * n_heads + h], val)
        if o == 0:
            val = jnp.where(dist >= 0, val, -jnp.inf)
        o_ref[0, o] = val
    o_ref[0, 2] = jnp.full((t, t), tbl_ref[(NUM_BUCKETS - 1) * n_heads + h], F32)


def _bias_tiles(rel_table, t):
    n_heads = rel_table.shape[1]
    assert t + 1 >= MAX_DISTANCE
    return pl.pallas_call(
        functools.partial(_bias_kernel, t=t, n_heads=n_heads),
        grid=(n_heads,),
        in_specs=[pl.BlockSpec(memory_space=pltpu.SMEM)],
        out_specs=pl.BlockSpec((1, 3, t, t), lambda h: (h, 0, 0, 0)),
        out_shape=jax.ShapeDtypeStruct((n_heads, 3, t, t), F32),
        compiler_params=_cparams(("arbitrary",)),
        name="bias_tiles",
    )(rel_table.reshape(-1))


def _attn_kernel(lam_ref, q_ref, k_ref, v_ref, bias_ref, g_ref, o_ref, vt_ref, acc_ref, m_ref, l_ref,
                 *, t):
    qi = pl.program_id(2)
    n_kt = v_ref.shape[1] // t

    @pl.when(qi == 0)
    def _():
        for kb in range(n_kt):
            vt_ref[:, kb * t:(kb + 1) * t] = v_ref[0, kb * t:(kb + 1) * t, :].astype(F32).T.astype(BF16)

    q = q_ref[0]
    qs = (q[:, :HEAD_DIM], q[:, HEAD_DIM:])
    dn = (((1,), (1,)), ((), ()))
    acc_ref[...] = jnp.zeros_like(acc_ref)
    m_ref[...] = jnp.full(m_ref.shape, -jnp.inf, F32)
    l_ref[...] = jnp.zeros_like(l_ref)

    def update(kb0, n_tiles):
        start = pl.multiple_of(kb0 * t, t)
        k = k_ref[0, pl.ds(start, n_tiles * t), :]
        vt = vt_ref[:, pl.ds(start, n_tiles * t)]
        bias = [bias_ref[0, jnp.minimum(qi - kb0 - i, 2)] for i in range(n_tiles)]
        for c in range(2):
            s = [lax.dot_general(k[i * t:(i + 1) * t, c * HEAD_DIM:(c + 1) * HEAD_DIM], qs[c], dn,
                                 preferred_element_type=F32) + bias[i]
                 for i in range(n_tiles)]
            m_old = m_ref[c]
            m_new = m_old
            for si in s:
                m_new = jnp.maximum(m_new, jnp.max(si, axis=0, keepdims=True))
            alpha = jnp.exp(m_old - m_new)
            p = [jnp.exp(si - m_new) for si in s]
            l_new = alpha * l_ref[c]
            for pi in p:
                l_new = l_new + jnp.sum(pi, axis=0, keepdims=True)
            pcat = p[0].astype(BF16) if n_tiles == 1 else jnp.concatenate(
                [pi.astype(BF16) for pi in p], axis=0)
            acc_ref[c] = alpha * acc_ref[c] + jnp.dot(vt, pcat, preferred_element_type=F32)
            m_ref[c] = m_new
            l_ref[c] = l_new

    n = qi + 1

    def pair(i, carry):
        update(2 * i, 2)
        return carry

    lax.fori_loop(0, n // 2, pair, 0)

    @pl.when(n % 2 == 1)
    def _():
        update(qi, 1)

    lp = lam_ref[...]
    lam = (jnp.exp(jnp.sum(lp[0:1] * lp[1:2], axis=-1, keepdims=True))
           - jnp.exp(jnp.sum(lp[2:3] * lp[3:4], axis=-1, keepdims=True)) + LAM_INIT)
    ot = acc_ref[0] * (1.0 / l_ref[0]) - acc_ref[1] * (lam / l_ref[1])
    ot = ot * lax.rsqrt(jnp.mean(jnp.square(ot), axis=0, keepdims=True) + LN_EPS)
    o = ot.T * g_ref[...]
    o_ref[0] = (o * (1.0 - LAM_INIT)).astype(o_ref.dtype)


def _diff_attention(proj3, bias, lam_params, subln_g, *, n_heads, k_col, v_col):
    B, S, _ = proj3.shape
    t = ATT_T
    nq = S // t
    w = 2 * HEAD_DIM
    return pl.pallas_call(
        functools.partial(_attn_kernel, t=t),
        grid=(B, n_heads, nq),
        in_specs=[pl.BlockSpec((4, HEAD_DIM), lambda b, h, i: (0, 0)),
                  pl.BlockSpec((1, t, w), lambda b, h, i: (b, i, h)),
                  pl.BlockSpec((1, S, w), lambda b, h, i: (b, 0, k_col + h)),
                  pl.BlockSpec((1, S, w), lambda b, h, i: (b, 0, v_col + h)),
                  pl.BlockSpec((1, 3, t, t), lambda b, h, i: (h, 0, 0, 0)),
                  pl.BlockSpec((1, V_DIM), lambda b, h, i: (0, 0))],
        out_specs=pl.BlockSpec((1, t, V_DIM), lambda b, h, i: (b, i, h)),
        out_shape=jax.ShapeDtypeStruct((B, S, n_heads * V_DIM), BF16),
        scratch_shapes=[pltpu.VMEM((V_DIM, S), BF16), pltpu.VMEM((2, V_DIM, t), F32),
                        pltpu.VMEM((2, 1, t), F32), pltpu.VMEM((2, 1, t), F32)],
        compiler_params=_cparams(("parallel", "parallel", "arbitrary")),
        name="diff_attention",
    )(lam_params, proj3, proj3, proj3, bias, subln_g)


def _conv_kernel(a_ref, g_ref, ap_ref, gp_ref, w_ref, cb_ref, lg_ref, lb_ref, o_ref,
                 hs_ref, cv_ref, *, ts, halo, taps, rc):
    i = pl.program_id(1)
    C = a_ref.shape[-1]
    hs_ref[halo:, :] = a_ref[0].astype(F32) * jax.nn.sigmoid(g_ref[0].astype(F32))
    hp = ap_ref[0].astype(F32) * jax.nn.sigmoid(gp_ref[0].astype(F32))
    hs_ref[:halo, :] = jnp.where(i > 0, hp, 0.0)
    base = halo - (taps - 1)
    for c in range(C // LANES):
        cs = slice(c * LANES, (c + 1) * LANES)
        for r in range(ts // rc):
            acc = jnp.zeros((rc, LANES), F32)
            for tp in range(taps):
                acc = acc + w_ref[tp:tp + 1, cs] * hs_ref[pl.ds(base + tp + r * rc, rc), cs]
            cv_ref[r * rc:(r + 1) * rc, cs] = acc + cb_ref[:, cs]
    h = cv_ref[...]
    mu = jnp.mean(h, axis=-1, keepdims=True)
    var = jnp.mean(jnp.square(h - mu), axis=-1, keepdims=True)
    y = (h - mu) * lax.rsqrt(var + LN_EPS) * lg_ref[...] + lb_ref[...]
    o_ref[0] = (y * jax.nn.sigmoid(y)).astype(o_ref.dtype)


def _conformer_conv(proj3, conv_w, conv_b, ln_g, ln_b, *, a_col, g_col):
    B, S, _ = proj3.shape
    taps, C = conv_w.shape
    ts, halo = CONV_TS, CONV_HALO
    assert halo >= taps - 1
    hb = ts // halo
    prev = lambda col: (lambda b, i: (b, jnp.maximum(i * hb - 1, 0), col))
    vec = pl.BlockSpec((1, C), lambda b, i: (0, 0))
    return pl.pallas_call(
        functools.partial(_conv_kernel, ts=ts, halo=halo, taps=taps, rc=128),
        grid=(B, S // ts),
        in_specs=[pl.BlockSpec((1, ts, C), lambda b, i: (b, i, a_col)),
                  pl.BlockSpec((1, ts, C), lambda b, i: (b, i, g_col)),
                  pl.BlockSpec((1, halo, C), prev(a_col)),
                  pl.BlockSpec((1, halo, C), prev(g_col)),
                  pl.BlockSpec((taps, C), lambda b, i: (0, 0)),
                  vec, vec, vec],
        out_specs=pl.BlockSpec((1, ts, C), lambda b, i: (b, i, 0)),
        out_shape=jax.ShapeDtypeStruct((B, S, C), BF16),
        scratch_shapes=[pltpu.VMEM((halo + ts, C), F32), pltpu.VMEM((ts, C), F32)],
        compiler_params=_cparams(("parallel", "arbitrary")),
        name="conformer_conv",
    )(proj3, proj3, proj3, proj3, conv_w, conv_b, ln_g, ln_b)


def _mem_attn_kernel(q_ref, k_ref, v_ref, o_ref):
    s = lax.dot_general(q_ref[0], k_ref[0], (((1,), (1,)), ((), ())), preferred_element_type=F32)
    m = jnp.max(s, axis=-1, keepdims=True)
    p = jnp.exp(s - m)
    p = p / jnp.sum(p, axis=-1, keepdims=True)
    o_ref[0] = jnp.dot(p.astype(BF16), v_ref[0], preferred_element_type=F32).astype(o_ref.dtype)


def _memory_attention(proj3, kv3, *, n_heads, q_col, hd):
    B, S, _ = proj3.shape
    M = kv3.shape[1]
    tq = MEM_TQ
    return pl.pallas_call(
        _mem_attn_kernel,
        grid=(B, n_heads, S // tq),
        in_specs=[pl.BlockSpec((1, tq, hd), lambda b, h, i: (b, i, q_col + h)),
                  pl.BlockSpec((1, M, hd), lambda b, h, i: (b, 0, h)),
                  pl.BlockSpec((1, M, hd), lambda b, h, i: (b, 0, n_heads + h))],
        out_specs=pl.BlockSpec((1, tq, hd), lambda b, h, i: (b, i, h)),
        out_shape=jax.ShapeDtypeStruct((B, S, n_heads * hd), BF16),
        compiler_params=_cparams(("parallel", "parallel", "arbitrary")),
        name="memory_attention",
    )(proj3, kv3, kv3)


def _oproj_kernel(a_ref, c_ref, m_ref, wa_ref, wc_ref, wm_ref, o_ref):
    acc = jnp.dot(a_ref[...], wa_ref[...].astype(BF16), preferred_element_type=F32)
    acc += jnp.dot(c_ref[...], wc_ref[...].astype(BF16), preferred_element_type=F32)
    acc += jnp.dot(m_ref[...], wm_ref[...].astype(BF16), preferred_element_type=F32)
    o_ref[...] = acc


def _out_projection(a, c, m, w_o, *, tm, tn):
    M, Ka = a.shape
    Kc, Km = c.shape[1], m.shape[1]
    assert Kc == Km and Ka % Kc == 0
    N = w_o.shape[1]
    return pl.pallas_call(
        _oproj_kernel,
        grid=(M // tm, N // tn),
        in_specs=[pl.BlockSpec((tm, Ka), lambda i, j: (i, 0)),
                  pl.BlockSpec((tm, Kc), lambda i, j: (i, 0)),
                  pl.BlockSpec((tm, Km), lambda i, j: (i, 0)),
                  pl.BlockSpec((Ka, tn), lambda i, j: (0, j)),
                  pl.BlockSpec((Kc, tn), lambda i, j: (Ka // Kc, j)),
                  pl.BlockSpec((Km, tn), lambda i, j: (Ka // Kc + 1, j))],
        out_specs=pl.BlockSpec((tm, tn), lambda i, j: (i, j)),
        out_shape=jax.ShapeDtypeStruct((M, N), F32),
        compiler_params=_cparams(("parallel", "arbitrary")),
        name="out_projection",
    )(a, c, m, w_o, w_o, w_o)


def _pack_pairs(lo, hi):
    lo_b = lax.bitcast_convert_type(lo.astype(BF16).astype(F32), U32)
    hi_b = lax.bitcast_convert_type(hi.astype(BF16).astype(F32), U32)
    return (lo_b >> 16) | (hi_b & jnp.uint32(0xFFFF0000))


def _unpack_lo(u):
    return lax.bitcast_convert_type(u << 16, F32)


def _unpack_hi(u):
    return lax.bitcast_convert_type(u & jnp.uint32(0xFFFF0000), F32)


def _layer_norm(y, g, b):
    mu = jnp.mean(y, axis=-1, keepdims=True)
    var = jnp.mean(jnp.square(y - mu), axis=-1, keepdims=True)
    return (y - mu) * lax.rsqrt(var + LN_EPS) * g + b


def _ln_router_kernel(x_ref, mix_ref, g_ref, b_ref, wr_ref, br_ref,
                      x1_ref, xp_ref, topi_ref, gate_ref):
    x1 = _layer_norm(DEEPNORM_ALPHA * x_ref[...] + mix_ref[...], g_ref[...], b_ref[...])
    x1_ref[...] = x1
    half = x1.shape[1] // 2
    xp_ref[...] = _pack_pairs(x1[:, :half], x1[:, half:])
    logits = jnp.dot(x1, wr_ref[...], preferred_element_type=F32,
                     precision=lax.Precision.HIGHEST) + br_ref[...]
    tm, ne = logits.shape
    lane = lax.broadcasted_iota(I32, (tm, ne), 1)
    kk = lax.broadcasted_iota(I32, (tm, TOP_K), 1)
    idx = jnp.zeros((tm, TOP_K), I32)
    val = jnp.zeros((tm, TOP_K), F32)
    cur = logits
    for k in range(TOP_K):
        mx = jnp.max(cur, axis=-1, keepdims=True)
        ix = jnp.min(jnp.where(cur == mx, lane, ne), axis=-1, keepdims=True)
        idx = jnp.where(kk == k, ix, idx)
        val = jnp.where(kk == k, mx, val)
        cur = jnp.where(lane == ix, -jnp.inf, cur)
    e = jnp.exp(val - jnp.max(val, axis=-1, keepdims=True))
    topi_ref[...] = idx
    gate_ref[...] = e / jnp.sum(e, axis=-1, keepdims=True)


def _ln_router(x, mix, g, b, w_router, b_router, *, tm):
    T, D = x.shape
    ne = w_router.shape[1]
    row = pl.BlockSpec((tm, D), lambda i: (i, 0))
    vec = pl.BlockSpec((1, D), lambda i: (0, 0))
    return pl.pallas_call(
        _ln_router_kernel,
        grid=(T // tm,),
        in_specs=[row, row, vec, vec,
                  pl.BlockSpec((D, ne), lambda i: (0, 0)),
                  pl.BlockSpec((1, ne), lambda i: (0, 0))],
        out_specs=[row,
                   pl.BlockSpec((tm, D // 2), lambda i: (i, 0)),
                   pl.BlockSpec((tm, TOP_K), lambda i: (i, 0)),
                   pl.BlockSpec((tm, TOP_K), lambda i: (i, 0))],
        out_shape=[jax.ShapeDtypeStruct((T, D), F32),
                   jax.ShapeDtypeStruct((T, D // 2), U32),
                   jax.ShapeDtypeStruct((T, TOP_K), I32),
                   jax.ShapeDtypeStruct((T, TOP_K), F32)],
        compiler_params=_cparams(("parallel",)),
        name="ln_router",
    )(x, mix, g, b, w_router, b_router)


def _rank_kernel(topi_ref, dest_ref, cnt_ref, ps_ref, rank_ref, carry_ref, *, tt, pad):
    ph = pl.program_id(0)
    i = pl.program_id(1)
    lane = lax.broadcasted_iota(I32, (tt, LANES), 1)
    kk = lax.broadcasted_iota(I32, (tt, TOP_K), 1)
    ti = topi_ref[...]
    sel = [lane == ti[:, k:k + 1] for k in range(TOP_K)]
    rows = pl.ds(pl.multiple_of(i * tt, tt), tt)

    @pl.when(ph == 0)
    def _():
        @pl.when(i == 0)
        def _():
            carry_ref[...] = jnp.zeros_like(carry_ref)

        oh = jnp.zeros((tt, LANES), F32)
        for k in range(TOP_K):
            oh = oh + sel[k].astype(F32)
        r = lax.broadcasted_iota(I32, (tt, tt), 0)
        c = lax.broadcasted_iota(I32, (tt, tt), 1)
        lower = (r > c).astype(BF16)
        before = jnp.dot(lower, oh.astype(BF16), preferred_element_type=F32) + carry_ref[0:1, :]
        rank = jnp.zeros((tt, TOP_K), F32)
        for k in range(TOP_K):
            rk = jnp.sum(jnp.where(sel[k], before, 0.0), axis=-1, keepdims=True)
            rank = jnp.where(kk == k, rk, rank)
        rank_ref[rows, :] = rank
        carry_ref[...] = carry_ref[...] + jnp.sum(oh, axis=0, keepdims=True)

    @pl.when(ph == 1)
    def _():
        cnt = carry_ref[...].astype(I32)
        padded = (cnt + (pad - 1)) & jnp.int32(-pad)
        l8 = lax.broadcasted_iota(I32, (8, LANES), 1)
        scan = padded
        for sh in (1, 2, 4, 8, 16, 32, 64):
            scan = scan + jnp.where(l8 >= sh, pltpu.roll(scan, sh, axis=1), 0)
        starts = scan - padded
        cnt_ref[...] = cnt
        ps_ref[...] = starts
        st = starts[0:1, :].astype(F32)
        rank = rank_ref[rows, :]
        dest = jnp.zeros((tt, TOP_K), F32)
        for k in range(TOP_K):
            base = jnp.sum(jnp.where(sel[k], st, 0.0), axis=-1, keepdims=True)
            dest = jnp.where(kk == k, base, dest)
        dest_ref[...] = (dest + rank).astype(I32)


def _rank_assignments(topi):
    T = topi.shape[0]
    tt = RANK_TT
    return pl.pallas_call(
        functools.partial(_rank_kernel, tt=tt, pad=MOE_PAD),
        grid=(2, T // tt),
        in_specs=[pl.BlockSpec((tt, TOP_K), lambda p, i: (i, 0))],
        out_specs=[pl.BlockSpec((tt, TOP_K), lambda p, i: (i * p, 0)),
                   pl.BlockSpec((8, LANES), lambda p, i: (0, 0)),
                   pl.BlockSpec((8, LANES), lambda p, i: (0, 0))],
        out_shape=[jax.ShapeDtypeStruct((T, TOP_K), I32),
                   jax.ShapeDtypeStruct((8, LANES), I32),
                   jax.ShapeDtypeStruct((8, LANES), I32)],
        scratch_shapes=[pltpu.VMEM((T, TOP_K), F32), pltpu.VMEM((8, LANES), F32)],
        compiler_params=_cparams(("arbitrary", "arbitrary")),
        name="rank_assignments",
    )(topi)


def _dispatch_kernel(fillc_ref, misc_ref, dest_ref, xp_ref, xs_ref, zero_ref, fill_sem, row_sem, *, ta):
    i = pl.program_id(0)
    tt = xp_ref.shape[0]

    @pl.when(i == 0)
    def _():
        def fill(e):
            row = pl.multiple_of(fillc_ref[e] * MOE_PAD, MOE_PAD)
            return pltpu.make_async_copy(zero_ref, xs_ref.at[pl.ds(row, MOE_PAD), :], fill_sem)

        def each_group(fn):
            for e in range(N_EXPERTS):
                pl.when(fillc_ref[e] >= 0)(functools.partial(fn, e))

        zero_ref[...] = jnp.zeros_like(zero_ref)
        each_group(lambda e: fill(e).start())
        _fill_tail(misc_ref, zero_ref, xs_ref, fill_sem)
        each_group(lambda e: fill(e).wait())

    def issue(t, carry):
        for k in range(TOP_K):
            pltpu.make_async_copy(xp_ref.at[pl.ds(t, 1), :],
                                  xs_ref.at[pl.ds(dest_ref[0, 0, t * TOP_K + k], 1), :],
                                  row_sem).start()
        return carry

    lax.fori_loop(0, tt, issue, 0, unroll=2)
    for _ in range(ta // tt):
        pltpu.make_async_copy(xp_ref, xs_ref.at[pl.ds(0, tt), :], row_sem).wait()


def _dispatch(dest, xp, fillc, misc, n_rows):
    T, W = xp.shape
    ta = DISPATCH_TA
    na = dest.size // ta
    assert n_rows % MOE_PAD == 0 and ta % TOP_K == 0
    return pl.pallas_call(
        functools.partial(_dispatch_kernel, ta=ta),
        grid_spec=pltpu.PrefetchScalarGridSpec(
            num_scalar_prefetch=2,
            grid=(na,),
            in_specs=[pl.BlockSpec((1, 1, ta), lambda i, fc, ms: (i, 0, 0), memory_space=pltpu.SMEM),
                      pl.BlockSpec((ta // TOP_K, W), lambda i, fc, ms: (i, 0))],
            out_specs=pl.BlockSpec(memory_space=pl.ANY),
            scratch_shapes=[pltpu.VMEM((MOE_PAD, W), U32),
                            pltpu.SemaphoreType.DMA(()), pltpu.SemaphoreType.DMA(())]),
        out_shape=jax.ShapeDtypeStruct((n_rows, W), U32),
        compiler_params=_cparams(("arbitrary",)),
        name="dispatch_rows",
    )(fillc, misc, dest.reshape(na, 1, ta), xp)


class _TileWriter:
    def __init__(self, wr_ref, wc_ref, stage_ref, out_ref, sem, tn):
        self.wr_ref, self.wc_ref = wr_ref, wc_ref
        self.stage_ref, self.out_ref, self.sem, self.tn = stage_ref, out_ref, sem, tn
        n_chunks = stage_ref.shape[1] // MOE_PAD
        self.sizes = [1 << b for b in reversed(range(n_chunks.bit_length()))]

    def _copy(self, slot, w, j, size):
        n = self.wc_ref[w]
        first = pl.multiple_of((n & -(2 * size)) * MOE_PAD, MOE_PAD)
        row = pl.multiple_of(self.wr_ref[w] + first, MOE_PAD)
        col = pl.multiple_of(j * self.tn, self.tn)
        return pltpu.make_async_copy(
            self.stage_ref.at[slot, pl.ds(first, size * MOE_PAD), :],
            self.out_ref.at[pl.ds(row, size * MOE_PAD), pl.ds(col, self.tn)],
            self.sem.at[slot])

    def _each_group(self, w, fn):
        for size in self.sizes:
            pl.when((self.wc_ref[w] & size) != 0)(functools.partial(fn, size))

    def start(self, slot, w, j):
        self._each_group(w, lambda size: self._copy(slot, w, j, size).start())

    def wait(self, slot, w, j):
        self._each_group(w, lambda size: self._copy(slot, w, j, size).wait())

    def retire_older(self, w, j, nj):
        step = w * nj + j

        @pl.when(step >= 2)
        def _():
            wrap = j < 2
            self.wait(step % 2, jnp.where(wrap, w - 1, w), jnp.where(wrap, j + nj - 2, j - 2))

    def drain(self, w, j, nw, nj):
        step = w * nj + j

        @pl.when(step == nw * nj - 1)
        def _():
            self.wait((step + 1) % 2, w, j - 1)
            self.wait(step % 2, w, j)


def _fill_tail(misc_ref, zero_ref, out_ref, sem):
    n_total = out_ref.shape[0] // MOE_PAD

    def fill(c):
        row = pl.multiple_of(c * MOE_PAD, MOE_PAD)
        return pltpu.make_async_copy(zero_ref, out_ref.at[pl.ds(row, MOE_PAD), :], sem)

    def start(c, carry):
        fill(c).start()
        return carry

    def wait(c, carry):
        fill(c).wait()
        return carry

    lax.fori_loop(misc_ref[0], n_total, start, 0)
    lax.fori_loop(misc_ref[0], n_total, wait, 0)


def _moe_up_kernel(we_ref, wr_ref, wc_ref, misc_ref, xs_ref, wg_ref, wu_ref, b_ref,
                   h_ref, xb_ref, wcat_ref, stage_ref, zero_ref, sem, fill_sem):
    w = pl.program_id(0)
    j = pl.program_id(1)
    nw = pl.num_programs(0)
    nj = pl.num_programs(1)
    slot = (w * nj + j) % 2
    writer = _TileWriter(wr_ref, wc_ref, stage_ref, h_ref, sem, stage_ref.shape[2])

    @pl.when((w == 0) & (j == 0))
    def _():
        zero_ref[...] = jnp.zeros_like(zero_ref)
        _fill_tail(misc_ref, zero_ref, h_ref, fill_sem)

    writer.retire_older(w, j, nj)

    @pl.when(wc_ref[w] > 0)
    def _():
        half = xs_ref.shape[1]

        @pl.when(j == 0)
        def _():
            u = xs_ref[...]
            xb_ref[:, :half] = _unpack_lo(u).astype(BF16)
            xb_ref[:, half:] = _unpack_hi(u).astype(BF16)

        tn = wg_ref.shape[2]
        wcat_ref[:, :tn] = wg_ref[0].astype(BF16)
        wcat_ref[:, tn:] = wu_ref[0].astype(BF16)
        col = pl.multiple_of(j * tn, tn)
        bg = b_ref[0, :, pl.ds(col, tn)]
        bu = b_ref[0, :, pl.ds(pl.multiple_of(b_ref.shape[2] // 2 + col, tn), tn)]
        mc = xb_ref.shape[0] // MOE_UP_ROW_CHUNKS
        for mi in range(MOE_UP_ROW_CHUNKS):
            rows = slice(mi * mc, (mi + 1) * mc)
            r = jnp.dot(xb_ref[rows, :], wcat_ref[...], preferred_element_type=F32)
            gate = r[:, :tn] + bg
            up = r[:, tn:] + bu
            gate = jnp.minimum(gate, SWIGLU_LIMIT)
            up = jnp.clip(up, -SWIGLU_LIMIT, SWIGLU_LIMIT)
            act = (up + 1.0) * (gate * jax.nn.sigmoid(SWIGLU_ALPHA * gate))
            stage_ref[slot, rows, :] = act.astype(stage_ref.dtype)
        writer.start(slot, w, j)

    writer.drain(w, j, nw, nj)


def _moe_up(xs, w_gu, b_gu, we, wr, wc, misc, *, d_ff):
    R, half = xs.shape
    E, D, _ = w_gu.shape
    tm, tn = MOE_TM, MOE_TN_UP
    nj = d_ff // tn
    nw = we.shape[0]
    assert nj >= 2 and R % MOE_PAD == 0 and tm % (16 * MOE_UP_ROW_CHUNKS) == 0
    jeff = lambda w, j, wc: jnp.where(wc[w] > 0, j, nj - 1)
    return pl.pallas_call(
        _moe_up_kernel,
        grid_spec=pltpu.PrefetchScalarGridSpec(
            num_scalar_prefetch=4,
            grid=(nw, nj),
            in_specs=[pl.BlockSpec((pl.Element(tm), pl.Element(half)),
                                   lambda w, j, we, wr, wc, ms: (pl.multiple_of(wr[w], MOE_PAD), 0)),
                      pl.BlockSpec((1, D, tn), lambda w, j, we, wr, wc, ms: (we[w], 0, jeff(w, j, wc))),
                      pl.BlockSpec((1, D, tn), lambda w, j, we, wr, wc, ms: (we[w], 0, nj + jeff(w, j, wc))),
                      pl.BlockSpec((1, 1, 2 * nj * tn), lambda w, j, we, wr, wc, ms: (we[w], 0, 0))],
            out_specs=pl.BlockSpec(memory_space=pl.ANY),
            scratch_shapes=[pltpu.VMEM((tm, 2 * half), BF16),
                            pltpu.VMEM((D, 2 * tn), BF16),
                            pltpu.VMEM((2, tm, tn), BF16),
                            pltpu.VMEM((MOE_PAD, d_ff), BF16),
                            pltpu.SemaphoreType.DMA((2,)), pltpu.SemaphoreType.DMA(())]),
        out_shape=jax.ShapeDtypeStruct((R, d_ff), BF16),
        compiler_params=_cparams(("arbitrary", "arbitrary")),
        name="moe_up",
    )(we, wr, wc, misc, xs, w_gu, w_gu, b_gu)


def _moe_down_kernel(we_ref, wr_ref, wc_ref, misc_ref, h_ref, wlo_ref, whi_ref, b_ref,
                     o_ref, wcat_ref, stage_ref, zero_ref, sem, fill_sem):
    w = pl.program_id(0)
    j = pl.program_id(1)
    nw = pl.num_programs(0)
    nj = pl.num_programs(1)
    slot = (w * nj + j) % 2
    writer = _TileWriter(wr_ref, wc_ref, stage_ref, o_ref, sem, stage_ref.shape[2])

    @pl.when((w == 0) & (j == 0))
    def _():
        zero_ref[...] = jnp.zeros_like(zero_ref)
        _fill_tail(misc_ref, zero_ref, o_ref, fill_sem)

    writer.retire_older(w, j, nj)

    @pl.when(wc_ref[w] > 0)
    def _():
        tn = wlo_ref.shape[2]
        wcat_ref[:, :tn] = wlo_ref[0].astype(BF16)
        wcat_ref[:, tn:] = whi_ref[0].astype(BF16)
        r = jnp.dot(h_ref[...], wcat_ref[...], preferred_element_type=F32)
        col = pl.multiple_of(j * tn, tn)
        blo = b_ref[0, :, pl.ds(col, tn)]
        bhi = b_ref[0, :, pl.ds(pl.multiple_of(b_ref.shape[2] // 2 + col, tn), tn)]
        stage_ref[slot] = _pack_pairs(r[:, :tn] + blo, r[:, tn:] + bhi)
        writer.start(slot, w, j)

    writer.drain(w, j, nw, nj)


def _moe_down(h, w_down, b_down, we, wr, wc, misc):
    R, d_ff = h.shape
    E, _, D = w_down.shape
    tm, tn = MOE_TM, MOE_TN_DOWN
    half = D // 2
    nj = half // tn
    nw = we.shape[0]
    assert nj >= 2 and R % MOE_PAD == 0
    jeff = lambda w, j, wc: jnp.where(wc[w] > 0, j, nj - 1)
    return pl.pallas_call(
        _moe_down_kernel,
        grid_spec=pltpu.PrefetchScalarGridSpec(
            num_scalar_prefetch=4,
            grid=(nw, nj),
            in_specs=[pl.BlockSpec((pl.Element(tm), pl.Element(d_ff)),
                                   lambda w, j, we, wr, wc, ms: (pl.multiple_of(wr[w], MOE_PAD), 0)),
                      pl.BlockSpec((1, d_ff, tn), lambda w, j, we, wr, wc, ms: (we[w], 0, jeff(w, j, wc))),
                      pl.BlockSpec((1, d_ff, tn), lambda w, j, we, wr, wc, ms: (we[w], 0, nj + jeff(w, j, wc))),
                      pl.BlockSpec((1, 1, 2 * nj * tn), lambda w, j, we, wr, wc, ms: (we[w], 0, 0))],
            out_specs=pl.BlockSpec(memory_space=pl.ANY),
            scratch_shapes=[pltpu.VMEM((d_ff, 2 * tn), BF16),
                            pltpu.VMEM((2, tm, tn), U32),
                            pltpu.VMEM((MOE_PAD, half), U32),
                            pltpu.SemaphoreType.DMA((2,)), pltpu.SemaphoreType.DMA(())]),
        out_shape=jax.ShapeDtypeStruct((R, half), U32),
        compiler_params=_cparams(("arbitrary", "arbitrary")),
        name="moe_down",
    )(we, wr, wc, misc, h, w_down, w_down, b_down)


def _combine_kernel(dest_ref, dnext_ref, eo_ref, gate_ref, x1_ref, g_ref, b_ref, o_ref, buf_ref, sem,
                    *, tc):
    i = pl.program_id(0)
    n = tc * TOP_K
    slot = i % 2

    def gather(d_ref, slot):
        def issue(t, carry):
            for k in range(TOP_K):
                pltpu.make_async_copy(eo_ref.at[pl.ds(d_ref[0, 0, t * TOP_K + k], 1), :],
                                      buf_ref.at[slot, pl.ds(k * tc + t, 1), :],
                                      sem.at[slot]).start()
            return carry

        lax.fori_loop(0, tc, issue, 0, unroll=2)

    @pl.when(i == 0)
    def _():
        gather(dest_ref, 0)

    @pl.when(i + 1 < pl.num_programs(0))
    def _():
        gather(dnext_ref, 1 - slot)

    pltpu.make_async_copy(eo_ref.at[pl.ds(0, n), :], buf_ref.at[slot], sem.at[slot]).wait()

    gates = gate_ref[...]
    half = buf_ref.shape[2]
    cw = 2 * LANES
    row_sum = jnp.zeros((tc, 1), F32)
    for c in range(half // cw):
        lo_cols = slice(c * cw, (c + 1) * cw)
        hi_cols = slice(half + c * cw, half + (c + 1) * cw)
        lo = DEEPNORM_ALPHA * x1_ref[:, lo_cols]
        hi = DEEPNORM_ALPHA * x1_ref[:, hi_cols]
        for k in range(TOP_K):
            u = buf_ref[slot, k * tc:(k + 1) * tc, lo_cols]
            gk = gates[:, k:k + 1]
            lo = lo + gk * _unpack_lo(u)
            hi = hi + gk * _unpack_hi(u)
        o_ref[:, lo_cols] = lo
        o_ref[:, hi_cols] = hi
        row_sum = row_sum + jnp.sum(lo, axis=-1, keepdims=True) + jnp.sum(hi, axis=-1, keepdims=True)
    inv_d = 1.0 / (2 * half)
    mu = row_sum * inv_d
    sq_sum = jnp.zeros((tc, 1), F32)
    for c in range(2 * half // cw):
        d = o_ref[:, c * cw:(c + 1) * cw] - mu
        sq_sum = sq_sum + jnp.sum(d * d, axis=-1, keepdims=True)
    rstd = lax.rsqrt(sq_sum * inv_d + LN_EPS)
    for c in range(2 * half // cw):
        cols = slice(c * cw, (c + 1) * cw)
        o_ref[:, cols] = (o_ref[:, cols] - mu) * rstd * g_ref[:, cols] + b_ref[:, cols]


def _combine(dest, eo, gates, x1, g, b):
    T, D = x1.shape
    tc = COMBINE_TC
    n = tc * TOP_K
    nt = T // tc
    half = eo.shape[1]
    row = pl.BlockSpec((tc, D), lambda i: (i, 0))
    vec = pl.BlockSpec((1, D), lambda i: (0, 0))
    dest3 = dest.reshape(nt, 1, n)
    return pl.pallas_call(
        functools.partial(_combine_kernel, tc=tc),
        grid=(nt,),
        in_specs=[pl.BlockSpec((1, 1, n), lambda i: (i, 0, 0), memory_space=pltpu.SMEM),
                  pl.BlockSpec((1, 1, n), lambda i: (jnp.minimum(i + 1, nt - 1), 0, 0),
                               memory_space=pltpu.SMEM),
                  pl.BlockSpec(memory_space=pl.ANY),
                  pl.BlockSpec((tc, TOP_K), lambda i: (i, 0)),
                  row, vec, vec],
        out_specs=row,
        out_shape=jax.ShapeDtypeStruct((T, D), F32),
        scratch_shapes=[pltpu.VMEM((2, n, half), U32), pltpu.SemaphoreType.DMA((2,))],
        compiler_params=_cparams(("arbitrary",)),
        name="combine_rows",
    )(dest3, dest3, eo, gates, x1, g, b)


def _work_items(cnt, starts, n_items):
    chunks_per_pass = MOE_TM // MOE_PAD
    gchunks = (cnt + MOE_PAD - 1) // MOE_PAD
    npass = (gchunks + chunks_per_pass - 1) // chunks_per_pass
    cum = jnp.cumsum(npass)
    total = cum[-1]
    w = jnp.arange(n_items, dtype=I32)
    e = jnp.minimum(jnp.sum((cum[None, :] <= w[:, None]).astype(I32), axis=1), N_EXPERTS - 1)
    p = w - (cum - npass)[e]
    row = starts[e] + p * MOE_TM
    nch = jnp.clip(gchunks[e] - p * chunks_per_pass, 0, chunks_per_pass)
    valid = w < total
    last = jnp.maximum(total - 1, 0)
    e = jnp.where(valid, e, e[last]).astype(I32)
    row = jnp.where(valid, row, row[last]).astype(I32)
    nch = jnp.where(valid, nch, 0).astype(I32)
    misc = jnp.sum(gchunks).astype(I32).reshape(1)
    fillc = jnp.where(cnt % MOE_PAD != 0, starts // MOE_PAD + gchunks - 1, -1).astype(I32)
    return e, row, nch, misc, fillc


def kernel(x, mem, rel_table, w_in, w_mem_kv, w_o, lambda_q1, lambda_k1, lambda_q2, lambda_k2, subln_g, conv_w, conv_b, conv_ln_g, conv_ln_b, ln1_g, ln1_b, w_router, b_router, w_gate_up, b_gate_up, w_down, b_down, ln2_g, ln2_b):
    B, S, D = x.shape
    T = B * S
    n_heads = rel_table.shape[1]
    qk_w = n_heads * 2 * HEAD_DIM
    v_w = n_heads * V_DIM
    C = conv_w.shape[-1]
    mem_w = w_mem_kv.shape[-1] // 2
    mem_heads = 4
    mem_hd = mem_w // mem_heads
    in_cols = w_in.shape[-1]
    assert in_cols == 2 * qk_w + v_w + 2 * C + mem_w
    d_ff = w_down.shape[2]
    assert DEPTH == 1 and w_in.shape[0] == 1

    col_scale = jnp.concatenate([
        jnp.full((qk_w,), HEAD_DIM ** -0.5, F32),
        jnp.ones((in_cols - qk_w - mem_w,), F32),
        jnp.full((mem_w,), mem_hd ** -0.5, F32)]).reshape(1, in_cols)

    xt = x.reshape(T, D)
    proj = _proj_matmul(xt.astype(BF16), w_in[0], col_scale, tm=1024, tn=512, out_dtype=BF16)
    proj3 = proj.reshape(B, S, in_cols)

    bias = _bias_tiles(rel_table, ATT_T)
    lam_params = jnp.concatenate([lambda_q1, lambda_k1, lambda_q2, lambda_k2], axis=0).astype(F32)
    a_out = _diff_attention(proj3, bias, lam_params, subln_g, n_heads=n_heads,
                            k_col=qk_w // (2 * HEAD_DIM), v_col=2 * qk_w // V_DIM)

    conv_col = (2 * qk_w + v_w) // C
    c_out = _conformer_conv(proj3, conv_w[0, :, 0, :], conv_b, conv_ln_g, conv_ln_b,
                            a_col=conv_col, g_col=conv_col + 1)

    n_mem = mem.shape[1]
    kv = _proj_matmul(mem.reshape(B * n_mem, D).astype(BF16), w_mem_kv[0],
                      jnp.ones((1, 2 * mem_w), F32), tm=B * n_mem, tn=512, out_dtype=BF16)
    m_out = _memory_attention(proj3, kv.reshape(B, n_mem, 2 * mem_w), n_heads=mem_heads,
                              q_col=(in_cols - mem_w) // mem_hd, hd=mem_hd)

    mix = _out_projection(a_out.reshape(T, v_w), c_out.reshape(T, C), m_out.reshape(T, mem_w),
                          w_o[0], tm=1024, tn=512)

    x1, xp, topi, gates = _ln_router(xt, mix, ln1_g, ln1_b, w_router[0],
                                     b_router.reshape(1, -1), tm=256)

    dest, cnt8, starts8 = _rank_assignments(topi)
    cnt, starts = cnt8[0, :N_EXPERTS], starts8[0, :N_EXPERTS]

    n_assign = T * TOP_K
    n_items = n_assign // MOE_TM + N_EXPERTS
    max_rows = n_assign + N_EXPERTS * (MOE_PAD - 1) + MOE_TM
    n_rows = -(-max_rows // MOE_PAD) * MOE_PAD
    we, wr, wc, misc, fillc = _work_items(cnt, starts, n_items)

    xs = _dispatch(dest, xp, fillc, misc, n_rows)
    h = _moe_up(xs, w_gate_up[0], b_gate_up[0].reshape(N_EXPERTS, 1, -1), we, wr, wc, misc, d_ff=d_ff)
    eo = _moe_down(h, w_down[0], b_down[0].reshape(N_EXPERTS, 1, -1), we, wr, wc, misc)
    out = _combine(dest, eo, gates, x1, ln2_g, ln2_b)
    return out.reshape(B, S, D)
```

```python
import functools
import math

import jax
import jax.numpy as jnp
from jax import lax
from jax.experimental import pallas as pl
from jax.experimental.pallas import tpu as pltpu

F32 = jnp.float32
BF16 = jnp.bfloat16
I32 = jnp.int32
U32 = jnp.uint32

HEAD_DIM = 128
V_DIM = 2 * HEAD_DIM
CONV_TAPS = 31
NUM_BUCKETS = 32
MAX_EXACT = NUM_BUCKETS // 2
MAX_DISTANCE = 128
N_EXPERTS = 32
TOP_K = 4
SWIGLU_LIMIT = 7.0
SWIGLU_ALPHA = 1.702
LN_EPS = 1e-5
DEPTH = 1
DEEPNORM_ALPHA = (2 * DEPTH) ** 0.25
LAM_INIT = 0.8 - 0.6 * math.exp(-0.3 * 0)

LANES = 128
V7X_VMEM_BYTES = 64 * 1024 * 1024
VMEM_LIMIT = 58 * 1024 * 1024

ATT_T = 256
CONV_TS = 256
CONV_HALO = 32
MEM_TQ = 512
MOE_TM = 1088
MOE_PAD = 64
MOE_TN_UP = 256
MOE_UP_ROW_CHUNKS = 2
MOE_TN_DOWN = 512
RANK_TT = 512
DISPATCH_TA = 1024
COMBINE_TC = 128


def _cparams(sem, vmem=VMEM_LIMIT):
    return pltpu.CompilerParams(dimension_semantics=sem, vmem_limit_bytes=vmem)


def _proj_kernel(a_ref, w_ref, cs_ref, o_ref):
    acc = jnp.dot(a_ref[...], w_ref[...].astype(BF16), preferred_element_type=F32)
    o_ref[...] = (acc * cs_ref[...]).astype(o_ref.dtype)


def _proj_matmul(a, w, col_scale, *, tm, tn, out_dtype):
    M, K = a.shape
    N = w.shape[1]
    return pl.pallas_call(
        _proj_kernel,
        grid=(M // tm, N // tn),
        in_specs=[pl.BlockSpec((tm, K), lambda i, j: (i, 0)),
                  pl.BlockSpec((K, tn), lambda i, j: (0, j)),
                  pl.BlockSpec((1, tn), lambda i, j: (0, j))],
        out_specs=pl.BlockSpec((tm, tn), lambda i, j: (i, j)),
        out_shape=jax.ShapeDtypeStruct((M, N), out_dtype),
        compiler_params=_cparams(("parallel", "arbitrary")),
        name="proj_matmul",
    )(a, w, col_scale)


def _bias_kernel(tbl_ref, o_ref, *, t, n_heads):
    h = pl.program_id(0)
    j = lax.broadcasted_iota(I32, (t, t), 0)
    i = lax.broadcasted_iota(I32, (t, t), 1)
    for o in range(2):
        dist = o * t + i - j
        n = jnp.maximum(dist, 0)
        nf = jnp.maximum(n, 1).astype(F32)
        large = MAX_EXACT + (jnp.log(nf / MAX_EXACT) / math.log(MAX_DISTANCE / MAX_EXACT)
                             * (NUM_BUCKETS - MAX_EXACT)).astype(I32)
        large = jnp.minimum(large, NUM_BUCKETS - 1)
        bucket = jnp.where(n < MAX_EXACT, n, large)
        val = jnp.zeros((t, t), F32)
        for b in range(NUM_BUCKETS):
            val = jnp.where(bucket == b, tbl_ref[b * n_heads + h], val)
        if o == 0:
            val = jnp.where(dist >= 0, val, -jnp.inf)
        o_ref[0, o] = val
    o_ref[0, 2] = jnp.full((t, t), tbl_ref[(NUM_BUCKETS - 1) * n_heads + h], F32)


def _bias_tiles(rel_table, t):
    n_heads = rel_table.shape[1]
    assert t + 1 >= MAX_DISTANCE
    return pl.pallas_call(
        functools.partial(_bias_kernel, t=t, n_heads=n_heads),
        grid=(n_heads,),
        in_specs=[pl.BlockSpec(memory_space=pltpu.SMEM)],
        out_specs=pl.BlockSpec((1, 3, t, t), lambda h: (h, 0, 0, 0)),
        out_shape=jax.ShapeDtypeStruct((n_heads, 3, t, t), F32),
        compiler_params=_cparams(("arbitrary",)),
        name="bias_tiles",
    )(rel_table.reshape(-1))


def _attn_kernel(lam_ref, q_ref, k_ref, v_ref, bias_ref, g_ref, o_ref, vt_ref, acc_ref, m_ref, l_ref,
                 *, t):
    qi = pl.program_id(2)
    n_kt = v_ref.shape[1] // t

    @pl.when(qi == 0)
    def _():
        for kb in range(n_kt):
            vt_ref[:, kb * t:(kb + 1) * t] = v_ref[0, kb * t:(kb + 1) * t, :].astype(F32).T.astype(BF16)

    q = q_ref[0]
    qs = (q[:, :HEAD_DIM], q[:, HEAD_DIM:])
    dn = (((1,), (1,)), ((), ()))
    acc_ref[...] = jnp.zeros_like(acc_ref)
    m_ref[...] = jnp.full(m_ref.shape, -jnp.inf, F32)
    l_ref[...] = jnp.zeros_like(l_ref)

    def update(kb0, n_tiles):
        start = pl.multiple_of(kb0 * t, t)
        k = k_ref[0, pl.ds(start, n_tiles * t), :]
        vt = vt_ref[:, pl.ds(start, n_tiles * t)]
        bias = [bias_ref[0, jnp.minimum(qi - kb0 - i, 2)] for i in range(n_tiles)]
        for c in range(2):
            s = [lax.dot_general(k[i * t:(i + 1) * t, c * HEAD_DIM:(c + 1) * HEAD_DIM], qs[c], dn,
                                 preferred_element_type=F32) + bias[i]
                 for i in range(n_tiles)]
            m_old = m_ref[c]
            m_new = m_old
            for si in s:
                m_new = jnp.maximum(m_new, jnp.max(si, axis=0, keepdims=True))
            alpha = jnp.exp(m_old - m_new)
            p = [jnp.exp(si - m_new) for si in s]
            l_new = alpha * l_ref[c]
            for pi in p:
                l_new = l_new + jnp.sum(pi, axis=0, keepdims=True)
            pcat = p[0].astype(BF16) if n_tiles == 1 else jnp.concatenate(
                [pi.astype(BF16) for pi in p], axis=0)
            acc_ref[c] = alpha * acc_ref[c] + jnp.dot(vt, pcat, preferred_element_type=F32)
            m_ref[c] = m_new
            l_ref[c] = l_new

    n = qi + 1

    def pair(i, carry):
        update(2 * i, 2)
        return carry

    lax.fori_loop(0, n // 2, pair, 0)

    @pl.when(n % 2 == 1)
    def _():
        update(qi, 1)

    lp = lam_ref[...]
    lam = (jnp.exp(jnp.sum(lp[0:1] * lp[1:2], axis=-1, keepdims=True))
           - jnp.exp(jnp.sum(lp[2:3] * lp[3:4], axis=-1, keepdims=True)) + LAM_INIT)
    ot = acc_ref[0] * (1.0 / l_ref[0]) - acc_ref[1] * (lam / l_ref[1])
    ot = ot * lax.rsqrt(jnp.mean(jnp.square(ot), axis=0, keepdims=True) + LN_EPS)
    o = ot.T * g_ref[...]
    o_ref[0] = (o * (1.0 - LAM_INIT)).astype(o_ref.dtype)


def _diff_attention(proj3, bias, lam_params, subln_g, *, n_heads, k_col, v_col):
    B, S, _ = proj3.shape
    t = ATT_T
    nq = S // t
    w = 2 * HEAD_DIM
    return pl.pallas_call(
        functools.partial(_attn_kernel, t=t),
        grid=(B, n_heads, nq),
        in_specs=[pl.BlockSpec((4, HEAD_DIM), lambda b, h, i: (0, 0)),
                  pl.BlockSpec((1, t, w), lambda b, h, i: (b, i, h)),
                  pl.BlockSpec((1, S, w), lambda b, h, i: (b, 0, k_col + h)),
                  pl.BlockSpec((1, S, w), lambda b, h, i: (b, 0, v_col + h)),
                  pl.BlockSpec((1, 3, t, t), lambda b, h, i: (h, 0, 0, 0)),
                  pl.BlockSpec((1, V_DIM), lambda b, h, i: (0, 0))],
        out_specs=pl.BlockSpec((1, t, V_DIM), lambda b, h, i: (b, i, h)),
        out_shape=jax.ShapeDtypeStruct((B, S, n_heads * V_DIM), BF16),
        scratch_shapes=[pltpu.VMEM((V_DIM, S), BF16), pltpu.VMEM((2, V_DIM, t), F32),
                        pltpu.VMEM((2, 1, t), F32), pltpu.VMEM((2, 1, t), F32)],
        compiler_params=_cparams(("parallel", "parallel", "arbitrary")),
        name="diff_attention",
    )(lam_params, proj3, proj3, proj3, bias, subln_g)


def _conv_kernel(a_ref, g_ref, ap_ref, gp_ref, w_ref, cb_ref, lg_ref, lb_ref, o_ref,
                 hs_ref, cv_ref, *, ts, halo, taps, rc):
    i = pl.program_id(1)
    C = a_ref.shape[-1]
    hs_ref[halo:, :] = a_ref[0].astype(F32) * jax.nn.sigmoid(g_ref[0].astype(F32))
    hp = ap_ref[0].astype(F32) * jax.nn.sigmoid(gp_ref[0].astype(F32))
    hs_ref[:halo, :] = jnp.where(i > 0, hp, 0.0)
    base = halo - (taps - 1)
    for c in range(C // LANES):
        cs = slice(c * LANES, (c + 1) * LANES)
        for r in range(ts // rc):
            acc = jnp.zeros((rc, LANES), F32)
            for tp in range(taps):
                acc = acc + w_ref[tp:tp + 1, cs] * hs_ref[pl.ds(base + tp + r * rc, rc), cs]
            cv_ref[r * rc:(r + 1) * rc, cs] = acc + cb_ref[:, cs]
    h = cv_ref[...]
    mu = jnp.mean(h, axis=-1, keepdims=True)
    var = jnp.mean(jnp.square(h - mu), axis=-1, keepdims=True)
    y = (h - mu) * lax.rsqrt(var + LN_EPS) * lg_ref[...] + lb_ref[...]
    o_ref[0] = (y * jax.nn.sigmoid(y)).astype(o_ref.dtype)


def _conformer_conv(proj3, conv_w, conv_b, ln_g, ln_b, *, a_col, g_col):
    B, S, _ = proj3.shape
    taps, C = conv_w.shape
    ts, halo = CONV_TS, CONV_HALO
    assert halo >= taps - 1
    hb = ts // halo
    prev = lambda col: (lambda b, i: (b, jnp.maximum(i * hb - 1, 0), col))
    vec = pl.BlockSpec((1, C), lambda b, i: (0, 0))
    return pl.pallas_call(
        functools.partial(_conv_kernel, ts=ts, halo=halo, taps=taps, rc=128),
        grid=(B, S // ts),
        in_specs=[pl.BlockSpec((1, ts, C), lambda b, i: (b, i, a_col)),
                  pl.BlockSpec((1, ts, C), lambda b, i: (b, i, g_col)),
                  pl.BlockSpec((1, halo, C), prev(a_col)),
                  pl.BlockSpec((1, halo, C), prev(g_col)),
                  pl.BlockSpec((taps, C), lambda b, i: (0, 0)),
                  vec, vec, vec],
        out_specs=pl.BlockSpec((1, ts, C), lambda b, i: (b, i, 0)),
        out_shape=jax.ShapeDtypeStruct((B, S, C), BF16),
        scratch_shapes=[pltpu.VMEM((halo + ts, C), F32), pltpu.VMEM((ts, C), F32)],
        compiler_params=_cparams(("parallel", "arbitrary")),
        name="conformer_conv",
    )(proj3, proj3, proj3, proj3, conv_w, conv_b, ln_g, ln_b)


def _mem_attn_kernel(q_ref, k_ref, v_ref, o_ref):
    s = lax.dot_general(q_ref[0], k_ref[0], (((1,), (1,)), ((), ())), preferred_element_type=F32)
    m = jnp.max(s, axis=-1, keepdims=True)
    p = jnp.exp(s - m)
    p = p / jnp.sum(p, axis=-1, keepdims=True)
    o_ref[0] = jnp.dot(p.astype(BF16), v_ref[0], preferred_element_type=F32).astype(o_ref.dtype)


def _memory_attention(proj3, kv3, *, n_heads, q_col, hd):
    B, S, _ = proj3.shape
    M = kv3.shape[1]
    tq = MEM_TQ
    return pl.pallas_call(
        _mem_attn_kernel,
        grid=(B, n_heads, S // tq),
        in_specs=[pl.BlockSpec((1, tq, hd), lambda b, h, i: (b, i, q_col + h)),
                  pl.BlockSpec((1, M, hd), lambda b, h, i: (b, 0, h)),
                  pl.BlockSpec((1, M, hd), lambda b, h, i: (b, 0, n_heads + h))],
        out_specs=pl.BlockSpec((1, tq, hd), lambda b, h, i: (b, i, h)),
        out_shape=jax.ShapeDtypeStruct((B, S, n_heads * hd), BF16),
        compiler_params=_cparams(("parallel", "parallel", "arbitrary")),
        name="memory_attention",
    )(proj3, kv3, kv3)


def _oproj_kernel(a_ref, c_ref, m_ref, wa_ref, wc_ref, wm_ref, o_ref):
    acc = jnp.dot(a_ref[...], wa_ref[...].astype(BF16), preferred_element_type=F32)
    acc += jnp.dot(c_ref[...], wc_ref[...].astype(BF16), preferred_element_type=F32)
    acc += jnp.dot(m_ref[...], wm_ref[...].astype(BF16), preferred_element_type=F32)
    o_ref[...] = acc


def _out_projection(a, c, m, w_o, *, tm, tn):
    M, Ka = a.shape
    Kc, Km = c.shape[1], m.shape[1]
    assert Kc == Km and Ka % Kc == 0
    N = w_o.shape[1]
    return pl.pallas_call(
        _oproj_kernel,
        grid=(M // tm, N // tn),
        in_specs=[pl.BlockSpec((tm, Ka), lambda i, j: (i, 0)),
                  pl.BlockSpec((tm, Kc), lambda i, j: (i, 0)),
                  pl.BlockSpec((tm, Km), lambda i, j: (i, 0)),
                  pl.BlockSpec((Ka, tn), lambda i, j: (0, j)),
                  pl.BlockSpec((Kc, tn), lambda i, j: (Ka // Kc, j)),
                  pl.BlockSpec((Km, tn), lambda i, j: (Ka // Kc + 1, j))],
        out_specs=pl.BlockSpec((tm, tn), lambda i, j: (i, j)),
        out_shape=jax.ShapeDtypeStruct((M, N), F32),
        compiler_params=_cparams(("parallel", "arbitrary")),
        name="out_projection",
    )(a, c, m, w_o, w_o, w_o)


def _pack_pairs(lo, hi):
    lo_b = lax.bitcast_convert_type(lo.astype(BF16).astype(F32), U32)
    hi_b = lax.bitcast_convert_type(hi.astype(BF16).astype(F32), U32)
    return (lo_b >> 16) | (hi_b & jnp.uint32(0xFFFF0000))


def _unpack_lo(u):
    return lax.bitcast_convert_type(u << 16, F32)


def _unpack_hi(u):
    return lax.bitcast_convert_type(u & jnp.uint32(0xFFFF0000), F32)


def _layer_norm(y, g, b):
    mu = jnp.mean(y, axis=-1, keepdims=True)
    var = jnp.mean(jnp.square(y - mu), axis=-1, keepdims=True)
    return (y - mu) * lax.rsqrt(var + LN_EPS) * g + b


def _ln_router_kernel(x_ref, mix_ref, g_ref, b_ref, wr_ref, br_ref,
                      x1_ref, xp_ref, topi_ref, gate_ref):
    x1 = _layer_norm(DEEPNORM_ALPHA * x_ref[...] + mix_ref[...], g_ref[...], b_ref[...])
    x1_ref[...] = x1
    half = x1.shape[1] // 2
    xp_ref[...] = _pack_pairs(x1[:, :half], x1[:, half:])
    logits = jnp.dot(x1, wr_ref[...], preferred_element_type=F32,
                     precision=lax.Precision.HIGHEST) + br_ref[...]
    tm, ne = logits.shape
    lane = lax.broadcasted_iota(I32, (tm, ne), 1)
    kk = lax.broadcasted_iota(I32, (tm, TOP_K), 1)
    idx = jnp.zeros((tm, TOP_K), I32)
    val = jnp.zeros((tm, TOP_K), F32)
    cur = logits
    for k in range(TOP_K):
        mx = jnp.max(cur, axis=-1, keepdims=True)
        ix = jnp.min(jnp.where(cur == mx, lane, ne), axis=-1, keepdims=True)
        idx = jnp.where(kk == k, ix, idx)
        val = jnp.where(kk == k, mx, val)
        cur = jnp.where(lane == ix, -jnp.inf, cur)
    e = jnp.exp(val - jnp.max(val, axis=-1, keepdims=True))
    topi_ref[...] = idx
    gate_ref[...] = e / jnp.sum(e, axis=-1, keepdims=True)


def _ln_router(x, mix, g, b, w_router, b_router, *, tm):
    T, D = x.shape
    ne = w_router.shape[1]
    row = pl.BlockSpec((tm, D), lambda i: (i, 0))
    vec = pl.BlockSpec((1, D), lambda i: (0, 0))
    return pl.pallas_call(
        _ln_router_kernel,
        grid=(T // tm,),
        in_specs=[row, row, vec, vec,
                  pl.BlockSpec((D, ne), lambda i: (0, 0)),
                  pl.BlockSpec((1, ne), lambda i: (0, 0))],
        out_specs=[row,
                   pl.BlockSpec((tm, D // 2), lambda i: (i, 0)),
                   pl.BlockSpec((tm, TOP_K), lambda i: (i, 0)),
                   pl.BlockSpec((tm, TOP_K), lambda i: (i, 0))],
        out_shape=[jax.ShapeDtypeStruct((T, D), F32),
                   jax.ShapeDtypeStruct((T, D // 2), U32),
                   jax.ShapeDtypeStruct((T, TOP_K), I32),
                   jax.ShapeDtypeStruct((T, TOP_K), F32)],
        compiler_params=_cparams(("parallel",)),
        name="ln_router",
    )(x, mix, g, b, w_router, b_router)


def _rank_kernel(topi_ref, dest_ref, cnt_ref, ps_ref, rank_ref, carry_ref, *, tt, pad):
    ph = pl.program_id(0)
    i = pl.program_id(1)
    lane = lax.broadcasted_iota(I32, (tt, LANES), 1)
    kk = lax.broadcasted_iota(I32, (tt, TOP_K), 1)
    ti = topi_ref[...]
    sel = [lane == ti[:, k:k + 1] for k in range(TOP_K)]
    rows = pl.ds(pl.multiple_of(i * tt, tt), tt)

    @pl.when(ph == 0)
    def _():
        @pl.when(i == 0)
        def _():
            carry_ref[...] = jnp.zeros_like(carry_ref)

        oh = jnp.zeros((tt, LANES), F32)
        for k in range(TOP_K):
            oh = oh + sel[k].astype(F32)
        r = lax.broadcasted_iota(I32, (tt, tt), 0)
        c = lax.broadcasted_iota(I32, (tt, tt), 1)
        lower = (r > c).astype(BF16)
        before = jnp.dot(lower, oh.astype(BF16), preferred_element_type=F32) + carry_ref[0:1, :]
        rank = jnp.zeros((tt, TOP_K), F32)
        for k in range(TOP_K):
            rk = jnp.sum(jnp.where(sel[k], before, 0.0), axis=-1, keepdims=True)
            rank = jnp.where(kk == k, rk, rank)
        rank_ref[rows, :] = rank
        carry_ref[...] = carry_ref[...] + jnp.sum(oh, axis=0, keepdims=True)

    @pl.when(ph == 1)
    def _():
        cnt = carry_ref[...].astype(I32)
        padded = (cnt + (pad - 1)) & jnp.int32(-pad)
        l8 = lax.broadcasted_iota(I32, (8, LANES), 1)
        scan = padded
        for sh in (1, 2, 4, 8, 16, 32, 64):
            scan = scan + jnp.where(l8 >= sh, pltpu.roll(scan, sh, axis=1), 0)
        starts = scan - padded
        cnt_ref[...] = cnt
        ps_ref[...] = starts
        st = starts[0:1, :].astype(F32)
        rank = rank_ref[rows, :]
        dest = jnp.zeros((tt, TOP_K), F32)
        for k in range(TOP_K):
            base = jnp.sum(jnp.where(sel[k], st, 0.0), axis=-1, keepdims=True)
            dest = jnp.where(kk == k, base, dest)
        dest_ref[...] = (dest + rank).astype(I32)


def _rank_assignments(topi):
    T = topi.shape[0]
    tt = RANK_TT
    return pl.pallas_call(
        functools.partial(_rank_kernel, tt=tt, pad=MOE_PAD),
        grid=(2, T // tt),
        in_specs=[pl.BlockSpec((tt, TOP_K), lambda p, i: (i, 0))],
        out_specs=[pl.BlockSpec((tt, TOP_K), lambda p, i: (i * p, 0)),
                   pl.BlockSpec((8, LANES), lambda p, i: (0, 0)),
                   pl.BlockSpec((8, LANES), lambda p, i: (0, 0))],
        out_shape=[jax.ShapeDtypeStruct((T, TOP_K), I32),
                   jax.ShapeDtypeStruct((8, LANES), I32),
                   jax.ShapeDtypeStruct((8, LANES), I32)],
        scratch_shapes=[pltpu.VMEM((T, TOP_K), F32), pltpu.VMEM((8, LANES), F32)],
        compiler_params=_cparams(("arbitrary", "arbitrary")),
        name="rank_assignments",
    )(topi)


def _dispatch_kernel(fillc_ref, misc_ref, dest_ref, xp_ref, xs_ref, zero_ref, fill_sem, row_sem, *, ta):
    i = pl.program_id(0)
    tt = xp_ref.shape[0]

    @pl.when(i == 0)
    def _():
        def fill(e):
            row = pl.multiple_of(fillc_ref[e] * MOE_PAD, MOE_PAD)
            return pltpu.make_async_copy(zero_ref, xs_ref.at[pl.ds(row, MOE_PAD), :], fill_sem)

        def each_group(fn):
            for e in range(N_EXPERTS):
                pl.when(fillc_ref[e] >= 0)(functools.partial(fn, e))

        zero_ref[...] = jnp.zeros_like(zero_ref)
        each_group(lambda e: fill(e).start())
        _fill_tail(misc_ref, zero_ref, xs_ref, fill_sem)
        each_group(lambda e: fill(e).wait())

    def issue(t, carry):
        for k in range(TOP_K):
            pltpu.make_async_copy(xp_ref.at[pl.ds(t, 1), :],
                                  xs_ref.at[pl.ds(dest_ref[0, 0, t * TOP_K + k], 1), :],
                                  row_sem).start()
        return carry

    lax.fori_loop(0, tt, issue, 0, unroll=2)
    for _ in range(ta // tt):
        pltpu.make_async_copy(xp_ref, xs_ref.at[pl.ds(0, tt), :], row_sem).wait()


def _dispatch(dest, xp, fillc, misc, n_rows):
    T, W = xp.shape
    ta = DISPATCH_TA
    na = dest.size // ta
    assert n_rows % MOE_PAD == 0 and ta % TOP_K == 0
    return pl.pallas_call(
        functools.partial(_dispatch_kernel, ta=ta),
        grid_spec=pltpu.PrefetchScalarGridSpec(
            num_scalar_prefetch=2,
            grid=(na,),
            in_specs=[pl.BlockSpec((1, 1, ta), lambda i, fc, ms: (i, 0, 0), memory_space=pltpu.SMEM),
                      pl.BlockSpec((ta // TOP_K, W), lambda i, fc, ms: (i, 0))],
            out_specs=pl.BlockSpec(memory_space=pl.ANY),
            scratch_shapes=[pltpu.VMEM((MOE_PAD, W), U32),
                            pltpu.SemaphoreType.DMA(()), pltpu.SemaphoreType.DMA(())]),
        out_shape=jax.ShapeDtypeStruct((n_rows, W), U32),
        compiler_params=_cparams(("arbitrary",)),
        name="dispatch_rows",
    )(fillc, misc, dest.reshape(na, 1, ta), xp)


class _TileWriter:
    def __init__(self, wr_ref, wc_ref, stage_ref, out_ref, sem, tn):
        self.wr_ref, self.wc_ref = wr_ref, wc_ref
        self.stage_ref, self.out_ref, self.sem, self.tn = stage_ref, out_ref, sem, tn
        n_chunks = stage_ref.shape[1] // MOE_PAD
        self.sizes = [1 << b for b in reversed(range(n_chunks.bit_length()))]

    def _copy(self, slot, w, j, size):
        n = self.wc_ref[w]
        first = pl.multiple_of((n & -(2 * size)) * MOE_PAD, MOE_PAD)
        row = pl.multiple_of(self.wr_ref[w] + first, MOE_PAD)
        col = pl.multiple_of(j * self.tn, self.tn)
        return pltpu.make_async_copy(
            self.stage_ref.at[slot, pl.ds(first, size * MOE_PAD), :],
            self.out_ref.at[pl.ds(row, size * MOE_PAD), pl.ds(col, self.tn)],
            self.sem.at[slot])

    def _each_group(self, w, fn):
        for size in self.sizes:
            pl.when((self.wc_ref[w] & size) != 0)(functools.partial(fn, size))

    def start(self, slot, w, j):
        self._each_group(w, lambda size: self._copy(slot, w, j, size).start())

    def wait(self, slot, w, j):
        self._each_group(w, lambda size: self._copy(slot, w, j, size).wait())

    def retire_older(self, w, j, nj):
        step = w * nj + j

        @pl.when(step >= 2)
        def _():
            wrap = j < 2
            self.wait(step % 2, jnp.where(wrap, w - 1, w), jnp.where(wrap, j + nj - 2, j - 2))

    def drain(self, w, j, nw, nj):
        step = w * nj + j

        @pl.when(step == nw * nj - 1)
        def _():
            self.wait((step + 1) % 2, w, j - 1)
            self.wait(step % 2, w, j)


def _fill_tail(misc_ref, zero_ref, out_ref, sem):
    n_total = out_ref.shape[0] // MOE_PAD

    def fill(c):
        row = pl.multiple_of(c * MOE_PAD, MOE_PAD)
        return pltpu.make_async_copy(zero_ref, out_ref.at[pl.ds(row, MOE_PAD), :], sem)

    def start(c, carry):
        fill(c).start()
        return carry

    def wait(c, carry):
        fill(c).wait()
        return carry

    lax.fori_loop(misc_ref[0], n_total, start, 0)
    lax.fori_loop(misc_ref[0], n_total, wait, 0)


def _moe_up_kernel(we_ref, wr_ref, wc_ref, misc_ref, xs_ref, wg_ref, wu_ref, b_ref,
                   h_ref, xb_ref, wcat_ref, stage_ref, zero_ref, sem, fill_sem):
    w = pl.program_id(0)
    j = pl.program_id(1)
    nw = pl.num_programs(0)
    nj = pl.num_programs(1)
    slot = (w * nj + j) % 2
    writer = _TileWriter(wr_ref, wc_ref, stage_ref, h_ref, sem, stage_ref.shape[2])

    @pl.when((w == 0) & (j == 0))
    def _():
        zero_ref[...] = jnp.zeros_like(zero_ref)
        _fill_tail(misc_ref, zero_ref, h_ref, fill_sem)

    writer.retire_older(w, j, nj)

    @pl.when(wc_ref[w] > 0)
    def _():
        half = xs_ref.shape[1]

        @pl.when(j == 0)
        def _():
            u = xs_ref[...]
            xb_ref[:, :half] = _unpack_lo(u).astype(BF16)
            xb_ref[:, half:] = _unpack_hi(u).astype(BF16)

        tn = wg_ref.shape[2]
        wcat_ref[:, :tn] = wg_ref[0].astype(BF16)
        wcat_ref[:, tn:] = wu_ref[0].astype(BF16)
        col = pl.multiple_of(j * tn, tn)
        bg = b_ref[0, :, pl.ds(col, tn)]
        bu = b_ref[0, :, pl.ds(pl.multiple_of(b_ref.shape[2] // 2 + col, tn), tn)]
        mc = xb_ref.shape[0] // MOE_UP_ROW_CHUNKS
        for mi in range(MOE_UP_ROW_CHUNKS):
            rows = slice(mi * mc, (mi + 1) * mc)
            r = jnp.dot(xb_ref[rows, :], wcat_ref[...], preferred_element_type=F32)
            gate = r[:, :tn] + bg
            up = r[:, tn:] + bu
            gate = jnp.minimum(gate, SWIGLU_LIMIT)
            up = jnp.clip(up, -SWIGLU_LIMIT, SWIGLU_LIMIT)
            act = (up + 1.0) * (gate * jax.nn.sigmoid(SWIGLU_ALPHA * gate))
            stage_ref[slot, rows, :] = act.astype(stage_ref.dtype)
        writer.start(slot, w, j)

    writer.drain(w, j, nw, nj)


def _moe_up(xs, w_gu, b_gu, we, wr, wc, misc, *, d_ff):
    R, half = xs.shape
    E, D, _ = w_gu.shape
    tm, tn = MOE_TM, MOE_TN_UP
    nj = d_ff // tn
    nw = we.shape[0]
    assert nj >= 2 and R % MOE_PAD == 0 and tm % (16 * MOE_UP_ROW_CHUNKS) == 0
    jeff = lambda w, j, wc: jnp.where(wc[w] > 0, j, nj - 1)
    return pl.pallas_call(
        _moe_up_kernel,
        grid_spec=pltpu.PrefetchScalarGridSpec(
            num_scalar_prefetch=4,
            grid=(misc[1], nj),
            in_specs=[pl.BlockSpec((pl.Element(tm), pl.Element(half)),
                                   lambda w, j, we, wr, wc, ms: (pl.multiple_of(wr[w], MOE_PAD), 0)),
                      pl.BlockSpec((1, D, tn), lambda w, j, we, wr, wc, ms: (we[w], 0, jeff(w, j, wc))),
                      pl.BlockSpec((1, D, tn), lambda w, j, we, wr, wc, ms: (we[w], 0, nj + jeff(w, j, wc))),
                      pl.BlockSpec((1, 1, 2 * nj * tn), lambda w, j, we, wr, wc, ms: (we[w], 0, 0))],
            out_specs=pl.BlockSpec(memory_space=pl.ANY),
            scratch_shapes=[pltpu.VMEM((tm, 2 * half), BF16),
                            pltpu.VMEM((D, 2 * tn), BF16),
                            pltpu.VMEM((2, tm, tn), BF16),
                            pltpu.VMEM((MOE_PAD, d_ff), BF16),
                            pltpu.SemaphoreType.DMA((2,)), pltpu.SemaphoreType.DMA(())]),
        out_shape=jax.ShapeDtypeStruct((R, d_ff), BF16),
        compiler_params=_cparams(("arbitrary", "arbitrary")),
        name="moe_up",
    )(we, wr, wc, misc, xs, w_gu, w_gu, b_gu)


def _moe_down_kernel(we_ref, wr_ref, wc_ref, misc_ref, h_ref, w_ref, b_ref,
                     o_ref, stage_ref, zero_ref, sem, fill_sem):
    w = pl.program_id(0)
    j = pl.program_id(1)
    nw = pl.num_programs(0)
    nj = pl.num_programs(1)
    slot = (w * nj + j) % 2
    writer = _TileWriter(wr_ref, wc_ref, stage_ref, o_ref, sem, stage_ref.shape[2])

    @pl.when((w == 0) & (j == 0))
    def _():
        zero_ref[...] = jnp.zeros_like(zero_ref)
        _fill_tail(misc_ref, zero_ref, o_ref, fill_sem)

    writer.retire_older(w, j, nj)

    @pl.when(wc_ref[w] > 0)
    def _():
        tn = stage_ref.shape[2]
        r = jnp.dot(h_ref[...], w_ref[0].astype(BF16), preferred_element_type=F32)
        r = r + b_ref[0, :, pl.ds(pl.multiple_of(j * 2 * tn, 2 * tn), 2 * tn)]
        stage_ref[slot] = _pack_pairs(r[:, :tn], r[:, tn:])
        writer.start(slot, w, j)

    writer.drain(w, j, nw, nj)


def _moe_down(h, w_down, b_down, we, wr, wc, misc):
    R, d_ff = h.shape
    E, _, D = w_down.shape
    tm, tn = MOE_TM, MOE_TN_DOWN
    half = D // 2
    nj = half // tn
    nw = we.shape[0]
    assert nj >= 2 and R % MOE_PAD == 0
    jeff = lambda w, j, wc: jnp.where(wc[w] > 0, j, nj - 1)
    return pl.pallas_call(
        _moe_down_kernel,
        grid_spec=pltpu.PrefetchScalarGridSpec(
            num_scalar_prefetch=4,
            grid=(misc[1], nj),
            in_specs=[pl.BlockSpec((pl.Element(tm), pl.Element(d_ff)),
                                   lambda w, j, we, wr, wc, ms: (pl.multiple_of(wr[w], MOE_PAD), 0)),
                      pl.BlockSpec((1, d_ff, 2 * tn), lambda w, j, we, wr, wc, ms: (we[w], 0, jeff(w, j, wc))),
                      pl.BlockSpec((1, 1, 2 * nj * tn), lambda w, j, we, wr, wc, ms: (we[w], 0, 0))],
            out_specs=pl.BlockSpec(memory_space=pl.ANY),
            scratch_shapes=[pltpu.VMEM((2, tm, tn), U32),
                            pltpu.VMEM((MOE_PAD, half), U32),
                            pltpu.SemaphoreType.DMA((2,)), pltpu.SemaphoreType.DMA(())]),
        out_shape=jax.ShapeDtypeStruct((R, half), U32),
        compiler_params=_cparams(("arbitrary", "arbitrary")),
        name="moe_down",
    )(we, wr, wc, misc, h, w_down, b_down)


def _combine_kernel(dest_ref, dnext_ref, eo_ref, gate_ref, x1_ref, g_ref, b_ref, o_ref, buf_ref, sem,
                    *, tc):
    i = pl.program_id(0)
    n = tc * TOP_K
    slot = i % 2

    def gather(d_ref, slot):
        def issue(t, carry):
            for k in range(TOP_K):
                pltpu.make_async_copy(eo_ref.at[pl.ds(d_ref[0, 0, t * TOP_K + k], 1), :],
                                      buf_ref.at[slot, pl.ds(k * tc + t, 1), :],
                                      sem.at[slot]).start()
            return carry

        lax.fori_loop(0, tc, issue, 0, unroll=2)

    @pl.when(i == 0)
    def _():
        gather(dest_ref, 0)

    @pl.when(i + 1 < pl.num_programs(0))
    def _():
        gather(dnext_ref, 1 - slot)

    pltpu.make_async_copy(eo_ref.at[pl.ds(0, n), :], buf_ref.at[slot], sem.at[slot]).wait()

    gates = gate_ref[...]
    half = buf_ref.shape[2]
    cw = 2 * LANES
    row_sum = jnp.zeros((tc, 1), F32)
    for c in range(half // cw):
        tn = MOE_TN_DOWN
        u_cols = slice(c * cw, (c + 1) * cw)
        first = (c * cw // tn) * 2 * tn + c * cw % tn
        lo_cols = slice(first, first + cw)
        hi_cols = slice(first + tn, first + tn + cw)
        lo = DEEPNORM_ALPHA * x1_ref[:, lo_cols]
        hi = DEEPNORM_ALPHA * x1_ref[:, hi_cols]
        for k in range(TOP_K):
            u = buf_ref[slot, k * tc:(k + 1) * tc, u_cols]
            gk = gates[:, k:k + 1]
            lo = lo + gk * _unpack_lo(u)
            hi = hi + gk * _unpack_hi(u)
        o_ref[:, lo_cols] = lo
        o_ref[:, hi_cols] = hi
        row_sum = row_sum + jnp.sum(lo, axis=-1, keepdims=True) + jnp.sum(hi, axis=-1, keepdims=True)
    inv_d = 1.0 / (2 * half)
    mu = row_sum * inv_d
    sq_sum = jnp.zeros((tc, 1), F32)
    for c in range(2 * half // cw):
        d = o_ref[:, c * cw:(c + 1) * cw] - mu
        sq_sum = sq_sum + jnp.sum(d * d, axis=-1, keepdims=True)
    rstd = lax.rsqrt(sq_sum * inv_d + LN_EPS)
    for c in range(2 * half // cw):
        cols = slice(c * cw, (c + 1) * cw)
        o_ref[:, cols] = (o_ref[:, cols] - mu) * rstd * g_ref[:, cols] + b_ref[:, cols]


def _combine(dest, eo, gates, x1, g, b):
    T, D = x1.shape
    tc = COMBINE_TC
    n = tc * TOP_K
    nt = T // tc
    half = eo.shape[1]
    row = pl.BlockSpec((tc, D), lambda i: (i, 0))
    vec = pl.BlockSpec((1, D), lambda i: (0, 0))
    dest3 = dest.reshape(nt, 1, n)
    return pl.pallas_call(
        functools.partial(_combine_kernel, tc=tc),
        grid=(nt,),
        in_specs=[pl.BlockSpec((1, 1, n), lambda i: (i, 0, 0), memory_space=pltpu.SMEM),
                  pl.BlockSpec((1, 1, n), lambda i: (jnp.minimum(i + 1, nt - 1), 0, 0),
                               memory_space=pltpu.SMEM),
                  pl.BlockSpec(memory_space=pl.ANY),
                  pl.BlockSpec((tc, TOP_K), lambda i: (i, 0)),
                  row, vec, vec],
        out_specs=row,
        out_shape=jax.ShapeDtypeStruct((T, D), F32),
        scratch_shapes=[pltpu.VMEM((2, n, half), U32), pltpu.SemaphoreType.DMA((2,))],
        compiler_params=_cparams(("arbitrary",)),
        name="combine_rows",
    )(dest3, dest3, eo, gates, x1, g, b)


def _work_items(cnt, starts, n_items):
    chunks_per_pass = MOE_TM // MOE_PAD
    gchunks = (cnt + MOE_PAD - 1) // MOE_PAD
    npass = (gchunks + chunks_per_pass - 1) // chunks_per_pass
    cum = jnp.cumsum(npass)
    total = cum[-1]
    w = jnp.arange(n_items, dtype=I32)
    e = jnp.minimum(jnp.sum((cum[None, :] <= w[:, None]).astype(I32), axis=1), N_EXPERTS - 1)
    p = w - (cum - npass)[e]
    row = starts[e] + p * MOE_TM
    nch = jnp.clip(gchunks[e] - p * chunks_per_pass, 0, chunks_per_pass)
    valid = w < total
    last = jnp.maximum(total - 1, 0)
    e = jnp.where(valid, e, e[last]).astype(I32)
    row = jnp.where(valid, row, row[last]).astype(I32)
    nch = jnp.where(valid, nch, 0).astype(I32)
    misc = jnp.stack([jnp.sum(gchunks), total]).astype(I32)
    fillc = jnp.where(cnt % MOE_PAD != 0, starts // MOE_PAD + gchunks - 1, -1).astype(I32)
    return e, row, nch, misc, fillc


def kernel(x, mem, rel_table, w_in, w_mem_kv, w_o, lambda_q1, lambda_k1, lambda_q2, lambda_k2, subln_g, conv_w, conv_b, conv_ln_g, conv_ln_b, ln1_g, ln1_b, w_router, b_router, w_gate_up, b_gate_up, w_down, b_down, ln2_g, ln2_b):
    B, S, D = x.shape
    T = B * S
    n_heads = rel_table.shape[1]
    qk_w = n_heads * 2 * HEAD_DIM
    v_w = n_heads * V_DIM
    C = conv_w.shape[-1]
    mem_w = w_mem_kv.shape[-1] // 2
    mem_heads = 4
    mem_hd = mem_w // mem_heads
    in_cols = w_in.shape[-1]
    assert in_cols == 2 * qk_w + v_w + 2 * C + mem_w
    d_ff = w_down.shape[2]
    assert DEPTH == 1 and w_in.shape[0] == 1

    col_scale = jnp.concatenate([
        jnp.full((qk_w,), HEAD_DIM ** -0.5, F32),
        jnp.ones((in_cols - qk_w - mem_w,), F32),
        jnp.full((mem_w,), mem_hd ** -0.5, F32)]).reshape(1, in_cols)

    xt = x.reshape(T, D)
    proj = _proj_matmul(xt.astype(BF16), w_in[0], col_scale, tm=1024, tn=512, out_dtype=BF16)
    proj3 = proj.reshape(B, S, in_cols)

    bias = _bias_tiles(rel_table, ATT_T)
    lam_params = jnp.concatenate([lambda_q1, lambda_k1, lambda_q2, lambda_k2], axis=0).astype(F32)
    a_out = _diff_attention(proj3, bias, lam_params, subln_g, n_heads=n_heads,
                            k_col=qk_w // (2 * HEAD_DIM), v_col=2 * qk_w // V_DIM)

    conv_col = (2 * qk_w + v_w) // C
    c_out = _conformer_conv(proj3, conv_w[0, :, 0, :], conv_b, conv_ln_g, conv_ln_b,
                            a_col=conv_col, g_col=conv_col + 1)

    n_mem = mem.shape[1]
    kv = _proj_matmul(mem.reshape(B * n_mem, D).astype(BF16), w_mem_kv[0],
                      jnp.ones((1, 2 * mem_w), F32), tm=B * n_mem, tn=512, out_dtype=BF16)
    m_out = _memory_attention(proj3, kv.reshape(B, n_mem, 2 * mem_w), n_heads=mem_heads,
                              q_col=(in_cols - mem_w) // mem_hd, hd=mem_hd)

    mix = _out_projection(a_out.reshape(T, v_w), c_out.reshape(T, C), m_out.reshape(T, mem_w),
                          w_o[0], tm=1024, tn=512)

    x1, xp, topi, gates = _ln_router(xt, mix, ln1_g, ln1_b, w_router[0],
                                     b_router.reshape(1, -1), tm=256)

    dest, cnt8, starts8 = _rank_assignments(topi)
    cnt, starts = cnt8[0, :N_EXPERTS], starts8[0, :N_EXPERTS]

    n_assign = T * TOP_K
    n_items = n_assign // MOE_TM + N_EXPERTS
    max_rows = n_assign + N_EXPERTS * (MOE_PAD - 1) + MOE_TM
    n_rows = -(-max_rows // MOE_PAD) * MOE_PAD
    we, wr, wc, misc, fillc = _work_items(cnt, starts, n_items)

    xs = _dispatch(dest, xp, fillc, misc, n_rows)
    h = _moe_up(xs, w_gate_up[0], b_gate_up[0].reshape(N_EXPERTS, 1, -1), we, wr, wc, misc, d_ff=d_ff)
    eo = _moe_down(h, w_down[0], b_down[0].reshape(N_EXPERTS, 1, -1), we, wr, wc, misc)
    out = _combine(dest, eo, gates, x1, ln2_g, ln2_b)
    return out.reshape(B, S, D)
```

```python
import functools
import math

import jax
import jax.numpy as jnp
from jax import lax
from jax.experimental import pallas as pl
from jax.experimental.pallas import tpu as pltpu

F32 = jnp.float32
BF16 = jnp.bfloat16
I32 = jnp.int32
U32 = jnp.uint32

HEAD_DIM = 128
V_DIM = 2 * HEAD_DIM
CONV_TAPS = 31
NUM_BUCKETS = 32
MAX_EXACT = NUM_BUCKETS // 2
MAX_DISTANCE = 128
N_EXPERTS = 32
TOP_K = 4
SWIGLU_LIMIT = 7.0
SWIGLU_ALPHA = 1.702
LN_EPS = 1e-5
DEPTH = 1
DEEPNORM_ALPHA = (2 * DEPTH) ** 0.25
LAM_INIT = 0.8 - 0.6 * math.exp(-0.3 * 0)

LANES = 128
V7X_VMEM_BYTES = 64 * 1024 * 1024
VMEM_LIMIT = 58 * 1024 * 1024

ATT_T = 256
CONV_TS = 256
CONV_HALO = 32
MEM_TQ = 512
MOE_TM = 1088
MOE_MAIN_ROWS = 1024
MOE_PAD = 64
MOE_TN_UP = 256
MOE_UP_ROW_CHUNKS = 2
MOE_TN_DOWN = 512
RANK_TT = 512
DISPATCH_TA = 1024
COMBINE_TC = 128


def _cparams(sem, vmem=VMEM_LIMIT):
    return pltpu.CompilerParams(dimension_semantics=sem, vmem_limit_bytes=vmem)


def _proj_kernel(a_ref, w_ref, cs_ref, o_ref):
    acc = jnp.dot(a_ref[...], w_ref[...].astype(BF16), preferred_element_type=F32)
    o_ref[...] = (acc * cs_ref[...]).astype(o_ref.dtype)


def _proj_matmul(a, w, col_scale, *, tm, tn, out_dtype):
    M, K = a.shape
    N = w.shape[1]
    return pl.pallas_call(
        _proj_kernel,
        grid=(M // tm, N // tn),
        in_specs=[pl.BlockSpec((tm, K), lambda i, j: (i, 0)),
                  pl.BlockSpec((K, tn), lambda i, j: (0, j)),
                  pl.BlockSpec((1, tn), lambda i, j: (0, j))],
        out_specs=pl.BlockSpec((tm, tn), lambda i, j: (i, j)),
        out_shape=jax.ShapeDtypeStruct((M, N), out_dtype),
        compiler_params=_cparams(("parallel", "arbitrary")),
        name="proj_matmul",
    )(a, w, col_scale)


def _bias_kernel(tbl_ref, o_ref, *, t, n_heads):
    h = pl.program_id(0)
    j = lax.broadcasted_iota(I32, (t, t), 0)
    i = lax.broadcasted_iota(I32, (t, t), 1)
    for o in range(2):
        dist = o * t + i - j
        n = jnp.maximum(dist, 0)
        nf = jnp.maximum(n, 1).astype(F32)
        large = MAX_EXACT + (jnp.log(nf / MAX_EXACT) / math.log(MAX_DISTANCE / MAX_EXACT)
                             * (NUM_BUCKETS - MAX_EXACT)).astype(I32)
        large = jnp.minimum(large, NUM_BUCKETS - 1)
        bucket = jnp.where(n < MAX_EXACT, n, large)
        val = jnp.zeros((t, t), F32)
        for b in range(NUM_BUCKETS):
            val = jnp.where(bucket == b, tbl_ref[b * n_heads + h], val)
        if o == 0:
            val = jnp.where(dist >= 0, val, -jnp.inf)
        o_ref[0, o] = val
    o_ref[0, 2] = jnp.full((t, t), tbl_ref[(NUM_BUCKETS - 1) * n_heads + h], F32)


def _bias_tiles(rel_table, t):
    n_heads = rel_table.shape[1]
    assert t + 1 >= MAX_DISTANCE
    return pl.pallas_call(
        functools.partial(_bias_kernel, t=t, n_heads=n_heads),
        grid=(n_heads,),
        in_specs=[pl.BlockSpec(memory_space=pltpu.SMEM)],
        out_specs=pl.BlockSpec((1, 3, t, t), lambda h: (h, 0, 0, 0)),
        out_shape=jax.ShapeDtypeStruct((n_heads, 3, t, t), F32),
        compiler_params=_cparams(("arbitrary",)),
        name="bias_tiles",
    )(rel_table.reshape(-1))


def _attn_kernel(lam_ref, q_ref, k_ref, v_ref, bias_ref, g_ref, o_ref, vt_ref, acc_ref, m_ref, l_ref,
                 *, t):
    qi = pl.program_id(2)
    n_kt = v_ref.shape[1] // t

    @pl.when(qi == 0)
    def _():
        for kb in range(n_kt):
            vt_ref[:, kb * t:(kb + 1) * t] = v_ref[0, kb * t:(kb + 1) * t, :].astype(F32).T.astype(BF16)

    q = q_ref[0]
    qs = (q[:, :HEAD_DIM], q[:, HEAD_DIM:])
    dn = (((1,), (1,)), ((), ()))
    acc_ref[...] = jnp.zeros_like(acc_ref)
    m_ref[...] = jnp.full(m_ref.shape, -jnp.inf, F32)
    l_ref[...] = jnp.zeros_like(l_ref)

    def update(kb0, n_tiles):
        start = pl.multiple_of(kb0 * t, t)
        k = k_ref[0, pl.ds(start, n_tiles * t), :]
        vt = vt_ref[:, pl.ds(start, n_tiles * t)]
        bias = [bias_ref[0, jnp.minimum(qi - kb0 - i, 2)] for i in range(n_tiles)]
        for c in range(2):
            s = [lax.dot_general(k[i * t:(i + 1) * t, c * HEAD_DIM:(c + 1) * HEAD_DIM], qs[c], dn,
                                 preferred_element_type=F32) + bias[i]
                 for i in range(n_tiles)]
            m_old = m_ref[c]
            m_new = m_old
            for si in s:
                m_new = jnp.maximum(m_new, jnp.max(si, axis=0, keepdims=True))
            alpha = jnp.exp(m_old - m_new)
            p = [jnp.exp(si - m_new) for si in s]
            l_new = alpha * l_ref[c]
            for pi in p:
                l_new = l_new + jnp.sum(pi, axis=0, keepdims=True)
            pcat = p[0].astype(BF16) if n_tiles == 1 else jnp.concatenate(
                [pi.astype(BF16) for pi in p], axis=0)
            acc_ref[c] = alpha * acc_ref[c] + jnp.dot(vt, pcat, preferred_element_type=F32)
            m_ref[c] = m_new
            l_ref[c] = l_new

    n = qi + 1

    def pair(i, carry):
        update(2 * i, 2)
        return carry

    lax.fori_loop(0, n // 2, pair, 0)

    @pl.when(n % 2 == 1)
    def _():
        update(qi, 1)

    lp = lam_ref[...]
    lam = (jnp.exp(jnp.sum(lp[0:1] * lp[1:2], axis=-1, keepdims=True))
           - jnp.exp(jnp.sum(lp[2:3] * lp[3:4], axis=-1, keepdims=True)) + LAM_INIT)
    ot = acc_ref[0] * (1.0 / l_ref[0]) - acc_ref[1] * (lam / l_ref[1])
    ot = ot * lax.rsqrt(jnp.mean(jnp.square(ot), axis=0, keepdims=True) + LN_EPS)
    o = ot.T * g_ref[...]
    o_ref[0] = (o * (1.0 - LAM_INIT)).astype(o_ref.dtype)


def _diff_attention(proj3, bias, lam_params, subln_g, *, n_heads, k_col, v_col):
    B, S, _ = proj3.shape
    t = ATT_T
    nq = S // t
    w = 2 * HEAD_DIM
    return pl.pallas_call(
        functools.partial(_attn_kernel, t=t),
        grid=(B, n_heads, nq),
        in_specs=[pl.BlockSpec((4, HEAD_DIM), lambda b, h, i: (0, 0)),
                  pl.BlockSpec((1, t, w), lambda b, h, i: (b, i, h)),
                  pl.BlockSpec((1, S, w), lambda b, h, i: (b, 0, k_col + h)),
                  pl.BlockSpec((1, S, w), lambda b, h, i: (b, 0, v_col + h)),
                  pl.BlockSpec((1, 3, t, t), lambda b, h, i: (h, 0, 0, 0)),
                  pl.BlockSpec((1, V_DIM), lambda b, h, i: (0, 0))],
        out_specs=pl.BlockSpec((1, t, V_DIM), lambda b, h, i: (b, i, h)),
        out_shape=jax.ShapeDtypeStruct((B, S, n_heads * V_DIM), BF16),
        scratch_shapes=[pltpu.VMEM((V_DIM, S), BF16), pltpu.VMEM((2, V_DIM, t), F32),
                        pltpu.VMEM((2, 1, t), F32), pltpu.VMEM((2, 1, t), F32)],
        compiler_params=_cparams(("parallel", "parallel", "arbitrary")),
        name="diff_attention",
    )(lam_params, proj3, proj3, proj3, bias, subln_g)


def _conv_kernel(a_ref, g_ref, ap_ref, gp_ref, w_ref, cb_ref, lg_ref, lb_ref, o_ref,
                 hs_ref, cv_ref, *, ts, halo, taps, rc):
    i = pl.program_id(1)
    C = a_ref.shape[-1]
    hs_ref[halo:, :] = a_ref[0].astype(F32) * jax.nn.sigmoid(g_ref[0].astype(F32))
    hp = ap_ref[0].astype(F32) * jax.nn.sigmoid(gp_ref[0].astype(F32))
    hs_ref[:halo, :] = jnp.where(i > 0, hp, 0.0)
    base = halo - (taps - 1)
    for c in range(C // LANES):
        cs = slice(c * LANES, (c + 1) * LANES)
        for r in range(ts // rc):
            acc = jnp.zeros((rc, LANES), F32)
            for tp in range(taps):
                acc = acc + w_ref[tp:tp + 1, cs] * hs_ref[pl.ds(base + tp + r * rc, rc), cs]
            cv_ref[r * rc:(r + 1) * rc, cs] = acc + cb_ref[:, cs]
    h = cv_ref[...]
    mu = jnp.mean(h, axis=-1, keepdims=True)
    var = jnp.mean(jnp.square(h - mu), axis=-1, keepdims=True)
    y = (h - mu) * lax.rsqrt(var + LN_EPS) * lg_ref[...] + lb_ref[...]
    o_ref[0] = (y * jax.nn.sigmoid(y)).astype(o_ref.dtype)


def _conformer_conv(proj3, conv_w, conv_b, ln_g, ln_b, *, a_col, g_col):
    B, S, _ = proj3.shape
    taps, C = conv_w.shape
    ts, halo = CONV_TS, CONV_HALO
    assert halo >= taps - 1
    hb = ts // halo
    prev = lambda col: (lambda b, i: (b, jnp.maximum(i * hb - 1, 0), col))
    vec = pl.BlockSpec((1, C), lambda b, i: (0, 0))
    return pl.pallas_call(
        functools.partial(_conv_kernel, ts=ts, halo=halo, taps=taps, rc=128),
        grid=(B, S // ts),
        in_specs=[pl.BlockSpec((1, ts, C), lambda b, i: (b, i, a_col)),
                  pl.BlockSpec((1, ts, C), lambda b, i: (b, i, g_col)),
                  pl.BlockSpec((1, halo, C), prev(a_col)),
                  pl.BlockSpec((1, halo, C), prev(g_col)),
                  pl.BlockSpec((taps, C), lambda b, i: (0, 0)),
                  vec, vec, vec],
        out_specs=pl.BlockSpec((1, ts, C), lambda b, i: (b, i, 0)),
        out_shape=jax.ShapeDtypeStruct((B, S, C), BF16),
        scratch_shapes=[pltpu.VMEM((halo + ts, C), F32), pltpu.VMEM((ts, C), F32)],
        compiler_params=_cparams(("parallel", "arbitrary")),
        name="conformer_conv",
    )(proj3, proj3, proj3, proj3, conv_w, conv_b, ln_g, ln_b)


def _mem_attn_kernel(q_ref, k_ref, v_ref, o_ref):
    s = lax.dot_general(q_ref[0], k_ref[0], (((1,), (1,)), ((), ())), preferred_element_type=F32)
    m = jnp.max(s, axis=-1, keepdims=True)
    p = jnp.exp(s - m)
    p = p / jnp.sum(p, axis=-1, keepdims=True)
    o_ref[0] = jnp.dot(p.astype(BF16), v_ref[0], preferred_element_type=F32).astype(o_ref.dtype)


def _memory_attention(proj3, kv3, *, n_heads, q_col, hd):
    B, S, _ = proj3.shape
    M = kv3.shape[1]
    tq = MEM_TQ
    return pl.pallas_call(
        _mem_attn_kernel,
        grid=(B, n_heads, S // tq),
        in_specs=[pl.BlockSpec((1, tq, hd), lambda b, h, i: (b, i, q_col + h)),
                  pl.BlockSpec((1, M, hd), lambda b, h, i: (b, 0, h)),
                  pl.BlockSpec((1, M, hd), lambda b, h, i: (b, 0, n_heads + h))],
        out_specs=pl.BlockSpec((1, tq, hd), lambda b, h, i: (b, i, h)),
        out_shape=jax.ShapeDtypeStruct((B, S, n_heads * hd), BF16),
        compiler_params=_cparams(("parallel", "parallel", "arbitrary")),
        name="memory_attention",
    )(proj3, kv3, kv3)


def _oproj_kernel(a_ref, c_ref, m_ref, wa_ref, wc_ref, wm_ref, o_ref):
    acc = jnp.dot(a_ref[...], wa_ref[...].astype(BF16), preferred_element_type=F32)
    acc += jnp.dot(c_ref[...], wc_ref[...].astype(BF16), preferred_element_type=F32)
    acc += jnp.dot(m_ref[...], wm_ref[...].astype(BF16), preferred_element_type=F32)
    o_ref[...] = acc


def _out_projection(a, c, m, w_o, *, tm, tn):
    M, Ka = a.shape
    Kc, Km = c.shape[1], m.shape[1]
    assert Kc == Km and Ka % Kc == 0
    N = w_o.shape[1]
    return pl.pallas_call(
        _oproj_kernel,
        grid=(M // tm, N // tn),
        in_specs=[pl.BlockSpec((tm, Ka), lambda i, j: (i, 0)),
                  pl.BlockSpec((tm, Kc), lambda i, j: (i, 0)),
                  pl.BlockSpec((tm, Km), lambda i, j: (i, 0)),
                  pl.BlockSpec((Ka, tn), lambda i, j: (0, j)),
                  pl.BlockSpec((Kc, tn), lambda i, j: (Ka // Kc, j)),
                  pl.BlockSpec((Km, tn), lambda i, j: (Ka // Kc + 1, j))],
        out_specs=pl.BlockSpec((tm, tn), lambda i, j: (i, j)),
        out_shape=jax.ShapeDtypeStruct((M, N), F32),
        compiler_params=_cparams(("parallel", "arbitrary")),
        name="out_projection",
    )(a, c, m, w_o, w_o, w_o)


def _pack_pairs(lo, hi):
    lo_b = lax.bitcast_convert_type(lo.astype(BF16).astype(F32), U32)
    hi_b = lax.bitcast_convert_type(hi.astype(BF16).astype(F32), U32)
    return (lo_b >> 16) | (hi_b & jnp.uint32(0xFFFF0000))


def _unpack_lo(u):
    return lax.bitcast_convert_type(u << 16, F32)


def _unpack_hi(u):
    return lax.bitcast_convert_type(u & jnp.uint32(0xFFFF0000), F32)


def _layer_norm(y, g, b):
    mu = jnp.mean(y, axis=-1, keepdims=True)
    var = jnp.mean(jnp.square(y - mu), axis=-1, keepdims=True)
    return (y - mu) * lax.rsqrt(var + LN_EPS) * g + b


def _ln_router_kernel(x_ref, mix_ref, g_ref, b_ref, wr_ref, br_ref,
                      x1_ref, xp_ref, topi_ref, gate_ref):
    x1 = _layer_norm(DEEPNORM_ALPHA * x_ref[...] + mix_ref[...], g_ref[...], b_ref[...])
    x1_ref[...] = x1
    half = x1.shape[1] // 2
    xp_ref[...] = _pack_pairs(x1[:, :half], x1[:, half:])
    logits = jnp.dot(x1, wr_ref[...], preferred_element_type=F32,
                     precision=lax.Precision.HIGHEST) + br_ref[...]
    tm, ne = logits.shape
    lane = lax.broadcasted_iota(I32, (tm, ne), 1)
    kk = lax.broadcasted_iota(I32, (tm, TOP_K), 1)
    idx = jnp.zeros((tm, TOP_K), I32)
    val = jnp.zeros((tm, TOP_K), F32)
    cur = logits
    for k in range(TOP_K):
        mx = jnp.max(cur, axis=-1, keepdims=True)
        ix = jnp.min(jnp.where(cur == mx, lane, ne), axis=-1, keepdims=True)
        idx = jnp.where(kk == k, ix, idx)
        val = jnp.where(kk == k, mx, val)
        cur = jnp.where(lane == ix, -jnp.inf, cur)
    e = jnp.exp(val - jnp.max(val, axis=-1, keepdims=True))
    topi_ref[...] = idx
    gate_ref[...] = e / jnp.sum(e, axis=-1, keepdims=True)


def _ln_router(x, mix, g, b, w_router, b_router, *, tm):
    T, D = x.shape
    ne = w_router.shape[1]
    row = pl.BlockSpec((tm, D), lambda i: (i, 0))
    vec = pl.BlockSpec((1, D), lambda i: (0, 0))
    return pl.pallas_call(
        _ln_router_kernel,
        grid=(T // tm,),
        in_specs=[row, row, vec, vec,
                  pl.BlockSpec((D, ne), lambda i: (0, 0)),
                  pl.BlockSpec((1, ne), lambda i: (0, 0))],
        out_specs=[row,
                   pl.BlockSpec((tm, D // 2), lambda i: (i, 0)),
                   pl.BlockSpec((tm, TOP_K), lambda i: (i, 0)),
                   pl.BlockSpec((tm, TOP_K), lambda i: (i, 0))],
        out_shape=[jax.ShapeDtypeStruct((T, D), F32),
                   jax.ShapeDtypeStruct((T, D // 2), U32),
                   jax.ShapeDtypeStruct((T, TOP_K), I32),
                   jax.ShapeDtypeStruct((T, TOP_K), F32)],
        compiler_params=_cparams(("parallel",)),
        name="ln_router",
    )(x, mix, g, b, w_router, b_router)


def _rank_kernel(topi_ref, dest_ref, cnt_ref, ps_ref, rank_ref, carry_ref, *, tt, pad):
    ph = pl.program_id(0)
    i = pl.program_id(1)
    lane = lax.broadcasted_iota(I32, (tt, LANES), 1)
    kk = lax.broadcasted_iota(I32, (tt, TOP_K), 1)
    ti = topi_ref[...]
    sel = [lane == ti[:, k:k + 1] for k in range(TOP_K)]
    rows = pl.ds(pl.multiple_of(i * tt, tt), tt)

    @pl.when(ph == 0)
    def _():
        @pl.when(i == 0)
        def _():
            carry_ref[...] = jnp.zeros_like(carry_ref)

        oh = jnp.zeros((tt, LANES), F32)
        for k in range(TOP_K):
            oh = oh + sel[k].astype(F32)
        r = lax.broadcasted_iota(I32, (tt, tt), 0)
        c = lax.broadcasted_iota(I32, (tt, tt), 1)
        lower = (r > c).astype(BF16)
        before = jnp.dot(lower, oh.astype(BF16), preferred_element_type=F32) + carry_ref[0:1, :]
        rank = jnp.zeros((tt, TOP_K), F32)
        for k in range(TOP_K):
            rk = jnp.sum(jnp.where(sel[k], before, 0.0), axis=-1, keepdims=True)
            rank = jnp.where(kk == k, rk, rank)
        rank_ref[rows, :] = rank
        carry_ref[...] = carry_ref[...] + jnp.sum(oh, axis=0, keepdims=True)

    @pl.when(ph == 1)
    def _():
        cnt = carry_ref[...].astype(I32)
        padded = (cnt + (pad - 1)) & jnp.int32(-pad)
        l8 = lax.broadcasted_iota(I32, (8, LANES), 1)
        scan = padded
        for sh in (1, 2, 4, 8, 16, 32, 64):
            scan = scan + jnp.where(l8 >= sh, pltpu.roll(scan, sh, axis=1), 0)
        starts = scan - padded
        cnt_ref[...] = cnt
        ps_ref[...] = starts
        st = starts[0:1, :].astype(F32)
        rank = rank_ref[rows, :]
        dest = jnp.zeros((tt, TOP_K), F32)
        for k in range(TOP_K):
            base = jnp.sum(jnp.where(sel[k], st, 0.0), axis=-1, keepdims=True)
            dest = jnp.where(kk == k, base, dest)
        dest_ref[...] = (dest + rank).astype(I32)


def _rank_assignments(topi):
    T = topi.shape[0]
    tt = RANK_TT
    return pl.pallas_call(
        functools.partial(_rank_kernel, tt=tt, pad=MOE_PAD),
        grid=(2, T // tt),
        in_specs=[pl.BlockSpec((tt, TOP_K), lambda p, i: (i, 0))],
        out_specs=[pl.BlockSpec((tt, TOP_K), lambda p, i: (i * p, 0)),
                   pl.BlockSpec((8, LANES), lambda p, i: (0, 0)),
                   pl.BlockSpec((8, LANES), lambda p, i: (0, 0))],
        out_shape=[jax.ShapeDtypeStruct((T, TOP_K), I32),
                   jax.ShapeDtypeStruct((8, LANES), I32),
                   jax.ShapeDtypeStruct((8, LANES), I32)],
        scratch_shapes=[pltpu.VMEM((T, TOP_K), F32), pltpu.VMEM((8, LANES), F32)],
        compiler_params=_cparams(("arbitrary", "arbitrary")),
        name="rank_assignments",
    )(topi)


def _dispatch_kernel(fillc_ref, misc_ref, dest_ref, xp_ref, xs_ref, zero_ref, fill_sem, row_sem, *, ta):
    i = pl.program_id(0)
    tt = xp_ref.shape[0]

    @pl.when(i == 0)
    def _():
        def fill(e):
            row = pl.multiple_of(fillc_ref[e] * MOE_PAD, MOE_PAD)
            return pltpu.make_async_copy(zero_ref, xs_ref.at[pl.ds(row, MOE_PAD), :], fill_sem)

        def each_group(fn):
            for e in range(N_EXPERTS):
                pl.when(fillc_ref[e] >= 0)(functools.partial(fn, e))

        zero_ref[...] = jnp.zeros_like(zero_ref)
        each_group(lambda e: fill(e).start())
        _fill_tail(misc_ref, zero_ref, xs_ref, fill_sem)
        each_group(lambda e: fill(e).wait())

    def issue(t, carry):
        for k in range(TOP_K):
            pltpu.make_async_copy(xp_ref.at[pl.ds(t, 1), :],
                                  xs_ref.at[pl.ds(dest_ref[0, 0, t * TOP_K + k], 1), :],
                                  row_sem).start()
        return carry

    lax.fori_loop(0, tt, issue, 0, unroll=2)
    for _ in range(ta // tt):
        pltpu.make_async_copy(xp_ref, xs_ref.at[pl.ds(0, tt), :], row_sem).wait()


def _dispatch(dest, xp, fillc, misc, n_rows):
    T, W = xp.shape
    ta = DISPATCH_TA
    na = dest.size // ta
    assert n_rows % MOE_PAD == 0 and ta % TOP_K == 0
    return pl.pallas_call(
        functools.partial(_dispatch_kernel, ta=ta),
        grid_spec=pltpu.PrefetchScalarGridSpec(
            num_scalar_prefetch=2,
            grid=(na,),
            in_specs=[pl.BlockSpec((1, 1, ta), lambda i, fc, ms: (i, 0, 0), memory_space=pltpu.SMEM),
                      pl.BlockSpec((ta // TOP_K, W), lambda i, fc, ms: (i, 0))],
            out_specs=pl.BlockSpec(memory_space=pl.ANY),
            scratch_shapes=[pltpu.VMEM((MOE_PAD, W), U32),
                            pltpu.SemaphoreType.DMA(()), pltpu.SemaphoreType.DMA(())]),
        out_shape=jax.ShapeDtypeStruct((n_rows, W), U32),
        compiler_params=_cparams(("arbitrary",)),
        name="dispatch_rows",
    )(fillc, misc, dest.reshape(na, 1, ta), xp)


class _TileWriter:
    def __init__(self, wr_ref, wc_ref, stage_ref, out_ref, sem, tn):
        self.wr_ref, self.wc_ref = wr_ref, wc_ref
        self.stage_ref, self.out_ref, self.sem, self.tn = stage_ref, out_ref, sem, tn
        n_chunks = stage_ref.shape[1] // MOE_PAD
        self.sizes = [1 << b for b in reversed(range(n_chunks.bit_length()))]

    def _copy(self, slot, w, j, size):
        n = self.wc_ref[w]
        first = pl.multiple_of((n & -(2 * size)) * MOE_PAD, MOE_PAD)
        row = pl.multiple_of(self.wr_ref[w] + first, MOE_PAD)
        col = pl.multiple_of(j * self.tn, self.tn)
        return pltpu.make_async_copy(
            self.stage_ref.at[slot, pl.ds(first, size * MOE_PAD), :],
            self.out_ref.at[pl.ds(row, size * MOE_PAD), pl.ds(col, self.tn)],
            self.sem.at[slot])

    def _each_group(self, w, fn):
        for size in self.sizes:
            pl.when((self.wc_ref[w] & size) != 0)(functools.partial(fn, size))

    def start(self, slot, w, j):
        self._each_group(w, lambda size: self._copy(slot, w, j, size).start())

    def wait(self, slot, w, j):
        self._each_group(w, lambda size: self._copy(slot, w, j, size).wait())

    def retire_older(self, w, j, nj):
        step = w * nj + j

        @pl.when(step >= 2)
        def _():
            wrap = j < 2
            self.wait(step % 2, jnp.where(wrap, w - 1, w), jnp.where(wrap, j + nj - 2, j - 2))

    def drain(self, w, j, nw, nj):
        step = w * nj + j

        @pl.when(step == nw * nj - 1)
        def _():
            self.wait((step + 1) % 2, w, j - 1)
            self.wait(step % 2, w, j)


def _fill_tail(misc_ref, zero_ref, out_ref, sem):
    n_total = out_ref.shape[0] // MOE_PAD

    def fill(c):
        row = pl.multiple_of(c * MOE_PAD, MOE_PAD)
        return pltpu.make_async_copy(zero_ref, out_ref.at[pl.ds(row, MOE_PAD), :], sem)

    def start(c, carry):
        fill(c).start()
        return carry

    def wait(c, carry):
        fill(c).wait()
        return carry

    lax.fori_loop(misc_ref[0], n_total, start, 0)
    lax.fori_loop(misc_ref[0], n_total, wait, 0)


def _moe_up_kernel(we_ref, wr_ref, wc_ref, misc_ref, xs_ref, wg_ref, wu_ref, b_ref,
                   h_ref, xb_ref, wcat_ref, stage_ref, zero_ref, sem, fill_sem):
    w = pl.program_id(0)
    j = pl.program_id(1)
    nw = pl.num_programs(0)
    nj = pl.num_programs(1)
    slot = (w * nj + j) % 2
    writer = _TileWriter(wr_ref, wc_ref, stage_ref, h_ref, sem, stage_ref.shape[2])

    @pl.when((w == 0) & (j == 0))
    def _():
        zero_ref[...] = jnp.zeros_like(zero_ref)
        _fill_tail(misc_ref, zero_ref, h_ref, fill_sem)

    writer.retire_older(w, j, nj)

    @pl.when(wc_ref[w] > 0)
    def _():
        half = xs_ref.shape[1]

        @pl.when(j == 0)
        def _():
            u = xs_ref[...]
            xb_ref[:, :half] = _unpack_lo(u).astype(BF16)
            xb_ref[:, half:] = _unpack_hi(u).astype(BF16)

        tn = wg_ref.shape[2]
        wcat_ref[:, :tn] = wg_ref[0].astype(BF16)
        wcat_ref[:, tn:] = wu_ref[0].astype(BF16)
        col = pl.multiple_of(j * tn, tn)
        bg = b_ref[0, :, pl.ds(col, tn)]
        bu = b_ref[0, :, pl.ds(pl.multiple_of(b_ref.shape[2] // 2 + col, tn), tn)]
        def rows_block(rows):
            r = jnp.dot(xb_ref[rows, :], wcat_ref[...], preferred_element_type=F32)
            gate = r[:, :tn] + bg
            up = r[:, tn:] + bu
            gate = jnp.minimum(gate, SWIGLU_LIMIT)
            up = jnp.clip(up, -SWIGLU_LIMIT, SWIGLU_LIMIT)
            act = (up + 1.0) * (gate * jax.nn.sigmoid(SWIGLU_ALPHA * gate))
            stage_ref[slot, rows, :] = act.astype(stage_ref.dtype)

        tm, main = xb_ref.shape[0], min(MOE_MAIN_ROWS, xb_ref.shape[0])
        mc = main // MOE_UP_ROW_CHUNKS
        for mi in range(MOE_UP_ROW_CHUNKS):
            rows_block(slice(mi * mc, (mi + 1) * mc))
        if tm > main:
            pl.when(wc_ref[w] * MOE_PAD > main)(lambda: rows_block(slice(main, tm)))
        writer.start(slot, w, j)

    writer.drain(w, j, nw, nj)


def _moe_up(xs, w_gu, b_gu, we, wr, wc, misc, *, d_ff):
    R, half = xs.shape
    E, D, _ = w_gu.shape
    tm, tn = MOE_TM, MOE_TN_UP
    nj = d_ff // tn
    nw = we.shape[0]
    assert nj >= 2 and R % MOE_PAD == 0 and tm % (16 * MOE_UP_ROW_CHUNKS) == 0
    jeff = lambda w, j, wc: jnp.where(wc[w] > 0, j, nj - 1)
    return pl.pallas_call(
        _moe_up_kernel,
        grid_spec=pltpu.PrefetchScalarGridSpec(
            num_scalar_prefetch=4,
            grid=(misc[1], nj),
            in_specs=[pl.BlockSpec((pl.Element(tm), pl.Element(half)),
                                   lambda w, j, we, wr, wc, ms: (pl.multiple_of(wr[w], MOE_PAD), 0)),
                      pl.BlockSpec((1, D, tn), lambda w, j, we, wr, wc, ms: (we[w], 0, jeff(w, j, wc))),
                      pl.BlockSpec((1, D, tn), lambda w, j, we, wr, wc, ms: (we[w], 0, nj + jeff(w, j, wc))),
                      pl.BlockSpec((1, 1, 2 * nj * tn), lambda w, j, we, wr, wc, ms: (we[w], 0, 0))],
            out_specs=pl.BlockSpec(memory_space=pl.ANY),
            scratch_shapes=[pltpu.VMEM((tm, 2 * half), BF16),
                            pltpu.VMEM((D, 2 * tn), BF16),
                            pltpu.VMEM((2, tm, tn), BF16),
                            pltpu.VMEM((MOE_PAD, d_ff), BF16),
                            pltpu.SemaphoreType.DMA((2,)), pltpu.SemaphoreType.DMA(())]),
        out_shape=jax.ShapeDtypeStruct((R, d_ff), BF16),
        compiler_params=_cparams(("arbitrary", "arbitrary")),
        name="moe_up",
    )(we, wr, wc, misc, xs, w_gu, w_gu, b_gu)


def _moe_down_kernel(we_ref, wr_ref, wc_ref, misc_ref, h_ref, w_ref, b_ref,
                     o_ref, stage_ref, zero_ref, sem, fill_sem):
    w = pl.program_id(0)
    j = pl.program_id(1)
    nw = pl.num_programs(0)
    nj = pl.num_programs(1)
    slot = (w * nj + j) % 2
    writer = _TileWriter(wr_ref, wc_ref, stage_ref, o_ref, sem, stage_ref.shape[2])

    @pl.when((w == 0) & (j == 0))
    def _():
        zero_ref[...] = jnp.zeros_like(zero_ref)
        _fill_tail(misc_ref, zero_ref, o_ref, fill_sem)

    writer.retire_older(w, j, nj)

    @pl.when(wc_ref[w] > 0)
    def _():
        tn = stage_ref.shape[2]
        bias = b_ref[0, :, pl.ds(pl.multiple_of(j * 2 * tn, 2 * tn), 2 * tn)]

        def rows_block(rows):
            r = jnp.dot(h_ref[rows, :], w_ref[0].astype(BF16), preferred_element_type=F32) + bias
            stage_ref[slot, rows, :] = _pack_pairs(r[:, :tn], r[:, tn:])

        tm, main = h_ref.shape[0], min(MOE_MAIN_ROWS, h_ref.shape[0])
        rows_block(slice(0, main))
        if tm > main:
            pl.when(wc_ref[w] * MOE_PAD > main)(lambda: rows_block(slice(main, tm)))
        writer.start(slot, w, j)

    writer.drain(w, j, nw, nj)


def _moe_down(h, w_down, b_down, we, wr, wc, misc):
    R, d_ff = h.shape
    E, _, D = w_down.shape
    tm, tn = MOE_TM, MOE_TN_DOWN
    half = D // 2
    nj = half // tn
    nw = we.shape[0]
    assert nj >= 2 and R % MOE_PAD == 0
    jeff = lambda w, j, wc: jnp.where(wc[w] > 0, j, nj - 1)
    return pl.pallas_call(
        _moe_down_kernel,
        grid_spec=pltpu.PrefetchScalarGridSpec(
            num_scalar_prefetch=4,
            grid=(misc[1], nj),
            in_specs=[pl.BlockSpec((pl.Element(tm), pl.Element(d_ff)),
                                   lambda w, j, we, wr, wc, ms: (pl.multiple_of(wr[w], MOE_PAD), 0)),
                      pl.BlockSpec((1, d_ff, 2 * tn), lambda w, j, we, wr, wc, ms: (we[w], 0, jeff(w, j, wc))),
                      pl.BlockSpec((1, 1, 2 * nj * tn), lambda w, j, we, wr, wc, ms: (we[w], 0, 0))],
            out_specs=pl.BlockSpec(memory_space=pl.ANY),
            scratch_shapes=[pltpu.VMEM((2, tm, tn), U32),
                            pltpu.VMEM((MOE_PAD, half), U32),
                            pltpu.SemaphoreType.DMA((2,)), pltpu.SemaphoreType.DMA(())]),
        out_shape=jax.ShapeDtypeStruct((R, half), U32),
        compiler_params=_cparams(("arbitrary", "arbitrary")),
        name="moe_down",
    )(we, wr, wc, misc, h, w_down, b_down)


def _combine_kernel(dest_ref, dnext_ref, eo_ref, gate_ref, x1_ref, g_ref, b_ref, o_ref, buf_ref, sem,
                    *, tc):
    i = pl.program_id(0)
    n = tc * TOP_K
    slot = i % 2

    def gather(d_ref, slot):
        def issue(t, carry):
            for k in range(TOP_K):
                pltpu.make_async_copy(eo_ref.at[pl.ds(d_ref[0, 0, t * TOP_K + k], 1), :],
                                      buf_ref.at[slot, pl.ds(k * tc + t, 1), :],
                                      sem.at[slot]).start()
            return carry

        lax.fori_loop(0, tc, issue, 0, unroll=2)

    @pl.when(i == 0)
    def _():
        gather(dest_ref, 0)

    @pl.when(i + 1 < pl.num_programs(0))
    def _():
        gather(dnext_ref, 1 - slot)

    pltpu.make_async_copy(eo_ref.at[pl.ds(0, n), :], buf_ref.at[slot], sem.at[slot]).wait()

    gates = gate_ref[...]
    half = buf_ref.shape[2]
    cw = 2 * LANES
    row_sum = jnp.zeros((tc, 1), F32)
    for c in range(half // cw):
        tn = MOE_TN_DOWN
        u_cols = slice(c * cw, (c + 1) * cw)
        first = (c * cw // tn) * 2 * tn + c * cw % tn
        lo_cols = slice(first, first + cw)
        hi_cols = slice(first + tn, first + tn + cw)
        lo = DEEPNORM_ALPHA * x1_ref[:, lo_cols]
        hi = DEEPNORM_ALPHA * x1_ref[:, hi_cols]
        for k in range(TOP_K):
            u = buf_ref[slot, k * tc:(k + 1) * tc, u_cols]
            gk = gates[:, k:k + 1]
            lo = lo + gk * _unpack_lo(u)
            hi = hi + gk * _unpack_hi(u)
        o_ref[:, lo_cols] = lo
        o_ref[:, hi_cols] = hi
        row_sum = row_sum + jnp.sum(lo, axis=-1, keepdims=True) + jnp.sum(hi, axis=-1, keepdims=True)
    inv_d = 1.0 / (2 * half)
    mu = row_sum * inv_d
    sq_sum = jnp.zeros((tc, 1), F32)
    for c in range(2 * half // cw):
        d = o_ref[:, c * cw:(c + 1) * cw] - mu
        sq_sum = sq_sum + jnp.sum(d * d, axis=-1, keepdims=True)
    rstd = lax.rsqrt(sq_sum * inv_d + LN_EPS)
    for c in range(2 * half // cw):
        cols = slice(c * cw, (c + 1) * cw)
        o_ref[:, cols] = (o_ref[:, cols] - mu) * rstd * g_ref[:, cols] + b_ref[:, cols]


def _combine(dest, eo, gates, x1, g, b):
    T, D = x1.shape
    tc = COMBINE_TC
    n = tc * TOP_K
    nt = T // tc
    half = eo.shape[1]
    row = pl.BlockSpec((tc, D), lambda i: (i, 0))
    vec = pl.BlockSpec((1, D), lambda i: (0, 0))
    dest3 = dest.reshape(nt, 1, n)
    return pl.pallas_call(
        functools.partial(_combine_kernel, tc=tc),
        grid=(nt,),
        in_specs=[pl.BlockSpec((1, 1, n), lambda i: (i, 0, 0), memory_space=pltpu.SMEM),
                  pl.BlockSpec((1, 1, n), lambda i: (jnp.minimum(i + 1, nt - 1), 0, 0),
                               memory_space=pltpu.SMEM),
                  pl.BlockSpec(memory_space=pl.ANY),
                  pl.BlockSpec((tc, TOP_K), lambda i: (i, 0)),
                  row, vec, vec],
        out_specs=row,
        out_shape=jax.ShapeDtypeStruct((T, D), F32),
        scratch_shapes=[pltpu.VMEM((2, n, half), U32), pltpu.SemaphoreType.DMA((2,))],
        compiler_params=_cparams(("arbitrary",)),
        name="combine_rows",
    )(dest3, dest3, eo, gates, x1, g, b)


def _work_items(cnt, starts, n_items):
    chunks_per_pass = MOE_TM // MOE_PAD
    gchunks = (cnt + MOE_PAD - 1) // MOE_PAD
    npass = (gchunks + chunks_per_pass - 1) // chunks_per_pass
    cum = jnp.cumsum(npass)
    total = cum[-1]
    w = jnp.arange(n_items, dtype=I32)
    e = jnp.minimum(jnp.sum((cum[None, :] <= w[:, None]).astype(I32), axis=1), N_EXPERTS - 1)
    p = w - (cum - npass)[e]
    row = starts[e] + p * MOE_TM
    nch = jnp.clip(gchunks[e] - p * chunks_per_pass, 0, chunks_per_pass)
    valid = w < total
    last = jnp.maximum(total - 1, 0)
    e = jnp.where(valid, e, e[last]).astype(I32)
    row = jnp.where(valid, row, row[last]).astype(I32)
    nch = jnp.where(valid, nch, 0).astype(I32)
    misc = jnp.stack([jnp.sum(gchunks), total]).astype(I32)
    fillc = jnp.where(cnt % MOE_PAD != 0, starts // MOE_PAD + gchunks - 1, -1).astype(I32)
    return e, row, nch, misc, fillc


def kernel(x, mem, rel_table, w_in, w_mem_kv, w_o, lambda_q1, lambda_k1, lambda_q2, lambda_k2, subln_g, conv_w, conv_b, conv_ln_g, conv_ln_b, ln1_g, ln1_b, w_router, b_router, w_gate_up, b_gate_up, w_down, b_down, ln2_g, ln2_b):
    B, S, D = x.shape
    T = B * S
    n_heads = rel_table.shape[1]
    qk_w = n_heads * 2 * HEAD_DIM
    v_w = n_heads * V_DIM
    C = conv_w.shape[-1]
    mem_w = w_mem_kv.shape[-1] // 2
    mem_heads = 4
    mem_hd = mem_w // mem_heads
    in_cols = w_in.shape[-1]
    assert in_cols == 2 * qk_w + v_w + 2 * C + mem_w
    d_ff = w_down.shape[2]
    assert DEPTH == 1 and w_in.shape[0] == 1

    col_scale = jnp.concatenate([
        jnp.full((qk_w,), HEAD_DIM ** -0.5, F32),
        jnp.ones((in_cols - qk_w - mem_w,), F32),
        jnp.full((mem_w,), mem_hd ** -0.5, F32)]).reshape(1, in_cols)

    xt = x.reshape(T, D)
    proj = _proj_matmul(xt.astype(BF16), w_in[0], col_scale, tm=1024, tn=512, out_dtype=BF16)
    proj3 = proj.reshape(B, S, in_cols)

    bias = _bias_tiles(rel_table, ATT_T)
    lam_params = jnp.concatenate([lambda_q1, lambda_k1, lambda_q2, lambda_k2], axis=0).astype(F32)
    a_out = _diff_attention(proj3, bias, lam_params, subln_g, n_heads=n_heads,
                            k_col=qk_w // (2 * HEAD_DIM), v_col=2 * qk_w // V_DIM)

    conv_col = (2 * qk_w + v_w) // C
    c_out = _conformer_conv(proj3, conv_w[0, :, 0, :], conv_b, conv_ln_g, conv_ln_b,
                            a_col=conv_col, g_col=conv_col + 1)

    n_mem = mem.shape[1]
    kv = _proj_matmul(mem.reshape(B * n_mem, D).astype(BF16), w_mem_kv[0],
                      jnp.ones((1, 2 * mem_w), F32), tm=B * n_mem, tn=512, out_dtype=BF16)
    m_out = _memory_attention(proj3, kv.reshape(B, n_mem, 2 * mem_w), n_heads=mem_heads,
                              q_col=(in_cols - mem_w) // mem_hd, hd=mem_hd)

    mix = _out_projection(a_out.reshape(T, v_w), c_out.reshape(T, C), m_out.reshape(T, mem_w),
                          w_o[0], tm=1024, tn=512)

    x1, xp, topi, gates = _ln_router(xt, mix, ln1_g, ln1_b, w_router[0],
                                     b_router.reshape(1, -1), tm=256)

    dest, cnt8, starts8 = _rank_assignments(topi)
    cnt, starts = cnt8[0, :N_EXPERTS], starts8[0, :N_EXPERTS]

    n_assign = T * TOP_K
    n_items = n_assign // MOE_TM + N_EXPERTS
    max_rows = n_assign + N_EXPERTS * (MOE_PAD - 1) + MOE_TM
    n_rows = -(-max_rows // MOE_PAD) * MOE_PAD
    we, wr, wc, misc, fillc = _work_items(cnt, starts, n_items)

    xs = _dispatch(dest, xp, fillc, misc, n_rows)
    h = _moe_up(xs, w_gate_up[0], b_gate_up[0].reshape(N_EXPERTS, 1, -1), we, wr, wc, misc, d_ff=d_ff)
    eo = _moe_down(h, w_down[0], b_down[0].reshape(N_EXPERTS, 1, -1), we, wr, wc, misc)
    out = _combine(dest, eo, gates, x1, ln2_g, ln2_b)
    return out.reshape(B, S, D)
```

```python
import functools
import math

import jax
import jax.numpy as jnp
from jax import lax
from jax.experimental import pallas as pl
from jax.experimental.pallas import tpu as pltpu

F32 = jnp.float32
BF16 = jnp.bfloat16
I32 = jnp.int32
U32 = jnp.uint32

HEAD_DIM = 128
V_DIM = 2 * HEAD_DIM
CONV_TAPS = 31
NUM_BUCKETS = 32
MAX_EXACT = NUM_BUCKETS // 2
MAX_DISTANCE = 128
N_EXPERTS = 32
TOP_K = 4
SWIGLU_LIMIT = 7.0
SWIGLU_ALPHA = 1.702
LN_EPS = 1e-5
DEPTH = 1
DEEPNORM_ALPHA = (2 * DEPTH) ** 0.25
LAM_INIT = 0.8 - 0.6 * math.exp(-0.3 * 0)

LANES = 128
V7X_VMEM_BYTES = 64 * 1024 * 1024
VMEM_LIMIT = 58 * 1024 * 1024

ATT_T = 256
CONV_TS = 256
CONV_HALO = 32
MEM_TQ = 512
MOE_TM = 1088
MOE_MAIN_ROWS = 1024
MOE_PAD = 64
MOE_TN_UP = 256
MOE_UP_ROW_CHUNKS = 2
MOE_TN_DOWN = 512
RANK_TT = 512
DISPATCH_TA = 1024
COMBINE_TC = 128


def _cparams(sem, vmem=VMEM_LIMIT):
    return pltpu.CompilerParams(dimension_semantics=sem, vmem_limit_bytes=vmem)


def _proj_kernel(a_ref, w_ref, cs_ref, o_ref):
    acc = jnp.dot(a_ref[...], w_ref[...].astype(BF16), preferred_element_type=F32)
    o_ref[...] = (acc * cs_ref[...]).astype(o_ref.dtype)


def _proj_matmul(a, w, col_scale, *, tm, tn, out_dtype):
    M, K = a.shape
    N = w.shape[1]
    return pl.pallas_call(
        _proj_kernel,
        grid=(M // tm, N // tn),
        in_specs=[pl.BlockSpec((tm, K), lambda i, j: (i, 0)),
                  pl.BlockSpec((K, tn), lambda i, j: (0, j)),
                  pl.BlockSpec((1, tn), lambda i, j: (0, j))],
        out_specs=pl.BlockSpec((tm, tn), lambda i, j: (i, j)),
        out_shape=jax.ShapeDtypeStruct((M, N), out_dtype),
        compiler_params=_cparams(("parallel", "arbitrary")),
        name="proj_matmul",
    )(a, w, col_scale)


def _bias_kernel(tbl_ref, o_ref, *, t, n_heads):
    h = pl.program_id(0)
    j = lax.broadcasted_iota(I32, (t, t), 0)
    i = lax.broadcasted_iota(I32, (t, t), 1)
    for o in range(2):
        dist = o * t + i - j
        n = jnp.maximum(dist, 0)
        nf = jnp.maximum(n, 1).astype(F32)
        large = MAX_EXACT + (jnp.log(nf / MAX_EXACT) / math.log(MAX_DISTANCE / MAX_EXACT)
                             * (NUM_BUCKETS - MAX_EXACT)).astype(I32)
        large = jnp.minimum(large, NUM_BUCKETS - 1)
        bucket = jnp.where(n < MAX_EXACT, n, large)
        val = jnp.zeros((t, t), F32)
        for b in range(NUM_BUCKETS):
            val = jnp.where(bucket == b, tbl_ref[b * n_heads + h], val)
        if o == 0:
            val = jnp.where(dist >= 0, val, -jnp.inf)
        o_ref[0, o] = val
    o_ref[0, 2] = jnp.full((t, t), tbl_ref[(NUM_BUCKETS - 1) * n_heads + h], F32)


def _bias_tiles(rel_table, t):
    n_heads = rel_table.shape[1]
    assert t + 1 >= MAX_DISTANCE
    return pl.pallas_call(
        functools.partial(_bias_kernel, t=t, n_heads=n_heads),
        grid=(n_heads,),
        in_specs=[pl.BlockSpec(memory_space=pltpu.SMEM)],
        out_specs=pl.BlockSpec((1, 3, t, t), lambda h: (h, 0, 0, 0)),
        out_shape=jax.ShapeDtypeStruct((n_heads, 3, t, t), F32),
        compiler_params=_cparams(("arbitrary",)),
        name="bias_tiles",
    )(rel_table.reshape(-1))


def _attn_kernel(lam_ref, q_ref, k_ref, v_ref, bias_ref, g_ref, o_ref, vt_ref, acc_ref, m_ref, l_ref,
                 *, t):
    qi = pl.program_id(2)
    n_kt = v_ref.shape[1] // t

    @pl.when(qi == 0)
    def _():
        for kb in range(n_kt):
            vt_ref[:, kb * t:(kb + 1) * t] = v_ref[0, kb * t:(kb + 1) * t, :].astype(F32).T.astype(BF16)

    q = q_ref[0]
    qs = (q[:, :HEAD_DIM], q[:, HEAD_DIM:])
    dn = (((1,), (1,)), ((), ()))
    acc_ref[...] = jnp.zeros_like(acc_ref)
    m_ref[...] = jnp.full(m_ref.shape, -jnp.inf, F32)
    l_ref[...] = jnp.zeros_like(l_ref)

    def update(kb0, n_tiles):
        start = pl.multiple_of(kb0 * t, t)
        k = k_ref[0, pl.ds(start, n_tiles * t), :]
        vt = vt_ref[:, pl.ds(start, n_tiles * t)]
        bias = [bias_ref[0, jnp.minimum(qi - kb0 - i, 2)] for i in range(n_tiles)]
        for c in range(2):
            s = [lax.dot_general(k[i * t:(i + 1) * t, c * HEAD_DIM:(c + 1) * HEAD_DIM], qs[c], dn,
                                 preferred_element_type=F32) + bias[i]
                 for i in range(n_tiles)]
            m_old = m_ref[c]
            m_new = m_old
            for si in s:
                m_new = jnp.maximum(m_new, jnp.max(si, axis=0, keepdims=True))
            alpha = jnp.exp(m_old - m_new)
            p = [jnp.exp(si - m_new) for si in s]
            l_new = alpha * l_ref[c]
            for pi in p:
                l_new = l_new + jnp.sum(pi, axis=0, keepdims=True)
            pcat = p[0].astype(BF16) if n_tiles == 1 else jnp.concatenate(
                [pi.astype(BF16) for pi in p], axis=0)
            acc_ref[c] = alpha * acc_ref[c] + jnp.dot(vt, pcat, preferred_element_type=F32)
            m_ref[c] = m_new
            l_ref[c] = l_new

    n = qi + 1

    def pair(i, carry):
        update(2 * i, 2)
        return carry

    lax.fori_loop(0, n // 2, pair, 0)

    @pl.when(n % 2 == 1)
    def _():
        update(qi, 1)

    lp = lam_ref[...]
    lam = (jnp.exp(jnp.sum(lp[0:1] * lp[1:2], axis=-1, keepdims=True))
           - jnp.exp(jnp.sum(lp[2:3] * lp[3:4], axis=-1, keepdims=True)) + LAM_INIT)
    ot = acc_ref[0] * (1.0 / l_ref[0]) - acc_ref[1] * (lam / l_ref[1])
    ot = ot * lax.rsqrt(jnp.mean(jnp.square(ot), axis=0, keepdims=True) + LN_EPS)
    o = ot.T * g_ref[...]
    o_ref[0] = (o * (1.0 - LAM_INIT)).astype(o_ref.dtype)


def _diff_attention(proj3, bias, lam_params, subln_g, *, n_heads, k_col, v_col):
    B, S, _ = proj3.shape
    t = ATT_T
    nq = S // t
    w = 2 * HEAD_DIM
    return pl.pallas_call(
        functools.partial(_attn_kernel, t=t),
        grid=(B, n_heads, nq),
        in_specs=[pl.BlockSpec((4, HEAD_DIM), lambda b, h, i: (0, 0)),
                  pl.BlockSpec((1, t, w), lambda b, h, i: (b, i, h)),
                  pl.BlockSpec((1, S, w), lambda b, h, i: (b, 0, k_col + h)),
                  pl.BlockSpec((1, S, w), lambda b, h, i: (b, 0, v_col + h)),
                  pl.BlockSpec((1, 3, t, t), lambda b, h, i: (h, 0, 0, 0)),
                  pl.BlockSpec((1, V_DIM), lambda b, h, i: (0, 0))],
        out_specs=pl.BlockSpec((1, t, V_DIM), lambda b, h, i: (b, i, h)),
        out_shape=jax.ShapeDtypeStruct((B, S, n_heads * V_DIM), BF16),
        scratch_shapes=[pltpu.VMEM((V_DIM, S), BF16), pltpu.VMEM((2, V_DIM, t), F32),
                        pltpu.VMEM((2, 1, t), F32), pltpu.VMEM((2, 1, t), F32)],
        compiler_params=_cparams(("parallel", "parallel", "arbitrary")),
        name="diff_attention",
    )(lam_params, proj3, proj3, proj3, bias, subln_g)


def _conv_kernel(a_ref, g_ref, ap_ref, gp_ref, w_ref, cb_ref, lg_ref, lb_ref, o_ref,
                 hs_ref, cv_ref, *, ts, halo, taps, rc):
    i = pl.program_id(1)
    C = a_ref.shape[-1]
    hs_ref[halo:, :] = a_ref[0].astype(F32) * jax.nn.sigmoid(g_ref[0].astype(F32))
    hp = ap_ref[0].astype(F32) * jax.nn.sigmoid(gp_ref[0].astype(F32))
    hs_ref[:halo, :] = jnp.where(i > 0, hp, 0.0)
    base = halo - (taps - 1)
    for c in range(C // LANES):
        cs = slice(c * LANES, (c + 1) * LANES)
        for r in range(ts // rc):
            acc = jnp.zeros((rc, LANES), F32)
            for tp in range(taps):
                acc = acc + w_ref[tp:tp + 1, cs] * hs_ref[pl.ds(base + tp + r * rc, rc), cs]
            cv_ref[r * rc:(r + 1) * rc, cs] = acc + cb_ref[:, cs]
    h = cv_ref[...]
    mu = jnp.mean(h, axis=-1, keepdims=True)
    var = jnp.mean(jnp.square(h - mu), axis=-1, keepdims=True)
    y = (h - mu) * lax.rsqrt(var + LN_EPS) * lg_ref[...] + lb_ref[...]
    o_ref[0] = (y * jax.nn.sigmoid(y)).astype(o_ref.dtype)


def _conformer_conv(proj3, conv_w, conv_b, ln_g, ln_b, *, a_col, g_col):
    B, S, _ = proj3.shape
    taps, C = conv_w.shape
    ts, halo = CONV_TS, CONV_HALO
    assert halo >= taps - 1
    hb = ts // halo
    prev = lambda col: (lambda b, i: (b, jnp.maximum(i * hb - 1, 0), col))
    vec = pl.BlockSpec((1, C), lambda b, i: (0, 0))
    return pl.pallas_call(
        functools.partial(_conv_kernel, ts=ts, halo=halo, taps=taps, rc=128),
        grid=(B, S // ts),
        in_specs=[pl.BlockSpec((1, ts, C), lambda b, i: (b, i, a_col)),
                  pl.BlockSpec((1, ts, C), lambda b, i: (b, i, g_col)),
                  pl.BlockSpec((1, halo, C), prev(a_col)),
                  pl.BlockSpec((1, halo, C), prev(g_col)),
                  pl.BlockSpec((taps, C), lambda b, i: (0, 0)),
                  vec, vec, vec],
        out_specs=pl.BlockSpec((1, ts, C), lambda b, i: (b, i, 0)),
        out_shape=jax.ShapeDtypeStruct((B, S, C), BF16),
        scratch_shapes=[pltpu.VMEM((halo + ts, C), F32), pltpu.VMEM((ts, C), F32)],
        compiler_params=_cparams(("parallel", "arbitrary")),
        name="conformer_conv",
    )(proj3, proj3, proj3, proj3, conv_w, conv_b, ln_g, ln_b)


def _mem_attn_kernel(q_ref, k_ref, v_ref, o_ref):
    s = lax.dot_general(q_ref[0], k_ref[0], (((1,), (1,)), ((), ())), preferred_element_type=F32)
    m = jnp.max(s, axis=-1, keepdims=True)
    p = jnp.exp(s - m)
    p = p / jnp.sum(p, axis=-1, keepdims=True)
    o_ref[0] = jnp.dot(p.astype(BF16), v_ref[0], preferred_element_type=F32).astype(o_ref.dtype)


def _memory_attention(proj3, kv3, *, n_heads, q_col, hd):
    B, S, _ = proj3.shape
    M = kv3.shape[1]
    tq = MEM_TQ
    return pl.pallas_call(
        _mem_attn_kernel,
        grid=(B, n_heads, S // tq),
        in_specs=[pl.BlockSpec((1, tq, hd), lambda b, h, i: (b, i, q_col + h)),
                  pl.BlockSpec((1, M, hd), lambda b, h, i: (b, 0, h)),
                  pl.BlockSpec((1, M, hd), lambda b, h, i: (b, 0, n_heads + h))],
        out_specs=pl.BlockSpec((1, tq, hd), lambda b, h, i: (b, i, h)),
        out_shape=jax.ShapeDtypeStruct((B, S, n_heads * hd), BF16),
        compiler_params=_cparams(("parallel", "parallel", "arbitrary")),
        name="memory_attention",
    )(proj3, kv3, kv3)


def _oproj_kernel(a_ref, c_ref, m_ref, wa_ref, wc_ref, wm_ref, o_ref):
    acc = jnp.dot(a_ref[...], wa_ref[...].astype(BF16), preferred_element_type=F32)
    acc += jnp.dot(c_ref[...], wc_ref[...].astype(BF16), preferred_element_type=F32)
    acc += jnp.dot(m_ref[...], wm_ref[...].astype(BF16), preferred_element_type=F32)
    o_ref[...] = acc


def _out_projection(a, c, m, w_o, *, tm, tn):
    M, Ka = a.shape
    Kc, Km = c.shape[1], m.shape[1]
    assert Kc == Km and Ka % Kc == 0
    N = w_o.shape[1]
    return pl.pallas_call(
        _oproj_kernel,
        grid=(M // tm, N // tn),
        in_specs=[pl.BlockSpec((tm, Ka), lambda i, j: (i, 0)),
                  pl.BlockSpec((tm, Kc), lambda i, j: (i, 0)),
                  pl.BlockSpec((tm, Km), lambda i, j: (i, 0)),
                  pl.BlockSpec((Ka, tn), lambda i, j: (0, j)),
                  pl.BlockSpec((Kc, tn), lambda i, j: (Ka // Kc, j)),
                  pl.BlockSpec((Km, tn), lambda i, j: (Ka // Kc + 1, j))],
        out_specs=pl.BlockSpec((tm, tn), lambda i, j: (i, j)),
        out_shape=jax.ShapeDtypeStruct((M, N), F32),
        compiler_params=_cparams(("parallel", "arbitrary")),
        name="out_projection",
    )(a, c, m, w_o, w_o, w_o)


def _pack_pairs(lo, hi):
    lo_b = lax.bitcast_convert_type(lo.astype(BF16).astype(F32), U32)
    hi_b = lax.bitcast_convert_type(hi.astype(BF16).astype(F32), U32)
    return (lo_b >> 16) | (hi_b & jnp.uint32(0xFFFF0000))


def _unpack_lo(u):
    return lax.bitcast_convert_type(u << 16, F32)


def _unpack_hi(u):
    return lax.bitcast_convert_type(u & jnp.uint32(0xFFFF0000), F32)


def _layer_norm(y, g, b):
    mu = jnp.mean(y, axis=-1, keepdims=True)
    var = jnp.mean(jnp.square(y - mu), axis=-1, keepdims=True)
    return (y - mu) * lax.rsqrt(var + LN_EPS) * g + b


def _ln_router_kernel(x_ref, mix_ref, g_ref, b_ref, wr_ref, br_ref,
                      x1_ref, xp_ref, topi_ref, gate_ref):
    x1 = _layer_norm(DEEPNORM_ALPHA * x_ref[...] + mix_ref[...], g_ref[...], b_ref[...])
    x1_ref[...] = x1
    half = x1.shape[1] // 2
    xp_ref[...] = _pack_pairs(x1[:, :half], x1[:, half:])
    logits = jnp.dot(x1, wr_ref[...], preferred_element_type=F32,
                     precision=lax.Precision.HIGHEST) + br_ref[...]
    tm, ne = logits.shape
    lane = lax.broadcasted_iota(I32, (tm, ne), 1)
    kk = lax.broadcasted_iota(I32, (tm, TOP_K), 1)
    idx = jnp.zeros((tm, TOP_K), I32)
    val = jnp.zeros((tm, TOP_K), F32)
    cur = logits
    for k in range(TOP_K):
        mx = jnp.max(cur, axis=-1, keepdims=True)
        ix = jnp.min(jnp.where(cur == mx, lane, ne), axis=-1, keepdims=True)
        idx = jnp.where(kk == k, ix, idx)
        val = jnp.where(kk == k, mx, val)
        cur = jnp.where(lane == ix, -jnp.inf, cur)
    e = jnp.exp(val - jnp.max(val, axis=-1, keepdims=True))
    topi_ref[...] = idx
    gate_ref[...] = e / jnp.sum(e, axis=-1, keepdims=True)


def _ln_router(x, mix, g, b, w_router, b_router, *, tm):
    T, D = x.shape
    ne = w_router.shape[1]
    row = pl.BlockSpec((tm, D), lambda i: (i, 0))
    vec = pl.BlockSpec((1, D), lambda i: (0, 0))
    return pl.pallas_call(
        _ln_router_kernel,
        grid=(T // tm,),
        in_specs=[row, row, vec, vec,
                  pl.BlockSpec((D, ne), lambda i: (0, 0)),
                  pl.BlockSpec((1, ne), lambda i: (0, 0))],
        out_specs=[row,
                   pl.BlockSpec((tm, D // 2), lambda i: (i, 0)),
                   pl.BlockSpec((tm, TOP_K), lambda i: (i, 0)),
                   pl.BlockSpec((tm, TOP_K), lambda i: (i, 0))],
        out_shape=[jax.ShapeDtypeStruct((T, D), F32),
                   jax.ShapeDtypeStruct((T, D // 2), U32),
                   jax.ShapeDtypeStruct((T, TOP_K), I32),
                   jax.ShapeDtypeStruct((T, TOP_K), F32)],
        compiler_params=_cparams(("parallel",)),
        name="ln_router",
    )(x, mix, g, b, w_router, b_router)


def _rank_kernel(topi_ref, dest_ref, cnt_ref, ps_ref, rank_ref, carry_ref, *, tt, pad):
    ph = pl.program_id(0)
    i = pl.program_id(1)
    lane = lax.broadcasted_iota(I32, (tt, LANES), 1)
    kk = lax.broadcasted_iota(I32, (tt, TOP_K), 1)
    ti = topi_ref[...]
    sel = [lane == ti[:, k:k + 1] for k in range(TOP_K)]
    rows = pl.ds(pl.multiple_of(i * tt, tt), tt)

    @pl.when(ph == 0)
    def _():
        @pl.when(i == 0)
        def _():
            carry_ref[...] = jnp.zeros_like(carry_ref)

        oh = jnp.zeros((tt, LANES), F32)
        for k in range(TOP_K):
            oh = oh + sel[k].astype(F32)
        r = lax.broadcasted_iota(I32, (tt, tt), 0)
        c = lax.broadcasted_iota(I32, (tt, tt), 1)
        lower = (r > c).astype(BF16)
        before = jnp.dot(lower, oh.astype(BF16), preferred_element_type=F32) + carry_ref[0:1, :]
        rank = jnp.zeros((tt, TOP_K), F32)
        for k in range(TOP_K):
            rk = jnp.sum(jnp.where(sel[k], before, 0.0), axis=-1, keepdims=True)
            rank = jnp.where(kk == k, rk, rank)
        rank_ref[rows, :] = rank
        carry_ref[...] = carry_ref[...] + jnp.sum(oh, axis=0, keepdims=True)

    @pl.when(ph == 1)
    def _():
        cnt = carry_ref[...].astype(I32)
        padded = (cnt + (pad - 1)) & jnp.int32(-pad)
        l8 = lax.broadcasted_iota(I32, (8, LANES), 1)
        scan = padded
        for sh in (1, 2, 4, 8, 16, 32, 64):
            scan = scan + jnp.where(l8 >= sh, pltpu.roll(scan, sh, axis=1), 0)
        starts = scan - padded
        cnt_ref[...] = cnt
        ps_ref[...] = starts
        st = starts[0:1, :].astype(F32)
        rank = rank_ref[rows, :]
        dest = jnp.zeros((tt, TOP_K), F32)
        for k in range(TOP_K):
            base = jnp.sum(jnp.where(sel[k], st, 0.0), axis=-1, keepdims=True)
            dest = jnp.where(kk == k, base, dest)
        dest_ref[...] = (dest + rank).astype(I32)


def _rank_assignments(topi):
    T = topi.shape[0]
    tt = RANK_TT
    return pl.pallas_call(
        functools.partial(_rank_kernel, tt=tt, pad=MOE_PAD),
        grid=(2, T // tt),
        in_specs=[pl.BlockSpec((tt, TOP_K), lambda p, i: (i, 0))],
        out_specs=[pl.BlockSpec((tt, TOP_K), lambda p, i: (i * p, 0)),
                   pl.BlockSpec((8, LANES), lambda p, i: (0, 0)),
                   pl.BlockSpec((8, LANES), lambda p, i: (0, 0))],
        out_shape=[jax.ShapeDtypeStruct((T, TOP_K), I32),
                   jax.ShapeDtypeStruct((8, LANES), I32),
                   jax.ShapeDtypeStruct((8, LANES), I32)],
        scratch_shapes=[pltpu.VMEM((T, TOP_K), F32), pltpu.VMEM((8, LANES), F32)],
        compiler_params=_cparams(("arbitrary", "arbitrary")),
        name="rank_assignments",
    )(topi)


def _dispatch_kernel(fillc_ref, misc_ref, dest_ref, xp_ref, xs_ref, zero_ref, fill_sem, row_sem, *, ta):
    i = pl.program_id(0)
    tt = xp_ref.shape[0]

    @pl.when(i == 0)
    def _():
        def fill(e):
            row = pl.multiple_of(fillc_ref[e] * MOE_PAD, MOE_PAD)
            return pltpu.make_async_copy(zero_ref, xs_ref.at[pl.ds(row, MOE_PAD), :], fill_sem)

        def each_group(fn):
            for e in range(N_EXPERTS):
                pl.when(fillc_ref[e] >= 0)(functools.partial(fn, e))

        zero_ref[...] = jnp.zeros_like(zero_ref)
        each_group(lambda e: fill(e).start())
        _fill_tail(misc_ref, zero_ref, xs_ref, fill_sem)
        each_group(lambda e: fill(e).wait())

    def issue(t, carry):
        for k in range(TOP_K):
            pltpu.make_async_copy(xp_ref.at[pl.ds(t, 1), :],
                                  xs_ref.at[pl.ds(dest_ref[0, 0, t * TOP_K + k], 1), :],
                                  row_sem).start()
        return carry

    lax.fori_loop(0, tt, issue, 0, unroll=2)
    for _ in range(ta // tt):
        pltpu.make_async_copy(xp_ref, xs_ref.at[pl.ds(0, tt), :], row_sem).wait()


def _dispatch(dest, xp, fillc, misc, n_rows):
    T, W = xp.shape
    ta = DISPATCH_TA
    na = dest.size // ta
    assert n_rows % MOE_PAD == 0 and ta % TOP_K == 0
    return pl.pallas_call(
        functools.partial(_dispatch_kernel, ta=ta),
        grid_spec=pltpu.PrefetchScalarGridSpec(
            num_scalar_prefetch=2,
            grid=(na,),
            in_specs=[pl.BlockSpec((1, 1, ta), lambda i, fc, ms: (i, 0, 0), memory_space=pltpu.SMEM),
                      pl.BlockSpec((ta // TOP_K, W), lambda i, fc, ms: (i, 0))],
            out_specs=pl.BlockSpec(memory_space=pl.ANY),
            scratch_shapes=[pltpu.VMEM((MOE_PAD, W), U32),
                            pltpu.SemaphoreType.DMA(()), pltpu.SemaphoreType.DMA(())]),
        out_shape=jax.ShapeDtypeStruct((n_rows, W), U32),
        compiler_params=_cparams(("arbitrary",)),
        name="dispatch_rows",
    )(fillc, misc, dest.reshape(na, 1, ta), xp)


class _TileWriter:
    def __init__(self, wr_ref, wc_ref, stage_ref, out_ref, sem, tn):
        self.wr_ref, self.wc_ref = wr_ref, wc_ref
        self.stage_ref, self.out_ref, self.sem, self.tn = stage_ref, out_ref, sem, tn
        n_chunks = stage_ref.shape[1] // MOE_PAD
        self.sizes = [1 << b for b in reversed(range(n_chunks.bit_length()))]

    def _copy(self, slot, w, j, size):
        n = self.wc_ref[w]
        first = pl.multiple_of((n & -(2 * size)) * MOE_PAD, MOE_PAD)
        row = pl.multiple_of(self.wr_ref[w] + first, MOE_PAD)
        col = pl.multiple_of(j * self.tn, self.tn)
        return pltpu.make_async_copy(
            self.stage_ref.at[slot, pl.ds(first, size * MOE_PAD), :],
            self.out_ref.at[pl.ds(row, size * MOE_PAD), pl.ds(col, self.tn)],
            self.sem.at[slot])

    def _each_group(self, w, fn):
        for size in self.sizes:
            pl.when((self.wc_ref[w] & size) != 0)(functools.partial(fn, size))

    def start(self, slot, w, j):
        self._each_group(w, lambda size: self._copy(slot, w, j, size).start())

    def wait(self, slot, w, j):
        self._each_group(w, lambda size: self._copy(slot, w, j, size).wait())

    def retire_older(self, w, j, nj):
        step = w * nj + j

        @pl.when(step >= 2)
        def _():
            wrap = j < 2
            self.wait(step % 2, jnp.where(wrap, w - 1, w), jnp.where(wrap, j + nj - 2, j - 2))

    def drain(self, w, j, nw, nj):
        step = w * nj + j

        @pl.when(step == nw * nj - 1)
        def _():
            self.wait((step + 1) % 2, w, j - 1)
            self.wait(step % 2, w, j)


def _fill_tail(misc_ref, zero_ref, out_ref, sem):
    n_total = out_ref.shape[0] // MOE_PAD

    def fill(c):
        row = pl.multiple_of(c * MOE_PAD, MOE_PAD)
        return pltpu.make_async_copy(zero_ref, out_ref.at[pl.ds(row, MOE_PAD), :], sem)

    def start(c, carry):
        fill(c).start()
        return carry

    def wait(c, carry):
        fill(c).wait()
        return carry

    lax.fori_loop(misc_ref[0], n_total, start, 0)
    lax.fori_loop(misc_ref[0], n_total, wait, 0)


def _moe_up_kernel(we_ref, wr_ref, wc_ref, misc_ref, xs_ref, wg_ref, wu_ref, b_ref,
                   h_ref, xb_ref, wcat_ref, stage_ref, zero_ref, sem, fill_sem):
    w = pl.program_id(0)
    j = pl.program_id(1)
    nw = pl.num_programs(0)
    nj = pl.num_programs(1)
    slot = (w * nj + j) % 2
    writer = _TileWriter(wr_ref, wc_ref, stage_ref, h_ref, sem, stage_ref.shape[2])

    @pl.when((w == 0) & (j == 0))
    def _():
        zero_ref[...] = jnp.zeros_like(zero_ref)
        _fill_tail(misc_ref, zero_ref, h_ref, fill_sem)

    writer.retire_older(w, j, nj)

    @pl.when(wc_ref[w] > 0)
    def _():
        half = xs_ref.shape[1]

        @pl.when(j == 0)
        def _():
            u = xs_ref[...]
            xb_ref[:, :half] = _unpack_lo(u).astype(BF16)
            xb_ref[:, half:] = _unpack_hi(u).astype(BF16)

        tn = wg_ref.shape[2]
        wcat_ref[:, :tn] = wg_ref[0].astype(BF16)
        wcat_ref[:, tn:] = wu_ref[0].astype(BF16)
        col = pl.multiple_of(j * tn, tn)
        bg = b_ref[0, :, pl.ds(col, tn)]
        bu = b_ref[0, :, pl.ds(pl.multiple_of(b_ref.shape[2] // 2 + col, tn), tn)]
        def rows_block(rows):
            r = jnp.dot(xb_ref[rows, :], wcat_ref[...], preferred_element_type=F32)
            gate = r[:, :tn] + bg
            up = r[:, tn:] + bu
            gate = jnp.minimum(gate, SWIGLU_LIMIT)
            up = jnp.clip(up, -SWIGLU_LIMIT, SWIGLU_LIMIT)
            act = (up + 1.0) * (gate * jax.nn.sigmoid(SWIGLU_ALPHA * gate))
            stage_ref[slot, rows, :] = act.astype(stage_ref.dtype)

        def all_rows(n_rows):
            mc = n_rows // MOE_UP_ROW_CHUNKS
            for mi in range(MOE_UP_ROW_CHUNKS):
                rows_block(slice(mi * mc, (mi + 1) * mc))

        tm, main = xb_ref.shape[0], min(MOE_MAIN_ROWS, xb_ref.shape[0])
        if tm > main:
            fits = wc_ref[w] * MOE_PAD <= main
            pl.when(fits)(functools.partial(all_rows, main))
            pl.when(jnp.logical_not(fits))(functools.partial(all_rows, tm))
        else:
            all_rows(tm)
        writer.start(slot, w, j)

    writer.drain(w, j, nw, nj)


def _moe_up(xs, w_gu, b_gu, we, wr, wc, misc, *, d_ff):
    R, half = xs.shape
    E, D, _ = w_gu.shape
    tm, tn = MOE_TM, MOE_TN_UP
    nj = d_ff // tn
    nw = we.shape[0]
    assert nj >= 2 and R % MOE_PAD == 0 and tm % (16 * MOE_UP_ROW_CHUNKS) == 0
    jeff = lambda w, j, wc: jnp.where(wc[w] > 0, j, nj - 1)
    return pl.pallas_call(
        _moe_up_kernel,
        grid_spec=pltpu.PrefetchScalarGridSpec(
            num_scalar_prefetch=4,
            grid=(misc[1], nj),
            in_specs=[pl.BlockSpec((pl.Element(tm), pl.Element(half)),
                                   lambda w, j, we, wr, wc, ms: (pl.multiple_of(wr[w], MOE_PAD), 0)),
                      pl.BlockSpec((1, D, tn), lambda w, j, we, wr, wc, ms: (we[w], 0, jeff(w, j, wc))),
                      pl.BlockSpec((1, D, tn), lambda w, j, we, wr, wc, ms: (we[w], 0, nj + jeff(w, j, wc))),
                      pl.BlockSpec((1, 1, 2 * nj * tn), lambda w, j, we, wr, wc, ms: (we[w], 0, 0))],
            out_specs=pl.BlockSpec(memory_space=pl.ANY),
            scratch_shapes=[pltpu.VMEM((tm, 2 * half), BF16),
                            pltpu.VMEM((D, 2 * tn), BF16),
                            pltpu.VMEM((2, tm, tn), BF16),
                            pltpu.VMEM((MOE_PAD, d_ff), BF16),
                            pltpu.SemaphoreType.DMA((2,)), pltpu.SemaphoreType.DMA(())]),
        out_shape=jax.ShapeDtypeStruct((R, d_ff), BF16),
        compiler_params=_cparams(("arbitrary", "arbitrary")),
        name="moe_up",
    )(we, wr, wc, misc, xs, w_gu, w_gu, b_gu)


def _moe_down_kernel(we_ref, wr_ref, wc_ref, misc_ref, h_ref, w_ref, b_ref,
                     o_ref, stage_ref, zero_ref, sem, fill_sem):
    w = pl.program_id(0)
    j = pl.program_id(1)
    nw = pl.num_programs(0)
    nj = pl.num_programs(1)
    slot = (w * nj + j) % 2
    writer = _TileWriter(wr_ref, wc_ref, stage_ref, o_ref, sem, stage_ref.shape[2])

    @pl.when((w == 0) & (j == 0))
    def _():
        zero_ref[...] = jnp.zeros_like(zero_ref)
        _fill_tail(misc_ref, zero_ref, o_ref, fill_sem)

    writer.retire_older(w, j, nj)

    @pl.when(wc_ref[w] > 0)
    def _():
        tn = stage_ref.shape[2]
        bias = b_ref[0, :, pl.ds(pl.multiple_of(j * 2 * tn, 2 * tn), 2 * tn)]

        def rows_block(rows):
            r = jnp.dot(h_ref[rows, :], w_ref[0].astype(BF16), preferred_element_type=F32) + bias
            stage_ref[slot, rows, :] = _pack_pairs(r[:, :tn], r[:, tn:])

        tm, main = h_ref.shape[0], min(MOE_MAIN_ROWS, h_ref.shape[0])
        if tm > main:
            fits = wc_ref[w] * MOE_PAD <= main
            pl.when(fits)(functools.partial(rows_block, slice(0, main)))
            pl.when(jnp.logical_not(fits))(functools.partial(rows_block, slice(0, tm)))
        else:
            rows_block(slice(0, tm))
        writer.start(slot, w, j)

    writer.drain(w, j, nw, nj)


def _moe_down(h, w_down, b_down, we, wr, wc, misc):
    R, d_ff = h.shape
    E, _, D = w_down.shape
    tm, tn = MOE_TM, MOE_TN_DOWN
    half = D // 2
    nj = half // tn
    nw = we.shape[0]
    assert nj >= 2 and R % MOE_PAD == 0
    jeff = lambda w, j, wc: jnp.where(wc[w] > 0, j, nj - 1)
    return pl.pallas_call(
        _moe_down_kernel,
        grid_spec=pltpu.PrefetchScalarGridSpec(
            num_scalar_prefetch=4,
            grid=(misc[1], nj),
            in_specs=[pl.BlockSpec((pl.Element(tm), pl.Element(d_ff)),
                                   lambda w, j, we, wr, wc, ms: (pl.multiple_of(wr[w], MOE_PAD), 0)),
                      pl.BlockSpec((1, d_ff, 2 * tn), lambda w, j, we, wr, wc, ms: (we[w], 0, jeff(w, j, wc))),
                      pl.BlockSpec((1, 1, 2 * nj * tn), lambda w, j, we, wr, wc, ms: (we[w], 0, 0))],
            out_specs=pl.BlockSpec(memory_space=pl.ANY),
            scratch_shapes=[pltpu.VMEM((2, tm, tn), U32),
                            pltpu.VMEM((MOE_PAD, half), U32),
                            pltpu.SemaphoreType.DMA((2,)), pltpu.SemaphoreType.DMA(())]),
        out_shape=jax.ShapeDtypeStruct((R, half), U32),
        compiler_params=_cparams(("arbitrary", "arbitrary")),
        name="moe_down",
    )(we, wr, wc, misc, h, w_down, b_down)


def _combine_kernel(dest_ref, dnext_ref, eo_ref, gate_ref, x1_ref, g_ref, b_ref, o_ref, buf_ref, sem,
                    *, tc):
    i = pl.program_id(0)
    n = tc * TOP_K
    slot = i % 2

    def gather(d_ref, slot):
        def issue(t, carry):
            for k in range(TOP_K):
                pltpu.make_async_copy(eo_ref.at[pl.ds(d_ref[0, 0, t * TOP_K + k], 1), :],
                                      buf_ref.at[slot, pl.ds(k * tc + t, 1), :],
                                      sem.at[slot]).start()
            return carry

        lax.fori_loop(0, tc, issue, 0, unroll=2)

    @pl.when(i == 0)
    def _():
        gather(dest_ref, 0)

    @pl.when(i + 1 < pl.num_programs(0))
    def _():
        gather(dnext_ref, 1 - slot)

    pltpu.make_async_copy(eo_ref.at[pl.ds(0, n), :], buf_ref.at[slot], sem.at[slot]).wait()

    gates = gate_ref[...]
    half = buf_ref.shape[2]
    cw = 2 * LANES
    row_sum = jnp.zeros((tc, 1), F32)
    for c in range(half // cw):
        tn = MOE_TN_DOWN
        u_cols = slice(c * cw, (c + 1) * cw)
        first = (c * cw // tn) * 2 * tn + c * cw % tn
        lo_cols = slice(first, first + cw)
        hi_cols = slice(first + tn, first + tn + cw)
        lo = DEEPNORM_ALPHA * x1_ref[:, lo_cols]
        hi = DEEPNORM_ALPHA * x1_ref[:, hi_cols]
        for k in range(TOP_K):
            u = buf_ref[slot, k * tc:(k + 1) * tc, u_cols]
            gk = gates[:, k:k + 1]
            lo = lo + gk * _unpack_lo(u)
            hi = hi + gk * _unpack_hi(u)
        o_ref[:, lo_cols] = lo
        o_ref[:, hi_cols] = hi
        row_sum = row_sum + jnp.sum(lo, axis=-1, keepdims=True) + jnp.sum(hi, axis=-1, keepdims=True)
    inv_d = 1.0 / (2 * half)
    mu = row_sum * inv_d
    sq_sum = jnp.zeros((tc, 1), F32)
    for c in range(2 * half // cw):
        d = o_ref[:, c * cw:(c + 1) * cw] - mu
        sq_sum = sq_sum + jnp.sum(d * d, axis=-1, keepdims=True)
    rstd = lax.rsqrt(sq_sum * inv_d + LN_EPS)
    for c in range(2 * half // cw):
        cols = slice(c * cw, (c + 1) * cw)
        o_ref[:, cols] = (o_ref[:, cols] - mu) * rstd * g_ref[:, cols] + b_ref[:, cols]


def _combine(dest, eo, gates, x1, g, b):
    T, D = x1.shape
    tc = COMBINE_TC
    n = tc * TOP_K
    nt = T // tc
    half = eo.shape[1]
    row = pl.BlockSpec((tc, D), lambda i: (i, 0))
    vec = pl.BlockSpec((1, D), lambda i: (0, 0))
    dest3 = dest.reshape(nt, 1, n)
    return pl.pallas_call(
        functools.partial(_combine_kernel, tc=tc),
        grid=(nt,),
        in_specs=[pl.BlockSpec((1, 1, n), lambda i: (i, 0, 0), memory_space=pltpu.SMEM),
                  pl.BlockSpec((1, 1, n), lambda i: (jnp.minimum(i + 1, nt - 1), 0, 0),
                               memory_space=pltpu.SMEM),
                  pl.BlockSpec(memory_space=pl.ANY),
                  pl.BlockSpec((tc, TOP_K), lambda i: (i, 0)),
                  row, vec, vec],
        out_specs=row,
        out_shape=jax.ShapeDtypeStruct((T, D), F32),
        scratch_shapes=[pltpu.VMEM((2, n, half), U32), pltpu.SemaphoreType.DMA((2,))],
        compiler_params=_cparams(("arbitrary",)),
        name="combine_rows",
    )(dest3, dest3, eo, gates, x1, g, b)


def _work_items(cnt, starts, n_items):
    chunks_per_pass = MOE_TM // MOE_PAD
    gchunks = (cnt + MOE_PAD - 1) // MOE_PAD
    npass = (gchunks + chunks_per_pass - 1) // chunks_per_pass
    cum = jnp.cumsum(npass)
    total = cum[-1]
    w = jnp.arange(n_items, dtype=I32)
    e = jnp.minimum(jnp.sum((cum[None, :] <= w[:, None]).astype(I32), axis=1), N_EXPERTS - 1)
    p = w - (cum - npass)[e]
    row = starts[e] + p * MOE_TM
    nch = jnp.clip(gchunks[e] - p * chunks_per_pass, 0, chunks_per_pass)
    valid = w < total
    last = jnp.maximum(total - 1, 0)
    e = jnp.where(valid, e, e[last]).astype(I32)
    row = jnp.where(valid, row, row[last]).astype(I32)
    nch = jnp.where(valid, nch, 0).astype(I32)
    misc = jnp.stack([jnp.sum(gchunks), total]).astype(I32)
    fillc = jnp.where(cnt % MOE_PAD != 0, starts // MOE_PAD + gchunks - 1, -1).astype(I32)
    return e, row, nch, misc, fillc


def kernel(x, mem, rel_table, w_in, w_mem_kv, w_o, lambda_q1, lambda_k1, lambda_q2, lambda_k2, subln_g, conv_w, conv_b, conv_ln_g, conv_ln_b, ln1_g, ln1_b, w_router, b_router, w_gate_up, b_gate_up, w_down, b_down, ln2_g, ln2_b):
    B, S, D = x.shape
    T = B * S
    n_heads = rel_table.shape[1]
    qk_w = n_heads * 2 * HEAD_DIM
    v_w = n_heads * V_DIM
    C = conv_w.shape[-1]
    mem_w = w_mem_kv.shape[-1] // 2
    mem_heads = 4
    mem_hd = mem_w // mem_heads
    in_cols = w_in.shape[-1]
    assert in_cols == 2 * qk_w + v_w + 2 * C + mem_w
    d_ff = w_down.shape[2]
    assert DEPTH == 1 and w_in.shape[0] == 1

    col_scale = jnp.concatenate([
        jnp.full((qk_w,), HEAD_DIM ** -0.5, F32),
        jnp.ones((in_cols - qk_w - mem_w,), F32),
        jnp.full((mem_w,), mem_hd ** -0.5, F32)]).reshape(1, in_cols)

    xt = x.reshape(T, D)
    proj = _proj_matmul(xt.astype(BF16), w_in[0], col_scale, tm=1024, tn=512, out_dtype=BF16)
    proj3 = proj.reshape(B, S, in_cols)

    bias = _bias_tiles(rel_table, ATT_T)
    lam_params = jnp.concatenate([lambda_q1, lambda_k1, lambda_q2, lambda_k2], axis=0).astype(F32)
    a_out = _diff_attention(proj3, bias, lam_params, subln_g, n_heads=n_heads,
                            k_col=qk_w // (2 * HEAD_DIM), v_col=2 * qk_w // V_DIM)

    conv_col = (2 * qk_w + v_w) // C
    c_out = _conformer_conv(proj3, conv_w[0, :, 0, :], conv_b, conv_ln_g, conv_ln_b,
                            a_col=conv_col, g_col=conv_col + 1)

    n_mem = mem.shape[1]
    kv = _proj_matmul(mem.reshape(B * n_mem, D).astype(BF16), w_mem_kv[0],
                      jnp.ones((1, 2 * mem_w), F32), tm=B * n_mem, tn=512, out_dtype=BF16)
    m_out = _memory_attention(proj3, kv.reshape(B, n_mem, 2 * mem_w), n_heads=mem_heads,
                              q_col=(in_cols - mem_w) // mem_hd, hd=mem_hd)

    mix = _out_projection(a_out.reshape(T, v_w), c_out.reshape(T, C), m_out.reshape(T, mem_w),
                          w_o[0], tm=1024, tn=512)

    x1, xp, topi, gates = _ln_router(xt, mix, ln1_g, ln1_b, w_router[0],
                                     b_router.reshape(1, -1), tm=256)

    dest, cnt8, starts8 = _rank_assignments(topi)
    cnt, starts = cnt8[0, :N_EXPERTS], starts8[0, :N_EXPERTS]

    n_assign = T * TOP_K
    n_items = n_assign // MOE_TM + N_EXPERTS
    max_rows = n_assign + N_EXPERTS * (MOE_PAD - 1) + MOE_TM
    n_rows = -(-max_rows // MOE_PAD) * MOE_PAD
    we, wr, wc, misc, fillc = _work_items(cnt, starts, n_items)

    xs = _dispatch(dest, xp, fillc, misc, n_rows)
    h = _moe_up(xs, w_gate_up[0], b_gate_up[0].reshape(N_EXPERTS, 1, -1), we, wr, wc, misc, d_ff=d_ff)
    eo = _moe_down(h, w_down[0], b_down[0].reshape(N_EXPERTS, 1, -1), we, wr, wc, misc)
    out = _combine(dest, eo, gates, x1, ln2_g, ln2_b)
    return out.reshape(B, S, D)
```

```python
import functools
import math

import jax
import jax.numpy as jnp
from jax import lax
from jax.experimental import pallas as pl
from jax.experimental.pallas import tpu as pltpu

F32 = jnp.float32
BF16 = jnp.bfloat16
I32 = jnp.int32
U32 = jnp.uint32

HEAD_DIM = 128
V_DIM = 2 * HEAD_DIM
CONV_TAPS = 31
NUM_BUCKETS = 32
MAX_EXACT = NUM_BUCKETS // 2
MAX_DISTANCE = 128
N_EXPERTS = 32
TOP_K = 4
SWIGLU_LIMIT = 7.0
SWIGLU_ALPHA = 1.702
LN_EPS = 1e-5
DEPTH = 1
DEEPNORM_ALPHA = (2 * DEPTH) ** 0.25
LAM_INIT = 0.8 - 0.6 * math.exp(-0.3 * 0)

LANES = 128
V7X_VMEM_BYTES = 64 * 1024 * 1024
VMEM_LIMIT = 58 * 1024 * 1024

ATT_T = 256
CONV_TS = 256
CONV_HALO = 32
MEM_TQ = 512
MOE_TM = 1088
MOE_MAIN_ROWS = 1024
MOE_PAD = 64
MOE_TN_UP = 256
MOE_UP_ROW_CHUNKS = 1
MOE_TN_DOWN = 512
RANK_TT = 512
DISPATCH_TA = 1024
COMBINE_TC = 128


def _cparams(sem, vmem=VMEM_LIMIT):
    return pltpu.CompilerParams(dimension_semantics=sem, vmem_limit_bytes=vmem)


def _proj_kernel(a_ref, w_ref, cs_ref, o_ref):
    acc = jnp.dot(a_ref[...], w_ref[...].astype(BF16), preferred_element_type=F32)
    o_ref[...] = (acc * cs_ref[...]).astype(o_ref.dtype)


def _proj_matmul(a, w, col_scale, *, tm, tn, out_dtype):
    M, K = a.shape
    N = w.shape[1]
    return pl.pallas_call(
        _proj_kernel,
        grid=(M // tm, N // tn),
        in_specs=[pl.BlockSpec((tm, K), lambda i, j: (i, 0)),
                  pl.BlockSpec((K, tn), lambda i, j: (0, j)),
                  pl.BlockSpec((1, tn), lambda i, j: (0, j))],
        out_specs=pl.BlockSpec((tm, tn), lambda i, j: (i, j)),
        out_shape=jax.ShapeDtypeStruct((M, N), out_dtype),
        compiler_params=_cparams(("parallel", "arbitrary")),
        name="proj_matmul",
    )(a, w, col_scale)


def _bias_kernel(tbl_ref, o_ref, *, t, n_heads):
    h = pl.program_id(0)
    j = lax.broadcasted_iota(I32, (t, t), 0)
    i = lax.broadcasted_iota(I32, (t, t), 1)
    for o in range(2):
        dist = o * t + i - j
        n = jnp.maximum(dist, 0)
        nf = jnp.maximum(n, 1).astype(F32)
        large = MAX_EXACT + (jnp.log(nf / MAX_EXACT) / math.log(MAX_DISTANCE / MAX_EXACT)
                             * (NUM_BUCKETS - MAX_EXACT)).astype(I32)
        large = jnp.minimum(large, NUM_BUCKETS - 1)
        bucket = jnp.where(n < MAX_EXACT, n, large)
        val = jnp.zeros((t, t), F32)
        for b in range(NUM_BUCKETS):
            val = jnp.where(bucket == b, tbl_ref[b * n_heads + h], val)
        if o == 0:
            val = jnp.where(dist >= 0, val, -jnp.inf)
        o_ref[0, o] = val
    o_ref[0, 2] = jnp.full((t, t), tbl_ref[(NUM_BUCKETS - 1) * n_heads + h], F32)


def _bias_tiles(rel_table, t):
    n_heads = rel_table.shape[1]
    assert t + 1 >= MAX_DISTANCE
    return pl.pallas_call(
        functools.partial(_bias_kernel, t=t, n_heads=n_heads),
        grid=(n_heads,),
        in_specs=[pl.BlockSpec(memory_space=pltpu.SMEM)],
        out_specs=pl.BlockSpec((1, 3, t, t), lambda h: (h, 0, 0, 0)),
        out_shape=jax.ShapeDtypeStruct((n_heads, 3, t, t), F32),
        compiler_params=_cparams(("arbitrary",)),
        name="bias_tiles",
    )(rel_table.reshape(-1))


def _attn_kernel(lam_ref, q_ref, k_ref, v_ref, bias_ref, g_ref, o_ref, vt_ref, acc_ref, m_ref, l_ref,
                 *, t):
    qi = pl.program_id(2)
    n_kt = v_ref.shape[1] // t

    @pl.when(qi == 0)
    def _():
        for kb in range(n_kt):
            vt_ref[:, kb * t:(kb + 1) * t] = v_ref[0, kb * t:(kb + 1) * t, :].astype(F32).T.astype(BF16)

    q = q_ref[0]
    qs = (q[:, :HEAD_DIM], q[:, HEAD_DIM:])
    dn = (((1,), (1,)), ((), ()))
    acc_ref[...] = jnp.zeros_like(acc_ref)
    m_ref[...] = jnp.full(m_ref.shape, -jnp.inf, F32)
    l_ref[...] = jnp.zeros_like(l_ref)

    def update(kb0, n_tiles):
        start = pl.multiple_of(kb0 * t, t)
        k = k_ref[0, pl.ds(start, n_tiles * t), :]
        vt = vt_ref[:, pl.ds(start, n_tiles * t)]
        bias = [bias_ref[0, jnp.minimum(qi - kb0 - i, 2)] for i in range(n_tiles)]
        for c in range(2):
            s = [lax.dot_general(k[i * t:(i + 1) * t, c * HEAD_DIM:(c + 1) * HEAD_DIM], qs[c], dn,
                                 preferred_element_type=F32) + bias[i]
                 for i in range(n_tiles)]
            m_old = m_ref[c]
            m_new = m_old
            for si in s:
                m_new = jnp.maximum(m_new, jnp.max(si, axis=0, keepdims=True))
            alpha = jnp.exp(m_old - m_new)
            p = [jnp.exp(si - m_new) for si in s]
            l_new = alpha * l_ref[c]
            for pi in p:
                l_new = l_new + jnp.sum(pi, axis=0, keepdims=True)
            pcat = p[0].astype(BF16) if n_tiles == 1 else jnp.concatenate(
                [pi.astype(BF16) for pi in p], axis=0)
            acc_ref[c] = alpha * acc_ref[c] + jnp.dot(vt, pcat, preferred_element_type=F32)
            m_ref[c] = m_new
            l_ref[c] = l_new

    n = qi + 1

    def pair(i, carry):
        update(2 * i, 2)
        return carry

    lax.fori_loop(0, n // 2, pair, 0)

    @pl.when(n % 2 == 1)
    def _():
        update(qi, 1)

    lp = lam_ref[...]
    lam = (jnp.exp(jnp.sum(lp[0:1] * lp[1:2], axis=-1, keepdims=True))
           - jnp.exp(jnp.sum(lp[2:3] * lp[3:4], axis=-1, keepdims=True)) + LAM_INIT)
    ot = acc_ref[0] * (1.0 / l_ref[0]) - acc_ref[1] * (lam / l_ref[1])
    ot = ot * lax.rsqrt(jnp.mean(jnp.square(ot), axis=0, keepdims=True) + LN_EPS)
    o = ot.T * g_ref[...]
    o_ref[0] = (o * (1.0 - LAM_INIT)).astype(o_ref.dtype)


def _diff_attention(proj3, bias, lam_params, subln_g, *, n_heads, k_col, v_col):
    B, S, _ = proj3.shape
    t = ATT_T
    nq = S // t
    w = 2 * HEAD_DIM
    return pl.pallas_call(
        functools.partial(_attn_kernel, t=t),
        grid=(B, n_heads, nq),
        in_specs=[pl.BlockSpec((4, HEAD_DIM), lambda b, h, i: (0, 0)),
                  pl.BlockSpec((1, t, w), lambda b, h, i: (b, i, h)),
                  pl.BlockSpec((1, S, w), lambda b, h, i: (b, 0, k_col + h)),
                  pl.BlockSpec((1, S, w), lambda b, h, i: (b, 0, v_col + h)),
                  pl.BlockSpec((1, 3, t, t), lambda b, h, i: (h, 0, 0, 0)),
                  pl.BlockSpec((1, V_DIM), lambda b, h, i: (0, 0))],
        out_specs=pl.BlockSpec((1, t, V_DIM), lambda b, h, i: (b, i, h)),
        out_shape=jax.ShapeDtypeStruct((B, S, n_heads * V_DIM), BF16),
        scratch_shapes=[pltpu.VMEM((V_DIM, S), BF16), pltpu.VMEM((2, V_DIM, t), F32),
                        pltpu.VMEM((2, 1, t), F32), pltpu.VMEM((2, 1, t), F32)],
        compiler_params=_cparams(("parallel", "parallel", "arbitrary")),
        name="diff_attention",
    )(lam_params, proj3, proj3, proj3, bias, subln_g)


def _conv_kernel(a_ref, g_ref, ap_ref, gp_ref, w_ref, cb_ref, lg_ref, lb_ref, o_ref,
                 hs_ref, cv_ref, *, ts, halo, taps, rc):
    i = pl.program_id(1)
    C = a_ref.shape[-1]
    hs_ref[halo:, :] = a_ref[0].astype(F32) * jax.nn.sigmoid(g_ref[0].astype(F32))
    hp = ap_ref[0].astype(F32) * jax.nn.sigmoid(gp_ref[0].astype(F32))
    hs_ref[:halo, :] = jnp.where(i > 0, hp, 0.0)
    base = halo - (taps - 1)
    for c in range(C // LANES):
        cs = slice(c * LANES, (c + 1) * LANES)
        for r in range(ts // rc):
            acc = jnp.zeros((rc, LANES), F32)
            for tp in range(taps):
                acc = acc + w_ref[tp:tp + 1, cs] * hs_ref[pl.ds(base + tp + r * rc, rc), cs]
            cv_ref[r * rc:(r + 1) * rc, cs] = acc + cb_ref[:, cs]
    h = cv_ref[...]
    mu = jnp.mean(h, axis=-1, keepdims=True)
    var = jnp.mean(jnp.square(h - mu), axis=-1, keepdims=True)
    y = (h - mu) * lax.rsqrt(var + LN_EPS) * lg_ref[...] + lb_ref[...]
    o_ref[0] = (y * jax.nn.sigmoid(y)).astype(o_ref.dtype)


def _conformer_conv(proj3, conv_w, conv_b, ln_g, ln_b, *, a_col, g_col):
    B, S, _ = proj3.shape
    taps, C = conv_w.shape
    ts, halo = CONV_TS, CONV_HALO
    assert halo >= taps - 1
    hb = ts // halo
    prev = lambda col: (lambda b, i: (b, jnp.maximum(i * hb - 1, 0), col))
    vec = pl.BlockSpec((1, C), lambda b, i: (0, 0))
    return pl.pallas_call(
        functools.partial(_conv_kernel, ts=ts, halo=halo, taps=taps, rc=128),
        grid=(B, S // ts),
        in_specs=[pl.BlockSpec((1, ts, C), lambda b, i: (b, i, a_col)),
                  pl.BlockSpec((1, ts, C), lambda b, i: (b, i, g_col)),
                  pl.BlockSpec((1, halo, C), prev(a_col)),
                  pl.BlockSpec((1, halo, C), prev(g_col)),
                  pl.BlockSpec((taps, C), lambda b, i: (0, 0)),
                  vec, vec, vec],
        out_specs=pl.BlockSpec((1, ts, C), lambda b, i: (b, i, 0)),
        out_shape=jax.ShapeDtypeStruct((B, S, C), BF16),
        scratch_shapes=[pltpu.VMEM((halo + ts, C), F32), pltpu.VMEM((ts, C), F32)],
        compiler_params=_cparams(("parallel", "arbitrary")),
        name="conformer_conv",
    )(proj3, proj3, proj3, proj3, conv_w, conv_b, ln_g, ln_b)


def _mem_attn_kernel(q_ref, k_ref, v_ref, o_ref):
    s = lax.dot_general(q_ref[0], k_ref[0], (((1,), (1,)), ((), ())), preferred_element_type=F32)
    m = jnp.max(s, axis=-1, keepdims=True)
    p = jnp.exp(s - m)
    p = p / jnp.sum(p, axis=-1, keepdims=True)
    o_ref[0] = jnp.dot(p.astype(BF16), v_ref[0], preferred_element_type=F32).astype(o_ref.dtype)


def _memory_attention(proj3, kv3, *, n_heads, q_col, hd):
    B, S, _ = proj3.shape
    M = kv3.shape[1]
    tq = MEM_TQ
    return pl.pallas_call(
        _mem_attn_kernel,
        grid=(B, n_heads, S // tq),
        in_specs=[pl.BlockSpec((1, tq, hd), lambda b, h, i: (b, i, q_col + h)),
                  pl.BlockSpec((1, M, hd), lambda b, h, i: (b, 0, h)),
                  pl.BlockSpec((1, M, hd), lambda b, h, i: (b, 0, n_heads + h))],
        out_specs=pl.BlockSpec((1, tq, hd), lambda b, h, i: (b, i, h)),
        out_shape=jax.ShapeDtypeStruct((B, S, n_heads * hd), BF16),
        compiler_params=_cparams(("parallel", "parallel", "arbitrary")),
        name="memory_attention",
    )(proj3, kv3, kv3)


def _oproj_kernel(a_ref, c_ref, m_ref, wa_ref, wc_ref, wm_ref, o_ref):
    acc = jnp.dot(a_ref[...], wa_ref[...].astype(BF16), preferred_element_type=F32)
    acc += jnp.dot(c_ref[...], wc_ref[...].astype(BF16), preferred_element_type=F32)
    acc += jnp.dot(m_ref[...], wm_ref[...].astype(BF16), preferred_element_type=F32)
    o_ref[...] = acc


def _out_projection(a, c, m, w_o, *, tm, tn):
    M, Ka = a.shape
    Kc, Km = c.shape[1], m.shape[1]
    assert Kc == Km and Ka % Kc == 0
    N = w_o.shape[1]
    return pl.pallas_call(
        _oproj_kernel,
        grid=(M // tm, N // tn),
        in_specs=[pl.BlockSpec((tm, Ka), lambda i, j: (i, 0)),
                  pl.BlockSpec((tm, Kc), lambda i, j: (i, 0)),
                  pl.BlockSpec((tm, Km), lambda i, j: (i, 0)),
                  pl.BlockSpec((Ka, tn), lambda i, j: (0, j)),
                  pl.BlockSpec((Kc, tn), lambda i, j: (Ka // Kc, j)),
                  pl.BlockSpec((Km, tn), lambda i, j: (Ka // Kc + 1, j))],
        out_specs=pl.BlockSpec((tm, tn), lambda i, j: (i, j)),
        out_shape=jax.ShapeDtypeStruct((M, N), F32),
        compiler_params=_cparams(("parallel", "arbitrary")),
        name="out_projection",
    )(a, c, m, w_o, w_o, w_o)


def _pack_pairs(lo, hi):
    lo_b = lax.bitcast_convert_type(lo.astype(BF16).astype(F32), U32)
    hi_b = lax.bitcast_convert_type(hi.astype(BF16).astype(F32), U32)
    return (lo_b >> 16) | (hi_b & jnp.uint32(0xFFFF0000))


def _unpack_lo(u):
    return lax.bitcast_convert_type(u << 16, F32)


def _unpack_hi(u):
    return lax.bitcast_convert_type(u & jnp.uint32(0xFFFF0000), F32)


def _layer_norm(y, g, b):
    mu = jnp.mean(y, axis=-1, keepdims=True)
    var = jnp.mean(jnp.square(y - mu), axis=-1, keepdims=True)
    return (y - mu) * lax.rsqrt(var + LN_EPS) * g + b


def _ln_router_kernel(x_ref, mix_ref, g_ref, b_ref, wr_ref, br_ref,
                      x1_ref, xp_ref, topi_ref, gate_ref):
    x1 = _layer_norm(DEEPNORM_ALPHA * x_ref[...] + mix_ref[...], g_ref[...], b_ref[...])
    x1_ref[...] = x1
    half = x1.shape[1] // 2
    xp_ref[...] = _pack_pairs(x1[:, :half], x1[:, half:])
    logits = jnp.dot(x1, wr_ref[...], preferred_element_type=F32,
                     precision=lax.Precision.HIGHEST) + br_ref[...]
    tm, ne = logits.shape
    lane = lax.broadcasted_iota(I32, (tm, ne), 1)
    kk = lax.broadcasted_iota(I32, (tm, TOP_K), 1)
    idx = jnp.zeros((tm, TOP_K), I32)
    val = jnp.zeros((tm, TOP_K), F32)
    cur = logits
    for k in range(TOP_K):
        mx = jnp.max(cur, axis=-1, keepdims=True)
        ix = jnp.min(jnp.where(cur == mx, lane, ne), axis=-1, keepdims=True)
        idx = jnp.where(kk == k, ix, idx)
        val = jnp.where(kk == k, mx, val)
        cur = jnp.where(lane == ix, -jnp.inf, cur)
    e = jnp.exp(val - jnp.max(val, axis=-1, keepdims=True))
    topi_ref[...] = idx
    gate_ref[...] = e / jnp.sum(e, axis=-1, keepdims=True)


def _ln_router(x, mix, g, b, w_router, b_router, *, tm):
    T, D = x.shape
    ne = w_router.shape[1]
    row = pl.BlockSpec((tm, D), lambda i: (i, 0))
    vec = pl.BlockSpec((1, D), lambda i: (0, 0))
    return pl.pallas_call(
        _ln_router_kernel,
        grid=(T // tm,),
        in_specs=[row, row, vec, vec,
                  pl.BlockSpec((D, ne), lambda i: (0, 0)),
                  pl.BlockSpec((1, ne), lambda i: (0, 0))],
        out_specs=[row,
                   pl.BlockSpec((tm, D // 2), lambda i: (i, 0)),
                   pl.BlockSpec((tm, TOP_K), lambda i: (i, 0)),
                   pl.BlockSpec((tm, TOP_K), lambda i: (i, 0))],
        out_shape=[jax.ShapeDtypeStruct((T, D), F32),
                   jax.ShapeDtypeStruct((T, D // 2), U32),
                   jax.ShapeDtypeStruct((T, TOP_K), I32),
                   jax.ShapeDtypeStruct((T, TOP_K), F32)],
        compiler_params=_cparams(("parallel",)),
        name="ln_router",
    )(x, mix, g, b, w_router, b_router)


def _rank_kernel(topi_ref, dest_ref, cnt_ref, ps_ref, rank_ref, carry_ref, *, tt, pad):
    ph = pl.program_id(0)
    i = pl.program_id(1)
    lane = lax.broadcasted_iota(I32, (tt, LANES), 1)
    kk = lax.broadcasted_iota(I32, (tt, TOP_K), 1)
    ti = topi_ref[...]
    sel = [lane == ti[:, k:k + 1] for k in range(TOP_K)]
    rows = pl.ds(pl.multiple_of(i * tt, tt), tt)

    @pl.when(ph == 0)
    def _():
        @pl.when(i == 0)
        def _():
            carry_ref[...] = jnp.zeros_like(carry_ref)

        oh = jnp.zeros((tt, LANES), F32)
        for k in range(TOP_K):
            oh = oh + sel[k].astype(F32)
        r = lax.broadcasted_iota(I32, (tt, tt), 0)
        c = lax.broadcasted_iota(I32, (tt, tt), 1)
        lower = (r > c).astype(BF16)
        before = jnp.dot(lower, oh.astype(BF16), preferred_element_type=F32) + carry_ref[0:1, :]
        rank = jnp.zeros((tt, TOP_K), F32)
        for k in range(TOP_K):
            rk = jnp.sum(jnp.where(sel[k], before, 0.0), axis=-1, keepdims=True)
            rank = jnp.where(kk == k, rk, rank)
        rank_ref[rows, :] = rank
        carry_ref[...] = carry_ref[...] + jnp.sum(oh, axis=0, keepdims=True)

    @pl.when(ph == 1)
    def _():
        cnt = carry_ref[...].astype(I32)
        padded = (cnt + (pad - 1)) & jnp.int32(-pad)
        l8 = lax.broadcasted_iota(I32, (8, LANES), 1)
        scan = padded
        for sh in (1, 2, 4, 8, 16, 32, 64):
            scan = scan + jnp.where(l8 >= sh, pltpu.roll(scan, sh, axis=1), 0)
        starts = scan - padded
        cnt_ref[...] = cnt
        ps_ref[...] = starts
        st = starts[0:1, :].astype(F32)
        rank = rank_ref[rows, :]
        dest = jnp.zeros((tt, TOP_K), F32)
        for k in range(TOP_K):
            base = jnp.sum(jnp.where(sel[k], st, 0.0), axis=-1, keepdims=True)
            dest = jnp.where(kk == k, base, dest)
        dest_ref[...] = (dest + rank).astype(I32)


def _rank_assignments(topi):
    T = topi.shape[0]
    tt = RANK_TT
    return pl.pallas_call(
        functools.partial(_rank_kernel, tt=tt, pad=MOE_PAD),
        grid=(2, T // tt),
        in_specs=[pl.BlockSpec((tt, TOP_K), lambda p, i: (i, 0))],
        out_specs=[pl.BlockSpec((tt, TOP_K), lambda p, i: (i * p, 0)),
                   pl.BlockSpec((8, LANES), lambda p, i: (0, 0)),
                   pl.BlockSpec((8, LANES), lambda p, i: (0, 0))],
        out_shape=[jax.ShapeDtypeStruct((T, TOP_K), I32),
                   jax.ShapeDtypeStruct((8, LANES), I32),
                   jax.ShapeDtypeStruct((8, LANES), I32)],
        scratch_shapes=[pltpu.VMEM((T, TOP_K), F32), pltpu.VMEM((8, LANES), F32)],
        compiler_params=_cparams(("arbitrary", "arbitrary")),
        name="rank_assignments",
    )(topi)


def _dispatch_kernel(fillc_ref, misc_ref, dest_ref, xp_ref, xs_ref, zero_ref, fill_sem, row_sem, *, ta):
    i = pl.program_id(0)
    tt = xp_ref.shape[0]

    @pl.when(i == 0)
    def _():
        def fill(e):
            row = pl.multiple_of(fillc_ref[e] * MOE_PAD, MOE_PAD)
            return pltpu.make_async_copy(zero_ref, xs_ref.at[pl.ds(row, MOE_PAD), :], fill_sem)

        def each_group(fn):
            for e in range(N_EXPERTS):
                pl.when(fillc_ref[e] >= 0)(functools.partial(fn, e))

        zero_ref[...] = jnp.zeros_like(zero_ref)
        each_group(lambda e: fill(e).start())
        _fill_tail(misc_ref, zero_ref, xs_ref, fill_sem)
        each_group(lambda e: fill(e).wait())

    def issue(t, carry):
        for k in range(TOP_K):
            pltpu.make_async_copy(xp_ref.at[pl.ds(t, 1), :],
                                  xs_ref.at[pl.ds(dest_ref[0, 0, t * TOP_K + k], 1), :],
                                  row_sem).start()
        return carry

    lax.fori_loop(0, tt, issue, 0, unroll=2)
    for _ in range(ta // tt):
        pltpu.make_async_copy(xp_ref, xs_ref.at[pl.ds(0, tt), :], row_sem).wait()


def _dispatch(dest, xp, fillc, misc, n_rows):
    T, W = xp.shape
    ta = DISPATCH_TA
    na = dest.size // ta
    assert n_rows % MOE_PAD == 0 and ta % TOP_K == 0
    return pl.pallas_call(
        functools.partial(_dispatch_kernel, ta=ta),
        grid_spec=pltpu.PrefetchScalarGridSpec(
            num_scalar_prefetch=2,
            grid=(na,),
            in_specs=[pl.BlockSpec((1, 1, ta), lambda i, fc, ms: (i, 0, 0), memory_space=pltpu.SMEM),
                      pl.BlockSpec((ta // TOP_K, W), lambda i, fc, ms: (i, 0))],
            out_specs=pl.BlockSpec(memory_space=pl.ANY),
            scratch_shapes=[pltpu.VMEM((MOE_PAD, W), U32),
                            pltpu.SemaphoreType.DMA(()), pltpu.SemaphoreType.DMA(())]),
        out_shape=jax.ShapeDtypeStruct((n_rows, W), U32),
        compiler_params=_cparams(("arbitrary",)),
        name="dispatch_rows",
    )(fillc, misc, dest.reshape(na, 1, ta), xp)


class _TileWriter:
    def __init__(self, wr_ref, wc_ref, stage_ref, out_ref, sem, tn):
        self.wr_ref, self.wc_ref = wr_ref, wc_ref
        self.stage_ref, self.out_ref, self.sem, self.tn = stage_ref, out_ref, sem, tn
        n_chunks = stage_ref.shape[1] // MOE_PAD
        self.sizes = [1 << b for b in reversed(range(n_chunks.bit_length()))]

    def _copy(self, slot, w, j, size):
        n = self.wc_ref[w]
        first = pl.multiple_of((n & -(2 * size)) * MOE_PAD, MOE_PAD)
        row = pl.multiple_of(self.wr_ref[w] + first, MOE_PAD)
        col = pl.multiple_of(j * self.tn, self.tn)
        return pltpu.make_async_copy(
            self.stage_ref.at[slot, pl.ds(first, size * MOE_PAD), :],
            self.out_ref.at[pl.ds(row, size * MOE_PAD), pl.ds(col, self.tn)],
            self.sem.at[slot])

    def _each_group(self, w, fn):
        for size in self.sizes:
            pl.when((self.wc_ref[w] & size) != 0)(functools.partial(fn, size))

    def start(self, slot, w, j):
        self._each_group(w, lambda size: self._copy(slot, w, j, size).start())

    def wait(self, slot, w, j):
        self._each_group(w, lambda size: self._copy(slot, w, j, size).wait())

    def retire_older(self, w, j, nj):
        step = w * nj + j

        @pl.when(step >= 2)
        def _():
            wrap = j < 2
            self.wait(step % 2, jnp.where(wrap, w - 1, w), jnp.where(wrap, j + nj - 2, j - 2))

    def drain(self, w, j, nw, nj):
        step = w * nj + j

        @pl.when(step == nw * nj - 1)
        def _():
            self.wait((step + 1) % 2, w, j - 1)
            self.wait(step % 2, w, j)


def _fill_tail(misc_ref, zero_ref, out_ref, sem):
    n_total = out_ref.shape[0] // MOE_PAD

    def fill(c):
        row = pl.multiple_of(c * MOE_PAD, MOE_PAD)
        return pltpu.make_async_copy(zero_ref, out_ref.at[pl.ds(row, MOE_PAD), :], sem)

    def start(c, carry):
        fill(c).start()
        return carry

    def wait(c, carry):
        fill(c).wait()
        return carry

    lax.fori_loop(misc_ref[0], n_total, start, 0)
    lax.fori_loop(misc_ref[0], n_total, wait, 0)


def _moe_up_kernel(we_ref, wr_ref, wc_ref, misc_ref, xs_ref, wg_ref, wu_ref, b_ref,
                   h_ref, xb_ref, wcat_ref, stage_ref, zero_ref, sem, fill_sem):
    w = pl.program_id(0)
    j = pl.program_id(1)
    nw = pl.num_programs(0)
    nj = pl.num_programs(1)
    slot = (w * nj + j) % 2
    writer = _TileWriter(wr_ref, wc_ref, stage_ref, h_ref, sem, stage_ref.shape[2])

    @pl.when((w == 0) & (j == 0))
    def _():
        zero_ref[...] = jnp.zeros_like(zero_ref)
        _fill_tail(misc_ref, zero_ref, h_ref, fill_sem)

    writer.retire_older(w, j, nj)

    @pl.when(wc_ref[w] > 0)
    def _():
        half = xs_ref.shape[1]

        @pl.when(j == 0)
        def _():
            u = xs_ref[...]
            xb_ref[:, :half] = _unpack_lo(u).astype(BF16)
            xb_ref[:, half:] = _unpack_hi(u).astype(BF16)

        tn = wg_ref.shape[2]
        wcat_ref[:, :tn] = wg_ref[0].astype(BF16)
        wcat_ref[:, tn:] = wu_ref[0].astype(BF16)
        col = pl.multiple_of(j * tn, tn)
        bg = b_ref[0, :, pl.ds(col, tn)]
        bu = b_ref[0, :, pl.ds(pl.multiple_of(b_ref.shape[2] // 2 + col, tn), tn)]
        def rows_block(rows):
            r = jnp.dot(xb_ref[rows, :], wcat_ref[...], preferred_element_type=F32)
            gate = r[:, :tn] + bg
            up = r[:, tn:] + bu
            gate = jnp.minimum(gate, SWIGLU_LIMIT)
            up = jnp.clip(up, -SWIGLU_LIMIT, SWIGLU_LIMIT)
            act = (up + 1.0) * (gate * jax.nn.sigmoid(SWIGLU_ALPHA * gate))
            stage_ref[slot, rows, :] = act.astype(stage_ref.dtype)

        def all_rows(n_rows):
            mc = n_rows // MOE_UP_ROW_CHUNKS
            for mi in range(MOE_UP_ROW_CHUNKS):
                rows_block(slice(mi * mc, (mi + 1) * mc))

        all_rows(xb_ref.shape[0])
        writer.start(slot, w, j)

    writer.drain(w, j, nw, nj)


def _moe_up(xs, w_gu, b_gu, we, wr, wc, misc, *, d_ff):
    R, half = xs.shape
    E, D, _ = w_gu.shape
    tm, tn = MOE_TM, MOE_TN_UP
    nj = d_ff // tn
    nw = we.shape[0]
    assert nj >= 2 and R % MOE_PAD == 0 and tm % (16 * MOE_UP_ROW_CHUNKS) == 0
    jeff = lambda w, j, wc: jnp.where(wc[w] > 0, j, nj - 1)
    return pl.pallas_call(
        _moe_up_kernel,
        grid_spec=pltpu.PrefetchScalarGridSpec(
            num_scalar_prefetch=4,
            grid=(misc[1], nj),
            in_specs=[pl.BlockSpec((pl.Element(tm), pl.Element(half)),
                                   lambda w, j, we, wr, wc, ms: (pl.multiple_of(wr[w], MOE_PAD), 0)),
                      pl.BlockSpec((1, D, tn), lambda w, j, we, wr, wc, ms: (we[w], 0, jeff(w, j, wc))),
                      pl.BlockSpec((1, D, tn), lambda w, j, we, wr, wc, ms: (we[w], 0, nj + jeff(w, j, wc))),
                      pl.BlockSpec((1, 1, 2 * nj * tn), lambda w, j, we, wr, wc, ms: (we[w], 0, 0))],
            out_specs=pl.BlockSpec(memory_space=pl.ANY),
            scratch_shapes=[pltpu.VMEM((tm, 2 * half), BF16),
                            pltpu.VMEM((D, 2 * tn), BF16),
                            pltpu.VMEM((2, tm, tn), BF16),
                            pltpu.VMEM((MOE_PAD, d_ff), BF16),
                            pltpu.SemaphoreType.DMA((2,)), pltpu.SemaphoreType.DMA(())]),
        out_shape=jax.ShapeDtypeStruct((R, d_ff), BF16),
        compiler_params=_cparams(("arbitrary", "arbitrary")),
        name="moe_up",
    )(we, wr, wc, misc, xs, w_gu, w_gu, b_gu)


def _moe_down_kernel(we_ref, wr_ref, wc_ref, misc_ref, h_ref, w_ref, b_ref,
                     o_ref, stage_ref, zero_ref, sem, fill_sem):
    w = pl.program_id(0)
    j = pl.program_id(1)
    nw = pl.num_programs(0)
    nj = pl.num_programs(1)
    slot = (w * nj + j) % 2
    writer = _TileWriter(wr_ref, wc_ref, stage_ref, o_ref, sem, stage_ref.shape[2])

    @pl.when((w == 0) & (j == 0))
    def _():
        zero_ref[...] = jnp.zeros_like(zero_ref)
        _fill_tail(misc_ref, zero_ref, o_ref, fill_sem)

    writer.retire_older(w, j, nj)

    @pl.when(wc_ref[w] > 0)
    def _():
        tn = stage_ref.shape[2]
        bias = b_ref[0, :, pl.ds(pl.multiple_of(j * 2 * tn, 2 * tn), 2 * tn)]

        def rows_block(rows):
            r = jnp.dot(h_ref[rows, :], w_ref[0].astype(BF16), preferred_element_type=F32) + bias
            stage_ref[slot, rows, :] = _pack_pairs(r[:, :tn], r[:, tn:])

        tm, main = h_ref.shape[0], min(MOE_MAIN_ROWS, h_ref.shape[0])
        if tm > main:
            fits = wc_ref[w] * MOE_PAD <= main
            pl.when(fits)(functools.partial(rows_block, slice(0, main)))
            pl.when(jnp.logical_not(fits))(functools.partial(rows_block, slice(0, tm)))
        else:
            rows_block(slice(0, tm))
        writer.start(slot, w, j)

    writer.drain(w, j, nw, nj)


def _moe_down(h, w_down, b_down, we, wr, wc, misc):
    R, d_ff = h.shape
    E, _, D = w_down.shape
    tm, tn = MOE_TM, MOE_TN_DOWN
    half = D // 2
    nj = half // tn
    nw = we.shape[0]
    assert nj >= 2 and R % MOE_PAD == 0
    jeff = lambda w, j, wc: jnp.where(wc[w] > 0, j, nj - 1)
    return pl.pallas_call(
        _moe_down_kernel,
        grid_spec=pltpu.PrefetchScalarGridSpec(
            num_scalar_prefetch=4,
            grid=(misc[1], nj),
            in_specs=[pl.BlockSpec((pl.Element(tm), pl.Element(d_ff)),
                                   lambda w, j, we, wr, wc, ms: (pl.multiple_of(wr[w], MOE_PAD), 0)),
                      pl.BlockSpec((1, d_ff, 2 * tn), lambda w, j, we, wr, wc, ms: (we[w], 0, jeff(w, j, wc))),
                      pl.BlockSpec((1, 1, 2 * nj * tn), lambda w, j, we, wr, wc, ms: (we[w], 0, 0))],
            out_specs=pl.BlockSpec(memory_space=pl.ANY),
            scratch_shapes=[pltpu.VMEM((2, tm, tn), U32),
                            pltpu.VMEM((MOE_PAD, half), U32),
                            pltpu.SemaphoreType.DMA((2,)), pltpu.SemaphoreType.DMA(())]),
        out_shape=jax.ShapeDtypeStruct((R, half), U32),
        compiler_params=_cparams(("arbitrary", "arbitrary")),
        name="moe_down",
    )(we, wr, wc, misc, h, w_down, b_down)


def _combine_kernel(dest_ref, dnext_ref, eo_ref, gate_ref, x1_ref, g_ref, b_ref, o_ref, buf_ref, sem,
                    *, tc):
    i = pl.program_id(0)
    n = tc * TOP_K
    slot = i % 2

    def gather(d_ref, slot):
        def issue(t, carry):
            for k in range(TOP_K):
                pltpu.make_async_copy(eo_ref.at[pl.ds(d_ref[0, 0, t * TOP_K + k], 1), :],
                                      buf_ref.at[slot, pl.ds(k * tc + t, 1), :],
                                      sem.at[slot]).start()
            return carry

        lax.fori_loop(0, tc, issue, 0, unroll=2)

    @pl.when(i == 0)
    def _():
        gather(dest_ref, 0)

    @pl.when(i + 1 < pl.num_programs(0))
    def _():
        gather(dnext_ref, 1 - slot)

    pltpu.make_async_copy(eo_ref.at[pl.ds(0, n), :], buf_ref.at[slot], sem.at[slot]).wait()

    gates = gate_ref[...]
    half = buf_ref.shape[2]
    cw = 2 * LANES
    row_sum = jnp.zeros((tc, 1), F32)
    for c in range(half // cw):
        tn = MOE_TN_DOWN
        u_cols = slice(c * cw, (c + 1) * cw)
        first = (c * cw // tn) * 2 * tn + c * cw % tn
        lo_cols = slice(first, first + cw)
        hi_cols = slice(first + tn, first + tn + cw)
        lo = DEEPNORM_ALPHA * x1_ref[:, lo_cols]
        hi = DEEPNORM_ALPHA * x1_ref[:, hi_cols]
        for k in range(TOP_K):
            u = buf_ref[slot, k * tc:(k + 1) * tc, u_cols]
            gk = gates[:, k:k + 1]
            lo = lo + gk * _unpack_lo(u)
            hi = hi + gk * _unpack_hi(u)
        o_ref[:, lo_cols] = lo
        o_ref[:, hi_cols] = hi
        row_sum = row_sum + jnp.sum(lo, axis=-1, keepdims=True) + jnp.sum(hi, axis=-1, keepdims=True)
    inv_d = 1.0 / (2 * half)
    mu = row_sum * inv_d
    sq_sum = jnp.zeros((tc, 1), F32)
    for c in range(2 * half // cw):
        d = o_ref[:, c * cw:(c + 1) * cw] - mu
        sq_sum = sq_sum + jnp.sum(d * d, axis=-1, keepdims=True)
    rstd = lax.rsqrt(sq_sum * inv_d + LN_EPS)
    for c in range(2 * half // cw):
        cols = slice(c * cw, (c + 1) * cw)
        o_ref[:, cols] = (o_ref[:, cols] - mu) * rstd * g_ref[:, cols] + b_ref[:, cols]


def _combine(dest, eo, gates, x1, g, b):
    T, D = x1.shape
    tc = COMBINE_TC
    n = tc * TOP_K
    nt = T // tc
    half = eo.shape[1]
    row = pl.BlockSpec((tc, D), lambda i: (i, 0))
    vec = pl.BlockSpec((1, D), lambda i: (0, 0))
    dest3 = dest.reshape(nt, 1, n)
    return pl.pallas_call(
        functools.partial(_combine_kernel, tc=tc),
        grid=(nt,),
        in_specs=[pl.BlockSpec((1, 1, n), lambda i: (i, 0, 0), memory_space=pltpu.SMEM),
                  pl.BlockSpec((1, 1, n), lambda i: (jnp.minimum(i + 1, nt - 1), 0, 0),
                               memory_space=pltpu.SMEM),
                  pl.BlockSpec(memory_space=pl.ANY),
                  pl.BlockSpec((tc, TOP_K), lambda i: (i, 0)),
                  row, vec, vec],
        out_specs=row,
        out_shape=jax.ShapeDtypeStruct((T, D), F32),
        scratch_shapes=[pltpu.VMEM((2, n, half), U32), pltpu.SemaphoreType.DMA((2,))],
        compiler_params=_cparams(("arbitrary",)),
        name="combine_rows",
    )(dest3, dest3, eo, gates, x1, g, b)


def _work_items(cnt, starts, n_items):
    chunks_per_pass = MOE_TM // MOE_PAD
    gchunks = (cnt + MOE_PAD - 1) // MOE_PAD
    npass = (gchunks + chunks_per_pass - 1) // chunks_per_pass
    cum = jnp.cumsum(npass)
    total = cum[-1]
    w = jnp.arange(n_items, dtype=I32)
    e = jnp.minimum(jnp.sum((cum[None, :] <= w[:, None]).astype(I32), axis=1), N_EXPERTS - 1)
    p = w - (cum - npass)[e]
    row = starts[e] + p * MOE_TM
    nch = jnp.clip(gchunks[e] - p * chunks_per_pass, 0, chunks_per_pass)
    valid = w < total
    last = jnp.maximum(total - 1, 0)
    e = jnp.where(valid, e, e[last]).astype(I32)
    row = jnp.where(valid, row, row[last]).astype(I32)
    nch = jnp.where(valid, nch, 0).astype(I32)
    misc = jnp.stack([jnp.sum(gchunks), total]).astype(I32)
    fillc = jnp.where(cnt % MOE_PAD != 0, starts // MOE_PAD + gchunks - 1, -1).astype(I32)
    return e, row, nch, misc, fillc


def kernel(x, mem, rel_table, w_in, w_mem_kv, w_o, lambda_q1, lambda_k1, lambda_q2, lambda_k2, subln_g, conv_w, conv_b, conv_ln_g, conv_ln_b, ln1_g, ln1_b, w_router, b_router, w_gate_up, b_gate_up, w_down, b_down, ln2_g, ln2_b):
    B, S, D = x.shape
    T = B * S
    n_heads = rel_table.shape[1]
    qk_w = n_heads * 2 * HEAD_DIM
    v_w = n_heads * V_DIM
    C = conv_w.shape[-1]
    mem_w = w_mem_kv.shape[-1] // 2
    mem_heads = 4
    mem_hd = mem_w // mem_heads
    in_cols = w_in.shape[-1]
    assert in_cols == 2 * qk_w + v_w + 2 * C + mem_w
    d_ff = w_down.shape[2]
    assert DEPTH == 1 and w_in.shape[0] == 1

    col_scale = jnp.concatenate([
        jnp.full((qk_w,), HEAD_DIM ** -0.5, F32),
        jnp.ones((in_cols - qk_w - mem_w,), F32),
        jnp.full((mem_w,), mem_hd ** -0.5, F32)]).reshape(1, in_cols)

    xt = x.reshape(T, D)
    proj = _proj_matmul(xt.astype(BF16), w_in[0], col_scale, tm=1024, tn=512, out_dtype=BF16)
    proj3 = proj.reshape(B, S, in_cols)

    bias = _bias_tiles(rel_table, ATT_T)
    lam_params = jnp.concatenate([lambda_q1, lambda_k1, lambda_q2, lambda_k2], axis=0).astype(F32)
    a_out = _diff_attention(proj3, bias, lam_params, subln_g, n_heads=n_heads,
                            k_col=qk_w // (2 * HEAD_DIM), v_col=2 * qk_w // V_DIM)

    conv_col = (2 * qk_w + v_w) // C
    c_out = _conformer_conv(proj3, conv_w[0, :, 0, :], conv_b, conv_ln_g, conv_ln_b,
                            a_col=conv_col, g_col=conv_col + 1)

    n_mem = mem.shape[1]
    kv = _proj_matmul(mem.reshape(B * n_mem, D).astype(BF16), w_mem_kv[0],
                      jnp.ones((1, 2 * mem_w), F32), tm=B * n_mem, tn=512, out_dtype=BF16)
    m_out = _memory_attention(proj3, kv.reshape(B, n_mem, 2 * mem_w), n_heads=mem_heads,
                              q_col=(in_cols - mem_w) // mem_hd, hd=mem_hd)

    mix = _out_projection(a_out.reshape(T, v_w), c_out.reshape(T, C), m_out.reshape(T, mem_w),
                          w_o[0], tm=1024, tn=512)

    x1, xp, topi, gates = _ln_router(xt, mix, ln1_g, ln1_b, w_router[0],
                                     b_router.reshape(1, -1), tm=256)

    dest, cnt8, starts8 = _rank_assignments(topi)
    cnt, starts = cnt8[0, :N_EXPERTS], starts8[0, :N_EXPERTS]

    n_assign = T * TOP_K
    n_items = n_assign // MOE_TM + N_EXPERTS
    max_rows = n_assign + N_EXPERTS * (MOE_PAD - 1) + MOE_TM
    n_rows = -(-max_rows // MOE_PAD) * MOE_PAD
    we, wr, wc, misc, fillc = _work_items(cnt, starts, n_items)

    xs = _dispatch(dest, xp, fillc, misc, n_rows)
    h = _moe_up(xs, w_gate_up[0], b_gate_up[0].reshape(N_EXPERTS, 1, -1), we, wr, wc, misc, d_ff=d_ff)
    eo = _moe_down(h, w_down[0], b_down[0].reshape(N_EXPERTS, 1, -1), we, wr, wc, misc)
    out = _combine(dest, eo, gates, x1, ln2_g, ln2_b)
    return out.reshape(B, S, D)
```

```python
import functools
import math

import jax
import jax.numpy as jnp
from jax import lax
from jax.experimental import pallas as pl
from jax.experimental.pallas import tpu as pltpu

F32 = jnp.float32
BF16 = jnp.bfloat16
I32 = jnp.int32
U32 = jnp.uint32

HEAD_DIM = 128
V_DIM = 2 * HEAD_DIM
CONV_TAPS = 31
NUM_BUCKETS = 32
MAX_EXACT = NUM_BUCKETS // 2
MAX_DISTANCE = 128
N_EXPERTS = 32
TOP_K = 4
SWIGLU_LIMIT = 7.0
SWIGLU_ALPHA = 1.702
LN_EPS = 1e-5
DEPTH = 1
DEEPNORM_ALPHA = (2 * DEPTH) ** 0.25
LAM_INIT = 0.8 - 0.6 * math.exp(-0.3 * 0)

LANES = 128
V7X_VMEM_BYTES = 64 * 1024 * 1024
VMEM_LIMIT = 58 * 1024 * 1024

ATT_T = 256
CONV_TS = 256
CONV_HALO = 32
MEM_TQ = 512
MOE_TM = 1088
MOE_MAIN_ROWS = 1024
MOE_PAD = 64
MOE_TN_UP = 256
MOE_UP_ROW_CHUNKS = 2
MOE_TN_DOWN = 512
RANK_TT = 512
DISPATCH_TA = 1024
COMBINE_TC = 128


def _cparams(sem, vmem=VMEM_LIMIT):
    return pltpu.CompilerParams(dimension_semantics=sem, vmem_limit_bytes=vmem)


def _proj_kernel(a_ref, w_ref, cs_ref, o_ref):
    acc = jnp.dot(a_ref[...], w_ref[...].astype(BF16), preferred_element_type=F32)
    o_ref[...] = (acc * cs_ref[...]).astype(o_ref.dtype)


def _proj_matmul(a, w, col_scale, *, tm, tn, out_dtype):
    M, K = a.shape
    N = w.shape[1]
    return pl.pallas_call(
        _proj_kernel,
        grid=(M // tm, N // tn),
        in_specs=[pl.BlockSpec((tm, K), lambda i, j: (i, 0)),
                  pl.BlockSpec((K, tn), lambda i, j: (0, j)),
                  pl.BlockSpec((1, tn), lambda i, j: (0, j))],
        out_specs=pl.BlockSpec((tm, tn), lambda i, j: (i, j)),
        out_shape=jax.ShapeDtypeStruct((M, N), out_dtype),
        compiler_params=_cparams(("parallel", "arbitrary")),
        name="proj_matmul",
    )(a, w, col_scale)


def _bias_kernel(tbl_ref, o_ref, *, t, n_heads):
    h = pl.program_id(0)
    j = lax.broadcasted_iota(I32, (t, t), 0)
    i = lax.broadcasted_iota(I32, (t, t), 1)
    for o in range(2):
        dist = o * t + i - j
        n = jnp.maximum(dist, 0)
        nf = jnp.maximum(n, 1).astype(F32)
        large = MAX_EXACT + (jnp.log(nf / MAX_EXACT) / math.log(MAX_DISTANCE / MAX_EXACT)
                             * (NUM_BUCKETS - MAX_EXACT)).astype(I32)
        large = jnp.minimum(large, NUM_BUCKETS - 1)
        bucket = jnp.where(n < MAX_EXACT, n, large)
        val = jnp.zeros((t, t), F32)
        for b in range(NUM_BUCKETS):
            val = jnp.where(bucket == b, tbl_ref[b * n_heads + h], val)
        if o == 0:
            val = jnp.where(dist >= 0, val, -jnp.inf)
        o_ref[0, o] = val
    o_ref[0, 2] = jnp.full((t, t), tbl_ref[(NUM_BUCKETS - 1) * n_heads + h], F32)


def _bias_tiles(rel_table, t):
    n_heads = rel_table.shape[1]
    assert t + 1 >= MAX_DISTANCE
    return pl.pallas_call(
        functools.partial(_bias_kernel, t=t, n_heads=n_heads),
        grid=(n_heads,),
        in_specs=[pl.BlockSpec(memory_space=pltpu.SMEM)],
        out_specs=pl.BlockSpec((1, 3, t, t), lambda h: (h, 0, 0, 0)),
        out_shape=jax.ShapeDtypeStruct((n_heads, 3, t, t), F32),
        compiler_params=_cparams(("arbitrary",)),
        name="bias_tiles",
    )(rel_table.reshape(-1))


def _attn_kernel(lam_ref, q_ref, k_ref, v_ref, bias_ref, g_ref, o_ref, vt_ref, acc_ref, m_ref, l_ref,
                 *, t):
    qi = pl.program_id(2)
    n_kt = v_ref.shape[1] // t

    @pl.when(qi == 0)
    def _():
        for kb in range(n_kt):
            vt_ref[:, kb * t:(kb + 1) * t] = v_ref[0, kb * t:(kb + 1) * t, :].astype(F32).T.astype(BF16)

    q = q_ref[0]
    qs = (q[:, :HEAD_DIM], q[:, HEAD_DIM:])
    dn = (((1,), (1,)), ((), ()))
    acc_ref[...] = jnp.zeros_like(acc_ref)
    m_ref[...] = jnp.full(m_ref.shape, -jnp.inf, F32)
    l_ref[...] = jnp.zeros_like(l_ref)

    def update(kb0, n_tiles):
        start = pl.multiple_of(kb0 * t, t)
        k = k_ref[0, pl.ds(start, n_tiles * t), :]
        vt = vt_ref[:, pl.ds(start, n_tiles * t)]
        bias = [bias_ref[0, jnp.minimum(qi - kb0 - i, 2)] for i in range(n_tiles)]
        for c in range(2):
            s = [lax.dot_general(k[i * t:(i + 1) * t, c * HEAD_DIM:(c + 1) * HEAD_DIM], qs[c], dn,
                                 preferred_element_type=F32) + bias[i]
                 for i in range(n_tiles)]
            m_old = m_ref[c]
            m_new = m_old
            for si in s:
                m_new = jnp.maximum(m_new, jnp.max(si, axis=0, keepdims=True))
            alpha = jnp.exp(m_old - m_new)
            p = [jnp.exp(si - m_new) for si in s]
            l_new = alpha * l_ref[c]
            for pi in p:
                l_new = l_new + jnp.sum(pi, axis=0, keepdims=True)
            pcat = p[0].astype(BF16) if n_tiles == 1 else jnp.concatenate(
                [pi.astype(BF16) for pi in p], axis=0)
            acc_ref[c] = alpha * acc_ref[c] + jnp.dot(vt, pcat, preferred_element_type=F32)
            m_ref[c] = m_new
            l_ref[c] = l_new

    n = qi + 1

    def pair(i, carry):
        update(2 * i, 2)
        return carry

    lax.fori_loop(0, n // 2, pair, 0)

    @pl.when(n % 2 == 1)
    def _():
        update(qi, 1)

    lp = lam_ref[...]
    lam = (jnp.exp(jnp.sum(lp[0:1] * lp[1:2], axis=-1, keepdims=True))
           - jnp.exp(jnp.sum(lp[2:3] * lp[3:4], axis=-1, keepdims=True)) + LAM_INIT)
    ot = acc_ref[0] * (1.0 / l_ref[0]) - acc_ref[1] * (lam / l_ref[1])
    ot = ot * lax.rsqrt(jnp.mean(jnp.square(ot), axis=0, keepdims=True) + LN_EPS)
    o = ot.T * g_ref[...]
    o_ref[0] = (o * (1.0 - LAM_INIT)).astype(o_ref.dtype)


def _diff_attention(proj3, bias, lam_params, subln_g, *, n_heads, k_col, v_col):
    B, S, _ = proj3.shape
    t = ATT_T
    nq = S // t
    w = 2 * HEAD_DIM
    return pl.pallas_call(
        functools.partial(_attn_kernel, t=t),
        grid=(B, n_heads, nq),
        in_specs=[pl.BlockSpec((4, HEAD_DIM), lambda b, h, i: (0, 0)),
                  pl.BlockSpec((1, t, w), lambda b, h, i: (b, i, h)),
                  pl.BlockSpec((1, S, w), lambda b, h, i: (b, 0, k_col + h)),
                  pl.BlockSpec((1, S, w), lambda b, h, i: (b, 0, v_col + h)),
                  pl.BlockSpec((1, 3, t, t), lambda b, h, i: (h, 0, 0, 0)),
                  pl.BlockSpec((1, V_DIM), lambda b, h, i: (0, 0))],
        out_specs=pl.BlockSpec((1, t, V_DIM), lambda b, h, i: (b, i, h)),
        out_shape=jax.ShapeDtypeStruct((B, S, n_heads * V_DIM), BF16),
        scratch_shapes=[pltpu.VMEM((V_DIM, S), BF16), pltpu.VMEM((2, V_DIM, t), F32),
                        pltpu.VMEM((2, 1, t), F32), pltpu.VMEM((2, 1, t), F32)],
        compiler_params=_cparams(("parallel", "parallel", "arbitrary")),
        name="diff_attention",
    )(lam_params, proj3, proj3, proj3, bias, subln_g)


def _conv_kernel(a_ref, g_ref, ap_ref, gp_ref, w_ref, cb_ref, lg_ref, lb_ref, o_ref,
                 hs_ref, cv_ref, *, ts, halo, taps, rc):
    i = pl.program_id(1)
    C = a_ref.shape[-1]
    hs_ref[halo:, :] = a_ref[0].astype(F32) * jax.nn.sigmoid(g_ref[0].astype(F32))
    hp = ap_ref[0].astype(F32) * jax.nn.sigmoid(gp_ref[0].astype(F32))
    hs_ref[:halo, :] = jnp.where(i > 0, hp, 0.0)
    base = halo - (taps - 1)
    for c in range(C // LANES):
        cs = slice(c * LANES, (c + 1) * LANES)
        for r in range(ts // rc):
            acc = jnp.zeros((rc, LANES), F32)
            for tp in range(taps):
                acc = acc + w_ref[tp:tp + 1, cs] * hs_ref[pl.ds(base + tp + r * rc, rc), cs]
            cv_ref[r * rc:(r + 1) * rc, cs] = acc + cb_ref[:, cs]
    h = cv_ref[...]
    mu = jnp.mean(h, axis=-1, keepdims=True)
    var = jnp.mean(jnp.square(h - mu), axis=-1, keepdims=True)
    y = (h - mu) * lax.rsqrt(var + LN_EPS) * lg_ref[...] + lb_ref[...]
    o_ref[0] = (y * jax.nn.sigmoid(y)).astype(o_ref.dtype)


def _conformer_conv(proj3, conv_w, conv_b, ln_g, ln_b, *, a_col, g_col):
    B, S, _ = proj3.shape
    taps, C = conv_w.shape
    ts, halo = CONV_TS, CONV_HALO
    assert halo >= taps - 1
    hb = ts // halo
    prev = lambda col: (lambda b, i: (b, jnp.maximum(i * hb - 1, 0), col))
    vec = pl.BlockSpec((1, C), lambda b, i: (0, 0))
    return pl.pallas_call(
        functools.partial(_conv_kernel, ts=ts, halo=halo, taps=taps, rc=128),
        grid=(B, S // ts),
        in_specs=[pl.BlockSpec((1, ts, C), lambda b, i: (b, i, a_col)),
                  pl.BlockSpec((1, ts, C), lambda b, i: (b, i, g_col)),
                  pl.BlockSpec((1, halo, C), prev(a_col)),
                  pl.BlockSpec((1, halo, C), prev(g_col)),
                  pl.BlockSpec((taps, C), lambda b, i: (0, 0)),
                  vec, vec, vec],
        out_specs=pl.BlockSpec((1, ts, C), lambda b, i: (b, i, 0)),
        out_shape=jax.ShapeDtypeStruct((B, S, C), BF16),
        scratch_shapes=[pltpu.VMEM((halo + ts, C), F32), pltpu.VMEM((ts, C), F32)],
        compiler_params=_cparams(("parallel", "arbitrary")),
        name="conformer_conv",
    )(proj3, proj3, proj3, proj3, conv_w, conv_b, ln_g, ln_b)


def _mem_attn_kernel(q_ref, k_ref, v_ref, o_ref):
    s = lax.dot_general(q_ref[0], k_ref[0], (((1,), (1,)), ((), ())), preferred_element_type=F32)
    m = jnp.max(s, axis=-1, keepdims=True)
    p = jnp.exp(s - m)
    p = p / jnp.sum(p, axis=-1, keepdims=True)
    o_ref[0] = jnp.dot(p.astype(BF16), v_ref[0], preferred_element_type=F32).astype(o_ref.dtype)


def _memory_attention(proj3, kv3, *, n_heads, q_col, hd):
    B, S, _ = proj3.shape
    M = kv3.shape[1]
    tq = MEM_TQ
    return pl.pallas_call(
        _mem_attn_kernel,
        grid=(B, n_heads, S // tq),
        in_specs=[pl.BlockSpec((1, tq, hd), lambda b, h, i: (b, i, q_col + h)),
                  pl.BlockSpec((1, M, hd), lambda b, h, i: (b, 0, h)),
                  pl.BlockSpec((1, M, hd), lambda b, h, i: (b, 0, n_heads + h))],
        out_specs=pl.BlockSpec((1, tq, hd), lambda b, h, i: (b, i, h)),
        out_shape=jax.ShapeDtypeStruct((B, S, n_heads * hd), BF16),
        compiler_params=_cparams(("parallel", "parallel", "arbitrary")),
        name="memory_attention",
    )(proj3, kv3, kv3)


def _oproj_kernel(a_ref, c_ref, m_ref, wa_ref, wc_ref, wm_ref, o_ref):
    acc = jnp.dot(a_ref[...], wa_ref[...].astype(BF16), preferred_element_type=F32)
    acc += jnp.dot(c_ref[...], wc_ref[...].astype(BF16), preferred_element_type=F32)
    acc += jnp.dot(m_ref[...], wm_ref[...].astype(BF16), preferred_element_type=F32)
    o_ref[...] = acc


def _out_projection(a, c, m, w_o, *, tm, tn):
    M, Ka = a.shape
    Kc, Km = c.shape[1], m.shape[1]
    assert Kc == Km and Ka % Kc == 0
    N = w_o.shape[1]
    return pl.pallas_call(
        _oproj_kernel,
        grid=(M // tm, N // tn),
        in_specs=[pl.BlockSpec((tm, Ka), lambda i, j: (i, 0)),
                  pl.BlockSpec((tm, Kc), lambda i, j: (i, 0)),
                  pl.BlockSpec((tm, Km), lambda i, j: (i, 0)),
                  pl.BlockSpec((Ka, tn), lambda i, j: (0, j)),
                  pl.BlockSpec((Kc, tn), lambda i, j: (Ka // Kc, j)),
                  pl.BlockSpec((Km, tn), lambda i, j: (Ka // Kc + 1, j))],
        out_specs=pl.BlockSpec((tm, tn), lambda i, j: (i, j)),
        out_shape=jax.ShapeDtypeStruct((M, N), F32),
        compiler_params=_cparams(("parallel", "arbitrary")),
        name="out_projection",
    )(a, c, m, w_o, w_o, w_o)


def _pack_pairs(lo, hi):
    lo_b = lax.bitcast_convert_type(lo.astype(BF16).astype(F32), U32)
    hi_b = lax.bitcast_convert_type(hi.astype(BF16).astype(F32), U32)
    return (lo_b >> 16) | (hi_b & jnp.uint32(0xFFFF0000))


def _unpack_lo(u):
    return lax.bitcast_convert_type(u << 16, F32)


def _unpack_hi(u):
    return lax.bitcast_convert_type(u & jnp.uint32(0xFFFF0000), F32)


def _layer_norm(y, g, b):
    mu = jnp.mean(y, axis=-1, keepdims=True)
    var = jnp.mean(jnp.square(y - mu), axis=-1, keepdims=True)
    return (y - mu) * lax.rsqrt(var + LN_EPS) * g + b


def _ln_router_kernel(x_ref, mix_ref, g_ref, b_ref, wr_ref, br_ref,
                      x1_ref, xp_ref, topi_ref, gate_ref):
    x1 = _layer_norm(DEEPNORM_ALPHA * x_ref[...] + mix_ref[...], g_ref[...], b_ref[...])
    x1_ref[...] = x1
    half = x1.shape[1] // 2
    xp_ref[...] = _pack_pairs(x1[:, :half], x1[:, half:])
    logits = jnp.dot(x1, wr_ref[...], preferred_element_type=F32,
                     precision=lax.Precision.HIGHEST) + br_ref[...]
    tm, ne = logits.shape
    lane = lax.broadcasted_iota(I32, (tm, ne), 1)
    kk = lax.broadcasted_iota(I32, (tm, TOP_K), 1)
    idx = jnp.zeros((tm, TOP_K), I32)
    val = jnp.zeros((tm, TOP_K), F32)
    cur = logits
    for k in range(TOP_K):
        mx = jnp.max(cur, axis=-1, keepdims=True)
        ix = jnp.min(jnp.where(cur == mx, lane, ne), axis=-1, keepdims=True)
        idx = jnp.where(kk == k, ix, idx)
        val = jnp.where(kk == k, mx, val)
        cur = jnp.where(lane == ix, -jnp.inf, cur)
    e = jnp.exp(val - jnp.max(val, axis=-1, keepdims=True))
    topi_ref[...] = idx
    gate_ref[...] = e / jnp.sum(e, axis=-1, keepdims=True)


def _ln_router(x, mix, g, b, w_router, b_router, *, tm):
    T, D = x.shape
    ne = w_router.shape[1]
    row = pl.BlockSpec((tm, D), lambda i: (i, 0))
    vec = pl.BlockSpec((1, D), lambda i: (0, 0))
    return pl.pallas_call(
        _ln_router_kernel,
        grid=(T // tm,),
        in_specs=[row, row, vec, vec,
                  pl.BlockSpec((D, ne), lambda i: (0, 0)),
                  pl.BlockSpec((1, ne), lambda i: (0, 0))],
        out_specs=[row,
                   pl.BlockSpec((tm, D // 2), lambda i: (i, 0)),
                   pl.BlockSpec((tm, TOP_K), lambda i: (i, 0)),
                   pl.BlockSpec((tm, TOP_K), lambda i: (i, 0))],
        out_shape=[jax.ShapeDtypeStruct((T, D), F32),
                   jax.ShapeDtypeStruct((T, D // 2), U32),
                   jax.ShapeDtypeStruct((T, TOP_K), I32),
                   jax.ShapeDtypeStruct((T, TOP_K), F32)],
        compiler_params=_cparams(("parallel",)),
        name="ln_router",
    )(x, mix, g, b, w_router, b_router)


def _rank_kernel(topi_ref, dest_ref, cnt_ref, ps_ref, rank_ref, carry_ref, *, tt, pad):
    ph = pl.program_id(0)
    i = pl.program_id(1)
    lane = lax.broadcasted_iota(I32, (tt, LANES), 1)
    kk = lax.broadcasted_iota(I32, (tt, TOP_K), 1)
    ti = topi_ref[...]
    sel = [lane == ti[:, k:k + 1] for k in range(TOP_K)]
    rows = pl.ds(pl.multiple_of(i * tt, tt), tt)

    @pl.when(ph == 0)
    def _():
        @pl.when(i == 0)
        def _():
            carry_ref[...] = jnp.zeros_like(carry_ref)

        oh = jnp.zeros((tt, LANES), F32)
        for k in range(TOP_K):
            oh = oh + sel[k].astype(F32)
        r = lax.broadcasted_iota(I32, (tt, tt), 0)
        c = lax.broadcasted_iota(I32, (tt, tt), 1)
        lower = (r > c).astype(BF16)
        before = jnp.dot(lower, oh.astype(BF16), preferred_element_type=F32) + carry_ref[0:1, :]
        rank = jnp.zeros((tt, TOP_K), F32)
        for k in range(TOP_K):
            rk = jnp.sum(jnp.where(sel[k], before, 0.0), axis=-1, keepdims=True)
            rank = jnp.where(kk == k, rk, rank)
        rank_ref[rows, :] = rank
        carry_ref[...] = carry_ref[...] + jnp.sum(oh, axis=0, keepdims=True)

    @pl.when(ph == 1)
    def _():
        cnt = carry_ref[...].astype(I32)
        padded = (cnt + (pad - 1)) & jnp.int32(-pad)
        l8 = lax.broadcasted_iota(I32, (8, LANES), 1)
        scan = padded
        for sh in (1, 2, 4, 8, 16, 32, 64):
            scan = scan + jnp.where(l8 >= sh, pltpu.roll(scan, sh, axis=1), 0)
        starts = scan - padded
        cnt_ref[...] = cnt
        ps_ref[...] = starts
        st = starts[0:1, :].astype(F32)
        rank = rank_ref[rows, :]
        dest = jnp.zeros((tt, TOP_K), F32)
        for k in range(TOP_K):
            base = jnp.sum(jnp.where(sel[k], st, 0.0), axis=-1, keepdims=True)
            dest = jnp.where(kk == k, base, dest)
        dest_ref[...] = (dest + rank).astype(I32)


def _rank_assignments(topi):
    T = topi.shape[0]
    tt = RANK_TT
    return pl.pallas_call(
        functools.partial(_rank_kernel, tt=tt, pad=MOE_PAD),
        grid=(2, T // tt),
        in_specs=[pl.BlockSpec((tt, TOP_K), lambda p, i: (i, 0))],
        out_specs=[pl.BlockSpec((tt, TOP_K), lambda p, i: (i * p, 0)),
                   pl.BlockSpec((8, LANES), lambda p, i: (0, 0)),
                   pl.BlockSpec((8, LANES), lambda p, i: (0, 0))],
        out_shape=[jax.ShapeDtypeStruct((T, TOP_K), I32),
                   jax.ShapeDtypeStruct((8, LANES), I32),
                   jax.ShapeDtypeStruct((8, LANES), I32)],
        scratch_shapes=[pltpu.VMEM((T, TOP_K), F32), pltpu.VMEM((8, LANES), F32)],
        compiler_params=_cparams(("arbitrary", "arbitrary")),
        name="rank_assignments",
    )(topi)


def _dispatch_kernel(fillc_ref, misc_ref, dest_ref, xp_ref, xs_ref, zero_ref, fill_sem, row_sem, *, ta):
    i = pl.program_id(0)
    tt = xp_ref.shape[0]

    @pl.when(i == 0)
    def _():
        def fill(e):
            row = pl.multiple_of(fillc_ref[e] * MOE_PAD, MOE_PAD)
            return pltpu.make_async_copy(zero_ref, xs_ref.at[pl.ds(row, MOE_PAD), :], fill_sem)

        def each_group(fn):
            for e in range(N_EXPERTS):
                pl.when(fillc_ref[e] >= 0)(functools.partial(fn, e))

        zero_ref[...] = jnp.zeros_like(zero_ref)
        each_group(lambda e: fill(e).start())
        _fill_tail(misc_ref, zero_ref, xs_ref, fill_sem)
        each_group(lambda e: fill(e).wait())

    def issue(t, carry):
        for k in range(TOP_K):
            pltpu.make_async_copy(xp_ref.at[pl.ds(t, 1), :],
                                  xs_ref.at[pl.ds(dest_ref[0, 0, t * TOP_K + k], 1), :],
                                  row_sem).start()
        return carry

    lax.fori_loop(0, tt, issue, 0, unroll=2)
    for _ in range(ta // tt):
        pltpu.make_async_copy(xp_ref, xs_ref.at[pl.ds(0, tt), :], row_sem).wait()


def _dispatch(dest, xp, fillc, misc, n_rows):
    T, W = xp.shape
    ta = DISPATCH_TA
    na = dest.size // ta
    assert n_rows % MOE_PAD == 0 and ta % TOP_K == 0
    return pl.pallas_call(
        functools.partial(_dispatch_kernel, ta=ta),
        grid_spec=pltpu.PrefetchScalarGridSpec(
            num_scalar_prefetch=2,
            grid=(na,),
            in_specs=[pl.BlockSpec((1, 1, ta), lambda i, fc, ms: (i, 0, 0), memory_space=pltpu.SMEM),
                      pl.BlockSpec((ta // TOP_K, W), lambda i, fc, ms: (i, 0))],
            out_specs=pl.BlockSpec(memory_space=pl.ANY),
            scratch_shapes=[pltpu.VMEM((MOE_PAD, W), U32),
                            pltpu.SemaphoreType.DMA(()), pltpu.SemaphoreType.DMA(())]),
        out_shape=jax.ShapeDtypeStruct((n_rows, W), U32),
        compiler_params=_cparams(("arbitrary",)),
        name="dispatch_rows",
    )(fillc, misc, dest.reshape(na, 1, ta), xp)


class _TileWriter:
    def __init__(self, wr_ref, wc_ref, stage_ref, out_ref, sem, tn):
        self.wr_ref, self.wc_ref = wr_ref, wc_ref
        self.stage_ref, self.out_ref, self.sem, self.tn = stage_ref, out_ref, sem, tn
        n_chunks = stage_ref.shape[1] // MOE_PAD
        self.sizes = [1 << b for b in reversed(range(n_chunks.bit_length()))]

    def _copy(self, slot, w, j, size):
        n = self.wc_ref[w]
        first = pl.multiple_of((n & -(2 * size)) * MOE_PAD, MOE_PAD)
        row = pl.multiple_of(self.wr_ref[w] + first, MOE_PAD)
        col = pl.multiple_of(j * self.tn, self.tn)
        return pltpu.make_async_copy(
            self.stage_ref.at[slot, pl.ds(first, size * MOE_PAD), :],
            self.out_ref.at[pl.ds(row, size * MOE_PAD), pl.ds(col, self.tn)],
            self.sem.at[slot])

    def _each_group(self, w, fn):
        for size in self.sizes:
            pl.when((self.wc_ref[w] & size) != 0)(functools.partial(fn, size))

    def start(self, slot, w, j):
        self._each_group(w, lambda size: self._copy(slot, w, j, size).start())

    def wait(self, slot, w, j):
        self._each_group(w, lambda size: self._copy(slot, w, j, size).wait())

    def retire_older(self, w, j, nj):
        step = w * nj + j

        @pl.when(step >= 2)
        def _():
            wrap = j < 2
            self.wait(step % 2, jnp.where(wrap, w - 1, w), jnp.where(wrap, j + nj - 2, j - 2))

    def drain(self, w, j, nw, nj):
        step = w * nj + j

        @pl.when(step == nw * nj - 1)
        def _():
            self.wait((step + 1) % 2, w, j - 1)
            self.wait(step % 2, w, j)


def _fill_tail(misc_ref, zero_ref, out_ref, sem):
    n_total = out_ref.shape[0] // MOE_PAD

    def fill(c):
        row = pl.multiple_of(c * MOE_PAD, MOE_PAD)
        return pltpu.make_async_copy(zero_ref, out_ref.at[pl.ds(row, MOE_PAD), :], sem)

    def start(c, carry):
        fill(c).start()
        return carry

    def wait(c, carry):
        fill(c).wait()
        return carry

    lax.fori_loop(misc_ref[0], n_total, start, 0)
    lax.fori_loop(misc_ref[0], n_total, wait, 0)


def _moe_up_kernel(we_ref, wr_ref, wc_ref, misc_ref, xs_ref, wg_ref, wu_ref, b_ref,
                   h_ref, xb_ref, wcat_ref, stage_ref, zero_ref, sem, fill_sem):
    w = pl.program_id(0)
    j = pl.program_id(1)
    nw = pl.num_programs(0)
    nj = pl.num_programs(1)
    slot = (w * nj + j) % 2
    writer = _TileWriter(wr_ref, wc_ref, stage_ref, h_ref, sem, stage_ref.shape[2])

    @pl.when((w == 0) & (j == 0))
    def _():
        zero_ref[...] = jnp.zeros_like(zero_ref)
        _fill_tail(misc_ref, zero_ref, h_ref, fill_sem)

    writer.retire_older(w, j, nj)

    @pl.when(wc_ref[w] > 0)
    def _():
        half = xs_ref.shape[1]

        @pl.when(j == 0)
        def _():
            u = xs_ref[...]
            xb_ref[:, :half] = _unpack_lo(u).astype(BF16)
            xb_ref[:, half:] = _unpack_hi(u).astype(BF16)

        tn = wg_ref.shape[2]
        wcat_ref[:, :tn] = wg_ref[0].astype(BF16)
        wcat_ref[:, tn:] = wu_ref[0].astype(BF16)
        col = pl.multiple_of(j * tn, tn)
        bg = b_ref[0, :, pl.ds(col, tn)]
        bu = b_ref[0, :, pl.ds(pl.multiple_of(b_ref.shape[2] // 2 + col, tn), tn)]
        def rows_block(rows):
            r = jnp.dot(xb_ref[rows, :], wcat_ref[...], preferred_element_type=F32)
            gate = r[:, :tn] + bg
            up = r[:, tn:] + bu
            gate = jnp.minimum(gate, SWIGLU_LIMIT)
            up = jnp.clip(up, -SWIGLU_LIMIT, SWIGLU_LIMIT)
            act = (up + 1.0) * (gate * jax.nn.sigmoid(SWIGLU_ALPHA * gate))
            stage_ref[slot, rows, :] = act.astype(stage_ref.dtype)

        def all_rows(n_rows):
            mc = n_rows // MOE_UP_ROW_CHUNKS
            for mi in range(MOE_UP_ROW_CHUNKS):
                rows_block(slice(mi * mc, (mi + 1) * mc))

        all_rows(xb_ref.shape[0])
        writer.start(slot, w, j)

    writer.drain(w, j, nw, nj)


def _moe_up(xs, w_gu, b_gu, we, wr, wc, misc, *, d_ff):
    R, half = xs.shape
    E, D, _ = w_gu.shape
    tm, tn = MOE_TM, MOE_TN_UP
    nj = d_ff // tn
    nw = we.shape[0]
    assert nj >= 2 and R % MOE_PAD == 0 and tm % (16 * MOE_UP_ROW_CHUNKS) == 0
    jeff = lambda w, j, wc: jnp.where(wc[w] > 0, j, nj - 1)
    return pl.pallas_call(
        _moe_up_kernel,
        grid_spec=pltpu.PrefetchScalarGridSpec(
            num_scalar_prefetch=4,
            grid=(misc[1], nj),
            in_specs=[pl.BlockSpec((pl.Element(tm), pl.Element(half)),
                                   lambda w, j, we, wr, wc, ms: (pl.multiple_of(wr[w], MOE_PAD), 0)),
                      pl.BlockSpec((1, D, tn), lambda w, j, we, wr, wc, ms: (we[w], 0, jeff(w, j, wc))),
                      pl.BlockSpec((1, D, tn), lambda w, j, we, wr, wc, ms: (we[w], 0, nj + jeff(w, j, wc))),
                      pl.BlockSpec((1, 1, 2 * nj * tn), lambda w, j, we, wr, wc, ms: (we[w], 0, 0))],
            out_specs=pl.BlockSpec(memory_space=pl.ANY),
            scratch_shapes=[pltpu.VMEM((tm, 2 * half), BF16),
                            pltpu.VMEM((D, 2 * tn), BF16),
                            pltpu.VMEM((2, tm, tn), BF16),
                            pltpu.VMEM((MOE_PAD, d_ff), BF16),
                            pltpu.SemaphoreType.DMA((2,)), pltpu.SemaphoreType.DMA(())]),
        out_shape=jax.ShapeDtypeStruct((R, d_ff), BF16),
        compiler_params=_cparams(("arbitrary", "arbitrary")),
        name="moe_up",
    )(we, wr, wc, misc, xs, w_gu, w_gu, b_gu)


def _moe_down_kernel(we_ref, wr_ref, wc_ref, misc_ref, h_ref, w_ref, b_ref,
                     o_ref, stage_ref, zero_ref, sem, fill_sem):
    w = pl.program_id(0)
    j = pl.program_id(1)
    nw = pl.num_programs(0)
    nj = pl.num_programs(1)
    slot = (w * nj + j) % 2
    writer = _TileWriter(wr_ref, wc_ref, stage_ref, o_ref, sem, stage_ref.shape[2])

    @pl.when((w == 0) & (j == 0))
    def _():
        zero_ref[...] = jnp.zeros_like(zero_ref)
        _fill_tail(misc_ref, zero_ref, o_ref, fill_sem)

    writer.retire_older(w, j, nj)

    @pl.when(wc_ref[w] > 0)
    def _():
        tn = stage_ref.shape[2]
        bias = b_ref[0, :, pl.ds(pl.multiple_of(j * 2 * tn, 2 * tn), 2 * tn)]

        def rows_block(rows):
            r = jnp.dot(h_ref[rows, :], w_ref[0].astype(BF16), preferred_element_type=F32) + bias
            stage_ref[slot, rows, :] = _pack_pairs(r[:, :tn], r[:, tn:])

        tm, main = h_ref.shape[0], min(MOE_MAIN_ROWS, h_ref.shape[0])
        if tm > main:
            fits = wc_ref[w] * MOE_PAD <= main
            pl.when(fits)(functools.partial(rows_block, slice(0, main)))
            pl.when(jnp.logical_not(fits))(functools.partial(rows_block, slice(0, tm)))
        else:
            rows_block(slice(0, tm))
        writer.start(slot, w, j)

    writer.drain(w, j, nw, nj)


def _moe_down(h, w_down, b_down, we, wr, wc, misc):
    R, d_ff = h.shape
    E, _, D = w_down.shape
    tm, tn = MOE_TM, MOE_TN_DOWN
    half = D // 2
    nj = half // tn
    nw = we.shape[0]
    assert nj >= 2 and R % MOE_PAD == 0
    jeff = lambda w, j, wc: jnp.where(wc[w] > 0, j, nj - 1)
    return pl.pallas_call(
        _moe_down_kernel,
        grid_spec=pltpu.PrefetchScalarGridSpec(
            num_scalar_prefetch=4,
            grid=(misc[1], nj),
            in_specs=[pl.BlockSpec((pl.Element(tm), pl.Element(d_ff)),
                                   lambda w, j, we, wr, wc, ms: (pl.multiple_of(wr[w], MOE_PAD), 0)),
                      pl.BlockSpec((1, d_ff, 2 * tn), lambda w, j, we, wr, wc, ms: (we[w], 0, jeff(w, j, wc))),
                      pl.BlockSpec((1, 1, 2 * nj * tn), lambda w, j, we, wr, wc, ms: (we[w], 0, 0))],
            out_specs=pl.BlockSpec(memory_space=pl.ANY),
            scratch_shapes=[pltpu.VMEM((2, tm, tn), U32),
                            pltpu.VMEM((MOE_PAD, half), U32),
                            pltpu.SemaphoreType.DMA((2,)), pltpu.SemaphoreType.DMA(())]),
        out_shape=jax.ShapeDtypeStruct((R, half), U32),
        compiler_params=_cparams(("arbitrary", "arbitrary")),
        name="moe_down",
    )(we, wr, wc, misc, h, w_down, b_down)


def _combine_kernel(dest_ref, dnext_ref, eo_ref, gate_ref, x1_ref, g_ref, b_ref, o_ref, buf_ref, sem,
                    *, tc):
    i = pl.program_id(0)
    n = tc * TOP_K
    slot = i % 2

    def gather(d_ref, slot):
        def issue(t, carry):
            for k in range(TOP_K):
                pltpu.make_async_copy(eo_ref.at[pl.ds(d_ref[0, 0, t * TOP_K + k], 1), :],
                                      buf_ref.at[slot, pl.ds(k * tc + t, 1), :],
                                      sem.at[slot]).start()
            return carry

        lax.fori_loop(0, tc, issue, 0, unroll=2)

    @pl.when(i == 0)
    def _():
        gather(dest_ref, 0)

    @pl.when(i + 1 < pl.num_programs(0))
    def _():
        gather(dnext_ref, 1 - slot)

    pltpu.make_async_copy(eo_ref.at[pl.ds(0, n), :], buf_ref.at[slot], sem.at[slot]).wait()

    gates = gate_ref[...]
    half = buf_ref.shape[2]
    cw = 2 * LANES
    row_sum = jnp.zeros((tc, 1), F32)
    for c in range(half // cw):
        tn = MOE_TN_DOWN
        u_cols = slice(c * cw, (c + 1) * cw)
        first = (c * cw // tn) * 2 * tn + c * cw % tn
        lo_cols = slice(first, first + cw)
        hi_cols = slice(first + tn, first + tn + cw)
        lo = DEEPNORM_ALPHA * x1_ref[:, lo_cols]
        hi = DEEPNORM_ALPHA * x1_ref[:, hi_cols]
        for k in range(TOP_K):
            u = buf_ref[slot, k * tc:(k + 1) * tc, u_cols]
            gk = gates[:, k:k + 1]
            lo = lo + gk * _unpack_lo(u)
            hi = hi + gk * _unpack_hi(u)
        o_ref[:, lo_cols] = lo
        o_ref[:, hi_cols] = hi
        row_sum = row_sum + jnp.sum(lo, axis=-1, keepdims=True) + jnp.sum(hi, axis=-1, keepdims=True)
    inv_d = 1.0 / (2 * half)
    mu = row_sum * inv_d
    sq_sum = jnp.zeros((tc, 1), F32)
    for c in range(2 * half // cw):
        d = o_ref[:, c * cw:(c + 1) * cw] - mu
        sq_sum = sq_sum + jnp.sum(d * d, axis=-1, keepdims=True)
    rstd = lax.rsqrt(sq_sum * inv_d + LN_EPS)
    for c in range(2 * half // cw):
        cols = slice(c * cw, (c + 1) * cw)
        o_ref[:, cols] = (o_ref[:, cols] - mu) * rstd * g_ref[:, cols] + b_ref[:, cols]


def _combine(dest, eo, gates, x1, g, b):
    T, D = x1.shape
    tc = COMBINE_TC
    n = tc * TOP_K
    nt = T // tc
    half = eo.shape[1]
    row = pl.BlockSpec((tc, D), lambda i: (i, 0))
    vec = pl.BlockSpec((1, D), lambda i: (0, 0))
    dest3 = dest.reshape(nt, 1, n)
    return pl.pallas_call(
        functools.partial(_combine_kernel, tc=tc),
        grid=(nt,),
        in_specs=[pl.BlockSpec((1, 1, n), lambda i: (i, 0, 0), memory_space=pltpu.SMEM),
                  pl.BlockSpec((1, 1, n), lambda i: (jnp.minimum(i + 1, nt - 1), 0, 0),
                               memory_space=pltpu.SMEM),
                  pl.BlockSpec(memory_space=pl.ANY),
                  pl.BlockSpec((tc, TOP_K), lambda i: (i, 0)),
                  row, vec, vec],
        out_specs=row,
        out_shape=jax.ShapeDtypeStruct((T, D), F32),
        scratch_shapes=[pltpu.VMEM((2, n, half), U32), pltpu.SemaphoreType.DMA((2,))],
        compiler_params=_cparams(("arbitrary",)),
        name="combine_rows",
    )(dest3, dest3, eo, gates, x1, g, b)


def _work_items(cnt, starts, n_items):
    chunks_per_pass = MOE_TM // MOE_PAD
    gchunks = (cnt + MOE_PAD - 1) // MOE_PAD
    npass = (gchunks + chunks_per_pass - 1) // chunks_per_pass
    cum = jnp.cumsum(npass)
    total = cum[-1]
    w = jnp.arange(n_items, dtype=I32)
    e = jnp.minimum(jnp.sum((cum[None, :] <= w[:, None]).astype(I32), axis=1), N_EXPERTS - 1)
    p = w - (cum - npass)[e]
    row = starts[e] + p * MOE_TM
    nch = jnp.clip(gchunks[e] - p * chunks_per_pass, 0, chunks_per_pass)
    valid = w < total
    last = jnp.maximum(total - 1, 0)
    e = jnp.where(valid, e, e[last]).astype(I32)
    row = jnp.where(valid, row, row[last]).astype(I32)
    nch = jnp.where(valid, nch, 0).astype(I32)
    misc = jnp.stack([jnp.sum(gchunks), total]).astype(I32)
    fillc = jnp.where(cnt % MOE_PAD != 0, starts // MOE_PAD + gchunks - 1, -1).astype(I32)
    return e, row, nch, misc, fillc


def kernel(x, mem, rel_table, w_in, w_mem_kv, w_o, lambda_q1, lambda_k1, lambda_q2, lambda_k2, subln_g, conv_w, conv_b, conv_ln_g, conv_ln_b, ln1_g, ln1_b, w_router, b_router, w_gate_up, b_gate_up, w_down, b_down, ln2_g, ln2_b):
    B, S, D = x.shape
    T = B * S
    n_heads = rel_table.shape[1]
    qk_w = n_heads * 2 * HEAD_DIM
    v_w = n_heads * V_DIM
    C = conv_w.shape[-1]
    mem_w = w_mem_kv.shape[-1] // 2
    mem_heads = 4
    mem_hd = mem_w // mem_heads
    in_cols = w_in.shape[-1]
    assert in_cols == 2 * qk_w + v_w + 2 * C + mem_w
    d_ff = w_down.shape[2]
    assert DEPTH == 1 and w_in.shape[0] == 1

    col_scale = jnp.concatenate([
        jnp.full((qk_w,), HEAD_DIM ** -0.5, F32),
        jnp.ones((in_cols - qk_w - mem_w,), F32),
        jnp.full((mem_w,), mem_hd ** -0.5, F32)]).reshape(1, in_cols)

    xt = x.reshape(T, D)
    proj = _proj_matmul(xt.astype(BF16), w_in[0], col_scale, tm=1024, tn=512, out_dtype=BF16)
    proj3 = proj.reshape(B, S, in_cols)

    bias = _bias_tiles(rel_table, ATT_T)
    lam_params = jnp.concatenate([lambda_q1, lambda_k1, lambda_q2, lambda_k2], axis=0).astype(F32)
    a_out = _diff_attention(proj3, bias, lam_params, subln_g, n_heads=n_heads,
                            k_col=qk_w // (2 * HEAD_DIM), v_col=2 * qk_w // V_DIM)

    conv_col = (2 * qk_w + v_w) // C
    c_out = _conformer_conv(proj3, conv_w[0, :, 0, :], conv_b, conv_ln_g, conv_ln_b,
                            a_col=conv_col, g_col=conv_col + 1)

    n_mem = mem.shape[1]
    kv = _proj_matmul(mem.reshape(B * n_mem, D).astype(BF16), w_mem_kv[0],
                      jnp.ones((1, 2 * mem_w), F32), tm=B * n_mem, tn=512, out_dtype=BF16)
    m_out = _memory_attention(proj3, kv.reshape(B, n_mem, 2 * mem_w), n_heads=mem_heads,
                              q_col=(in_cols - mem_w) // mem_hd, hd=mem_hd)

    mix = _out_projection(a_out.reshape(T, v_w), c_out.reshape(T, C), m_out.reshape(T, mem_w),
                          w_o[0], tm=1024, tn=512)

    x1, xp, topi, gates = _ln_router(xt, mix, ln1_g, ln1_b, w_router[0],
                                     b_router.reshape(1, -1), tm=256)

    dest, cnt8, starts8 = _rank_assignments(topi)
    cnt, starts = cnt8[0, :N_EXPERTS], starts8[0, :N_EXPERTS]

    n_assign = T * TOP_K
    n_items = n_assign // MOE_TM + N_EXPERTS
    max_rows = n_assign + N_EXPERTS * (MOE_PAD - 1) + MOE_TM
    n_rows = -(-max_rows // MOE_PAD) * MOE_PAD
    we, wr, wc, misc, fillc = _work_items(cnt, starts, n_items)

    xs = _dispatch(dest, xp, fillc, misc, n_rows)
    h = _moe_up(xs, w_gate_up[0], b_gate_up[0].reshape(N_EXPERTS, 1, -1), we, wr, wc, misc, d_ff=d_ff)
    eo = _moe_down(h, w_down[0], b_down[0].reshape(N_EXPERTS, 1, -1), we, wr, wc, misc)
    out = _combine(dest, eo, gates, x1, ln2_g, ln2_b)
    return out.reshape(B, S, D)
```

```python
import functools
import math

import jax
import jax.numpy as jnp
from jax import lax
from jax.experimental import pallas as pl
from jax.experimental.pallas import tpu as pltpu

F32 = jnp.float32
BF16 = jnp.bfloat16
I32 = jnp.int32
U32 = jnp.uint32

HEAD_DIM = 128
V_DIM = 2 * HEAD_DIM
CONV_TAPS = 31
NUM_BUCKETS = 32
MAX_EXACT = NUM_BUCKETS // 2
MAX_DISTANCE = 128
N_EXPERTS = 32
TOP_K = 4
SWIGLU_LIMIT = 7.0
SWIGLU_ALPHA = 1.702
LN_EPS = 1e-5
DEPTH = 1
DEEPNORM_ALPHA = (2 * DEPTH) ** 0.25
LAM_INIT = 0.8 - 0.6 * math.exp(-0.3 * 0)

LANES = 128
V7X_VMEM_BYTES = 64 * 1024 * 1024
VMEM_LIMIT = 58 * 1024 * 1024

ATT_T = 256
CONV_TS = 256
CONV_HALO = 32
MEM_TQ = 512
MOE_TM = 1088
MOE_MAIN_ROWS = 1024
MOE_PAD = 64
MOE_TN_UP = 256
MOE_UP_ROW_CHUNKS = 2
MOE_TN_DOWN = 512
RANK_TT = 512
DISPATCH_TA = 2048
COMBINE_TC = 256


def _cparams(sem, vmem=VMEM_LIMIT):
    return pltpu.CompilerParams(dimension_semantics=sem, vmem_limit_bytes=vmem)


def _proj_kernel(a_ref, w_ref, cs_ref, o_ref):
    acc = jnp.dot(a_ref[...], w_ref[...].astype(BF16), preferred_element_type=F32)
    o_ref[...] = (acc * cs_ref[...]).astype(o_ref.dtype)


def _proj_matmul(a, w, col_scale, *, tm, tn, out_dtype):
    M, K = a.shape
    N = w.shape[1]
    return pl.pallas_call(
        _proj_kernel,
        grid=(M // tm, N // tn),
        in_specs=[pl.BlockSpec((tm, K), lambda i, j: (i, 0)),
                  pl.BlockSpec((K, tn), lambda i, j: (0, j)),
                  pl.BlockSpec((1, tn), lambda i, j: (0, j))],
        out_specs=pl.BlockSpec((tm, tn), lambda i, j: (i, j)),
        out_shape=jax.ShapeDtypeStruct((M, N), out_dtype),
        compiler_params=_cparams(("parallel", "arbitrary")),
        name="proj_matmul",
    )(a, w, col_scale)


def _bias_kernel(tbl_ref, o_ref, *, t, n_heads):
    h = pl.program_id(0)
    j = lax.broadcasted_iota(I32, (t, t), 0)
    i = lax.broadcasted_iota(I32, (t, t), 1)
    for o in range(2):
        dist = o * t + i - j
        n = jnp.maximum(dist, 0)
        nf = jnp.maximum(n, 1).astype(F32)
        large = MAX_EXACT + (jnp.log(nf / MAX_EXACT) / math.log(MAX_DISTANCE / MAX_EXACT)
                             * (NUM_BUCKETS - MAX_EXACT)).astype(I32)
        large = jnp.minimum(large, NUM_BUCKETS - 1)
        bucket = jnp.where(n < MAX_EXACT, n, large)
        val = jnp.zeros((t, t), F32)
        for b in range(NUM_BUCKETS):
            val = jnp.where(bucket == b, tbl_ref[b * n_heads + h], val)
        if o == 0:
            val = jnp.where(dist >= 0, val, -jnp.inf)
        o_ref[0, o] = val
    o_ref[0, 2] = jnp.full((t, t), tbl_ref[(NUM_BUCKETS - 1) * n_heads + h], F32)


def _bias_tiles(rel_table, t):
    n_heads = rel_table.shape[1]
    assert t + 1 >= MAX_DISTANCE
    return pl.pallas_call(
        functools.partial(_bias_kernel, t=t, n_heads=n_heads),
        grid=(n_heads,),
        in_specs=[pl.BlockSpec(memory_space=pltpu.SMEM)],
        out_specs=pl.BlockSpec((1, 3, t, t), lambda h: (h, 0, 0, 0)),
        out_shape=jax.ShapeDtypeStruct((n_heads, 3, t, t), F32),
        compiler_params=_cparams(("arbitrary",)),
        name="bias_tiles",
    )(rel_table.reshape(-1))


def _attn_kernel(lam_ref, q_ref, k_ref, v_ref, bias_ref, g_ref, o_ref, vt_ref, acc_ref, m_ref, l_ref,
                 *, t):
    qi = pl.program_id(2)
    n_kt = v_ref.shape[1] // t

    @pl.when(qi == 0)
    def _():
        for kb in range(n_kt):
            vt_ref[:, kb * t:(kb + 1) * t] = v_ref[0, kb * t:(kb + 1) * t, :].astype(F32).T.astype(BF16)

    q = q_ref[0]
    qs = (q[:, :HEAD_DIM], q[:, HEAD_DIM:])
    dn = (((1,), (1,)), ((), ()))
    acc_ref[...] = jnp.zeros_like(acc_ref)
    m_ref[...] = jnp.full(m_ref.shape, -jnp.inf, F32)
    l_ref[...] = jnp.zeros_like(l_ref)

    def update(kb0, n_tiles):
        start = pl.multiple_of(kb0 * t, t)
        k = k_ref[0, pl.ds(start, n_tiles * t), :]
        vt = vt_ref[:, pl.ds(start, n_tiles * t)]
        bias = [bias_ref[0, jnp.minimum(qi - kb0 - i, 2)] for i in range(n_tiles)]
        for c in range(2):
            s = [lax.dot_general(k[i * t:(i + 1) * t, c * HEAD_DIM:(c + 1) * HEAD_DIM], qs[c], dn,
                                 preferred_element_type=F32) + bias[i]
                 for i in range(n_tiles)]
            m_old = m_ref[c]
            m_new = m_old
            for si in s:
                m_new = jnp.maximum(m_new, jnp.max(si, axis=0, keepdims=True))
            alpha = jnp.exp(m_old - m_new)
            p = [jnp.exp(si - m_new) for si in s]
            l_new = alpha * l_ref[c]
            for pi in p:
                l_new = l_new + jnp.sum(pi, axis=0, keepdims=True)
            pcat = p[0].astype(BF16) if n_tiles == 1 else jnp.concatenate(
                [pi.astype(BF16) for pi in p], axis=0)
            acc_ref[c] = alpha * acc_ref[c] + jnp.dot(vt, pcat, preferred_element_type=F32)
            m_ref[c] = m_new
            l_ref[c] = l_new

    n = qi + 1

    def pair(i, carry):
        update(2 * i, 2)
        return carry

    lax.fori_loop(0, n // 2, pair, 0)

    @pl.when(n % 2 == 1)
    def _():
        update(qi, 1)

    lp = lam_ref[...]
    lam = (jnp.exp(jnp.sum(lp[0:1] * lp[1:2], axis=-1, keepdims=True))
           - jnp.exp(jnp.sum(lp[2:3] * lp[3:4], axis=-1, keepdims=True)) + LAM_INIT)
    ot = acc_ref[0] * (1.0 / l_ref[0]) - acc_ref[1] * (lam / l_ref[1])
    ot = ot * lax.rsqrt(jnp.mean(jnp.square(ot), axis=0, keepdims=True) + LN_EPS)
    o = ot.T * g_ref[...]
    o_ref[0] = (o * (1.0 - LAM_INIT)).astype(o_ref.dtype)


def _diff_attention(proj3, bias, lam_params, subln_g, *, n_heads, k_col, v_col):
    B, S, _ = proj3.shape
    t = ATT_T
    nq = S // t
    w = 2 * HEAD_DIM
    return pl.pallas_call(
        functools.partial(_attn_kernel, t=t),
        grid=(B, n_heads, nq),
        in_specs=[pl.BlockSpec((4, HEAD_DIM), lambda b, h, i: (0, 0)),
                  pl.BlockSpec((1, t, w), lambda b, h, i: (b, i, h)),
                  pl.BlockSpec((1, S, w), lambda b, h, i: (b, 0, k_col + h)),
                  pl.BlockSpec((1, S, w), lambda b, h, i: (b, 0, v_col + h)),
                  pl.BlockSpec((1, 3, t, t), lambda b, h, i: (h, 0, 0, 0)),
                  pl.BlockSpec((1, V_DIM), lambda b, h, i: (0, 0))],
        out_specs=pl.BlockSpec((1, t, V_DIM), lambda b, h, i: (b, i, h)),
        out_shape=jax.ShapeDtypeStruct((B, S, n_heads * V_DIM), BF16),
        scratch_shapes=[pltpu.VMEM((V_DIM, S), BF16), pltpu.VMEM((2, V_DIM, t), F32),
                        pltpu.VMEM((2, 1, t), F32), pltpu.VMEM((2, 1, t), F32)],
        compiler_params=_cparams(("parallel", "parallel", "arbitrary")),
        name="diff_attention",
    )(lam_params, proj3, proj3, proj3, bias, subln_g)


def _conv_kernel(a_ref, g_ref, ap_ref, gp_ref, w_ref, cb_ref, lg_ref, lb_ref, o_ref,
                 hs_ref, cv_ref, *, ts, halo, taps, rc):
    i = pl.program_id(1)
    C = a_ref.shape[-1]
    hs_ref[halo:, :] = a_ref[0].astype(F32) * jax.nn.sigmoid(g_ref[0].astype(F32))
    hp = ap_ref[0].astype(F32) * jax.nn.sigmoid(gp_ref[0].astype(F32))
    hs_ref[:halo, :] = jnp.where(i > 0, hp, 0.0)
    base = halo - (taps - 1)
    for c in range(C // LANES):
        cs = slice(c * LANES, (c + 1) * LANES)
        for r in range(ts // rc):
            acc = jnp.zeros((rc, LANES), F32)
            for tp in range(taps):
                acc = acc + w_ref[tp:tp + 1, cs] * hs_ref[pl.ds(base + tp + r * rc, rc), cs]
            cv_ref[r * rc:(r + 1) * rc, cs] = acc + cb_ref[:, cs]
    h = cv_ref[...]
    mu = jnp.mean(h, axis=-1, keepdims=True)
    var = jnp.mean(jnp.square(h - mu), axis=-1, keepdims=True)
    y = (h - mu) * lax.rsqrt(var + LN_EPS) * lg_ref[...] + lb_ref[...]
    o_ref[0] = (y * jax.nn.sigmoid(y)).astype(o_ref.dtype)


def _conformer_conv(proj3, conv_w, conv_b, ln_g, ln_b, *, a_col, g_col):
    B, S, _ = proj3.shape
    taps, C = conv_w.shape
    ts, halo = CONV_TS, CONV_HALO
    assert halo >= taps - 1
    hb = ts // halo
    prev = lambda col: (lambda b, i: (b, jnp.maximum(i * hb - 1, 0), col))
    vec = pl.BlockSpec((1, C), lambda b, i: (0, 0))
    return pl.pallas_call(
        functools.partial(_conv_kernel, ts=ts, halo=halo, taps=taps, rc=128),
        grid=(B, S // ts),
        in_specs=[pl.BlockSpec((1, ts, C), lambda b, i: (b, i, a_col)),
                  pl.BlockSpec((1, ts, C), lambda b, i: (b, i, g_col)),
                  pl.BlockSpec((1, halo, C), prev(a_col)),
                  pl.BlockSpec((1, halo, C), prev(g_col)),
                  pl.BlockSpec((taps, C), lambda b, i: (0, 0)),
                  vec, vec, vec],
        out_specs=pl.BlockSpec((1, ts, C), lambda b, i: (b, i, 0)),
        out_shape=jax.ShapeDtypeStruct((B, S, C), BF16),
        scratch_shapes=[pltpu.VMEM((halo + ts, C), F32), pltpu.VMEM((ts, C), F32)],
        compiler_params=_cparams(("parallel", "arbitrary")),
        name="conformer_conv",
    )(proj3, proj3, proj3, proj3, conv_w, conv_b, ln_g, ln_b)


def _mem_attn_kernel(q_ref, k_ref, v_ref, o_ref):
    s = lax.dot_general(q_ref[0], k_ref[0], (((1,), (1,)), ((), ())), preferred_element_type=F32)
    m = jnp.max(s, axis=-1, keepdims=True)
    p = jnp.exp(s - m)
    p = p / jnp.sum(p, axis=-1, keepdims=True)
    o_ref[0] = jnp.dot(p.astype(BF16), v_ref[0], preferred_element_type=F32).astype(o_ref.dtype)


def _memory_attention(proj3, kv3, *, n_heads, q_col, hd):
    B, S, _ = proj3.shape
    M = kv3.shape[1]
    tq = MEM_TQ
    return pl.pallas_call(
        _mem_attn_kernel,
        grid=(B, n_heads, S // tq),
        in_specs=[pl.BlockSpec((1, tq, hd), lambda b, h, i: (b, i, q_col + h)),
                  pl.BlockSpec((1, M, hd), lambda b, h, i: (b, 0, h)),
                  pl.BlockSpec((1, M, hd), lambda b, h, i: (b, 0, n_heads + h))],
        out_specs=pl.BlockSpec((1, tq, hd), lambda b, h, i: (b, i, h)),
        out_shape=jax.ShapeDtypeStruct((B, S, n_heads * hd), BF16),
        compiler_params=_cparams(("parallel", "parallel", "arbitrary")),
        name="memory_attention",
    )(proj3, kv3, kv3)


def _oproj_kernel(a_ref, c_ref, m_ref, wa_ref, wc_ref, wm_ref, o_ref):
    acc = jnp.dot(a_ref[...], wa_ref[...].astype(BF16), preferred_element_type=F32)
    acc += jnp.dot(c_ref[...], wc_ref[...].astype(BF16), preferred_element_type=F32)
    acc += jnp.dot(m_ref[...], wm_ref[...].astype(BF16), preferred_element_type=F32)
    o_ref[...] = acc


def _out_projection(a, c, m, w_o, *, tm, tn):
    M, Ka = a.shape
    Kc, Km = c.shape[1], m.shape[1]
    assert Kc == Km and Ka % Kc == 0
    N = w_o.shape[1]
    return pl.pallas_call(
        _oproj_kernel,
        grid=(M // tm, N // tn),
        in_specs=[pl.BlockSpec((tm, Ka), lambda i, j: (i, 0)),
                  pl.BlockSpec((tm, Kc), lambda i, j: (i, 0)),
                  pl.BlockSpec((tm, Km), lambda i, j: (i, 0)),
                  pl.BlockSpec((Ka, tn), lambda i, j: (0, j)),
                  pl.BlockSpec((Kc, tn), lambda i, j: (Ka // Kc, j)),
                  pl.BlockSpec((Km, tn), lambda i, j: (Ka // Kc + 1, j))],
        out_specs=pl.BlockSpec((tm, tn), lambda i, j: (i, j)),
        out_shape=jax.ShapeDtypeStruct((M, N), F32),
        compiler_params=_cparams(("parallel", "arbitrary")),
        name="out_projection",
    )(a, c, m, w_o, w_o, w_o)


def _pack_pairs(lo, hi):
    lo_b = lax.bitcast_convert_type(lo.astype(BF16).astype(F32), U32)
    hi_b = lax.bitcast_convert_type(hi.astype(BF16).astype(F32), U32)
    return (lo_b >> 16) | (hi_b & jnp.uint32(0xFFFF0000))


def _unpack_lo(u):
    return lax.bitcast_convert_type(u << 16, F32)


def _unpack_hi(u):
    return lax.bitcast_convert_type(u & jnp.uint32(0xFFFF0000), F32)


def _layer_norm(y, g, b):
    mu = jnp.mean(y, axis=-1, keepdims=True)
    var = jnp.mean(jnp.square(y - mu), axis=-1, keepdims=True)
    return (y - mu) * lax.rsqrt(var + LN_EPS) * g + b


def _ln_router_kernel(x_ref, mix_ref, g_ref, b_ref, wr_ref, br_ref,
                      x1_ref, xp_ref, topi_ref, gate_ref):
    x1 = _layer_norm(DEEPNORM_ALPHA * x_ref[...] + mix_ref[...], g_ref[...], b_ref[...])
    x1_ref[...] = x1
    half = x1.shape[1] // 2
    xp_ref[...] = _pack_pairs(x1[:, :half], x1[:, half:])
    logits = jnp.dot(x1, wr_ref[...], preferred_element_type=F32,
                     precision=lax.Precision.HIGHEST) + br_ref[...]
    tm, ne = logits.shape
    lane = lax.broadcasted_iota(I32, (tm, ne), 1)
    kk = lax.broadcasted_iota(I32, (tm, TOP_K), 1)
    idx = jnp.zeros((tm, TOP_K), I32)
    val = jnp.zeros((tm, TOP_K), F32)
    cur = logits
    for k in range(TOP_K):
        mx = jnp.max(cur, axis=-1, keepdims=True)
        ix = jnp.min(jnp.where(cur == mx, lane, ne), axis=-1, keepdims=True)
        idx = jnp.where(kk == k, ix, idx)
        val = jnp.where(kk == k, mx, val)
        cur = jnp.where(lane == ix, -jnp.inf, cur)
    e = jnp.exp(val - jnp.max(val, axis=-1, keepdims=True))
    topi_ref[...] = idx
    gate_ref[...] = e / jnp.sum(e, axis=-1, keepdims=True)


def _ln_router(x, mix, g, b, w_router, b_router, *, tm):
    T, D = x.shape
    ne = w_router.shape[1]
    row = pl.BlockSpec((tm, D), lambda i: (i, 0))
    vec = pl.BlockSpec((1, D), lambda i: (0, 0))
    return pl.pallas_call(
        _ln_router_kernel,
        grid=(T // tm,),
        in_specs=[row, row, vec, vec,
                  pl.BlockSpec((D, ne), lambda i: (0, 0)),
                  pl.BlockSpec((1, ne), lambda i: (0, 0))],
        out_specs=[row,
                   pl.BlockSpec((tm, D // 2), lambda i: (i, 0)),
                   pl.BlockSpec((tm, TOP_K), lambda i: (i, 0)),
                   pl.BlockSpec((tm, TOP_K), lambda i: (i, 0))],
        out_shape=[jax.ShapeDtypeStruct((T, D), F32),
                   jax.ShapeDtypeStruct((T, D // 2), U32),
                   jax.ShapeDtypeStruct((T, TOP_K), I32),
                   jax.ShapeDtypeStruct((T, TOP_K), F32)],
        compiler_params=_cparams(("parallel",)),
        name="ln_router",
    )(x, mix, g, b, w_router, b_router)


def _rank_kernel(topi_ref, dest_ref, cnt_ref, ps_ref, rank_ref, carry_ref, *, tt, pad):
    ph = pl.program_id(0)
    i = pl.program_id(1)
    lane = lax.broadcasted_iota(I32, (tt, LANES), 1)
    kk = lax.broadcasted_iota(I32, (tt, TOP_K), 1)
    ti = topi_ref[...]
    sel = [lane == ti[:, k:k + 1] for k in range(TOP_K)]
    rows = pl.ds(pl.multiple_of(i * tt, tt), tt)

    @pl.when(ph == 0)
    def _():
        @pl.when(i == 0)
        def _():
            carry_ref[...] = jnp.zeros_like(carry_ref)

        oh = jnp.zeros((tt, LANES), F32)
        for k in range(TOP_K):
            oh = oh + sel[k].astype(F32)
        r = lax.broadcasted_iota(I32, (tt, tt), 0)
        c = lax.broadcasted_iota(I32, (tt, tt), 1)
        lower = (r > c).astype(BF16)
        before = jnp.dot(lower, oh.astype(BF16), preferred_element_type=F32) + carry_ref[0:1, :]
        rank = jnp.zeros((tt, TOP_K), F32)
        for k in range(TOP_K):
            rk = jnp.sum(jnp.where(sel[k], before, 0.0), axis=-1, keepdims=True)
            rank = jnp.where(kk == k, rk, rank)
        rank_ref[rows, :] = rank
        carry_ref[...] = carry_ref[...] + jnp.sum(oh, axis=0, keepdims=True)

    @pl.when(ph == 1)
    def _():
        cnt = carry_ref[...].astype(I32)
        padded = (cnt + (pad - 1)) & jnp.int32(-pad)
        l8 = lax.broadcasted_iota(I32, (8, LANES), 1)
        scan = padded
        for sh in (1, 2, 4, 8, 16, 32, 64):
            scan = scan + jnp.where(l8 >= sh, pltpu.roll(scan, sh, axis=1), 0)
        starts = scan - padded
        cnt_ref[...] = cnt
        ps_ref[...] = starts
        st = starts[0:1, :].astype(F32)
        rank = rank_ref[rows, :]
        dest = jnp.zeros((tt, TOP_K), F32)
        for k in range(TOP_K):
            base = jnp.sum(jnp.where(sel[k], st, 0.0), axis=-1, keepdims=True)
            dest = jnp.where(kk == k, base, dest)
        dest_ref[...] = (dest + rank).astype(I32)


def _rank_assignments(topi):
    T = topi.shape[0]
    tt = RANK_TT
    return pl.pallas_call(
        functools.partial(_rank_kernel, tt=tt, pad=MOE_PAD),
        grid=(2, T // tt),
        in_specs=[pl.BlockSpec((tt, TOP_K), lambda p, i: (i, 0))],
        out_specs=[pl.BlockSpec((tt, TOP_K), lambda p, i: (i * p, 0)),
                   pl.BlockSpec((8, LANES), lambda p, i: (0, 0)),
                   pl.BlockSpec((8, LANES), lambda p, i: (0, 0))],
        out_shape=[jax.ShapeDtypeStruct((T, TOP_K), I32),
                   jax.ShapeDtypeStruct((8, LANES), I32),
                   jax.ShapeDtypeStruct((8, LANES), I32)],
        scratch_shapes=[pltpu.VMEM((T, TOP_K), F32), pltpu.VMEM((8, LANES), F32)],
        compiler_params=_cparams(("arbitrary", "arbitrary")),
        name="rank_assignments",
    )(topi)


def _dispatch_kernel(fillc_ref, misc_ref, dest_ref, xp_ref, xs_ref, zero_ref, fill_sem, row_sem, *, ta):
    i = pl.program_id(0)
    tt = xp_ref.shape[0]

    @pl.when(i == 0)
    def _():
        def fill(e):
            row = pl.multiple_of(fillc_ref[e] * MOE_PAD, MOE_PAD)
            return pltpu.make_async_copy(zero_ref, xs_ref.at[pl.ds(row, MOE_PAD), :], fill_sem)

        def each_group(fn):
            for e in range(N_EXPERTS):
                pl.when(fillc_ref[e] >= 0)(functools.partial(fn, e))

        zero_ref[...] = jnp.zeros_like(zero_ref)
        each_group(lambda e: fill(e).start())
        _fill_tail(misc_ref, zero_ref, xs_ref, fill_sem)
        each_group(lambda e: fill(e).wait())

    def issue(t, carry):
        for k in range(TOP_K):
            pltpu.make_async_copy(xp_ref.at[pl.ds(t, 1), :],
                                  xs_ref.at[pl.ds(dest_ref[0, 0, t * TOP_K + k], 1), :],
                                  row_sem).start()
        return carry

    lax.fori_loop(0, tt, issue, 0, unroll=2)
    for _ in range(ta // tt):
        pltpu.make_async_copy(xp_ref, xs_ref.at[pl.ds(0, tt), :], row_sem).wait()


def _dispatch(dest, xp, fillc, misc, n_rows):
    T, W = xp.shape
    ta = DISPATCH_TA
    na = dest.size // ta
    assert n_rows % MOE_PAD == 0 and ta % TOP_K == 0
    return pl.pallas_call(
        functools.partial(_dispatch_kernel, ta=ta),
        grid_spec=pltpu.PrefetchScalarGridSpec(
            num_scalar_prefetch=2,
            grid=(na,),
            in_specs=[pl.BlockSpec((1, 1, ta), lambda i, fc, ms: (i, 0, 0), memory_space=pltpu.SMEM),
                      pl.BlockSpec((ta // TOP_K, W), lambda i, fc, ms: (i, 0))],
            out_specs=pl.BlockSpec(memory_space=pl.ANY),
            scratch_shapes=[pltpu.VMEM((MOE_PAD, W), U32),
                            pltpu.SemaphoreType.DMA(()), pltpu.SemaphoreType.DMA(())]),
        out_shape=jax.ShapeDtypeStruct((n_rows, W), U32),
        compiler_params=_cparams(("arbitrary",)),
        name="dispatch_rows",
    )(fillc, misc, dest.reshape(na, 1, ta), xp)


class _TileWriter:
    def __init__(self, wr_ref, wc_ref, stage_ref, out_ref, sem, tn):
        self.wr_ref, self.wc_ref = wr_ref, wc_ref
        self.stage_ref, self.out_ref, self.sem, self.tn = stage_ref, out_ref, sem, tn
        n_chunks = stage_ref.shape[1] // MOE_PAD
        self.sizes = [1 << b for b in reversed(range(n_chunks.bit_length()))]

    def _copy(self, slot, w, j, size):
        n = self.wc_ref[w]
        first = pl.multiple_of((n & -(2 * size)) * MOE_PAD, MOE_PAD)
        row = pl.multiple_of(self.wr_ref[w] + first, MOE_PAD)
        col = pl.multiple_of(j * self.tn, self.tn)
        return pltpu.make_async_copy(
            self.stage_ref.at[slot, pl.ds(first, size * MOE_PAD), :],
            self.out_ref.at[pl.ds(row, size * MOE_PAD), pl.ds(col, self.tn)],
            self.sem.at[slot])

    def _each_group(self, w, fn):
        for size in self.sizes:
            pl.when((self.wc_ref[w] & size) != 0)(functools.partial(fn, size))

    def start(self, slot, w, j):
        self._each_group(w, lambda size: self._copy(slot, w, j, size).start())

    def wait(self, slot, w, j):
        self._each_group(w, lambda size: self._copy(slot, w, j, size).wait())

    def retire_older(self, w, j, nj):
        step = w * nj + j

        @pl.when(step >= 2)
        def _():
            wrap = j < 2
            self.wait(step % 2, jnp.where(wrap, w - 1, w), jnp.where(wrap, j + nj - 2, j - 2))

    def drain(self, w, j, nw, nj):
        step = w * nj + j

        @pl.when(step == nw * nj - 1)
        def _():
            self.wait((step + 1) % 2, w, j - 1)
            self.wait(step % 2, w, j)


def _fill_tail(misc_ref, zero_ref, out_ref, sem):
    n_total = out_ref.shape[0] // MOE_PAD

    def fill(c):
        row = pl.multiple_of(c * MOE_PAD, MOE_PAD)
        return pltpu.make_async_copy(zero_ref, out_ref.at[pl.ds(row, MOE_PAD), :], sem)

    def start(c, carry):
        fill(c).start()
        return carry

    def wait(c, carry):
        fill(c).wait()
        return carry

    lax.fori_loop(misc_ref[0], n_total, start, 0)
    lax.fori_loop(misc_ref[0], n_total, wait, 0)


def _moe_up_kernel(we_ref, wr_ref, wc_ref, misc_ref, xs_ref, wg_ref, wu_ref, b_ref,
                   h_ref, xb_ref, wcat_ref, stage_ref, zero_ref, sem, fill_sem):
    w = pl.program_id(0)
    j = pl.program_id(1)
    nw = pl.num_programs(0)
    nj = pl.num_programs(1)
    slot = (w * nj + j) % 2
    writer = _TileWriter(wr_ref, wc_ref, stage_ref, h_ref, sem, stage_ref.shape[2])

    @pl.when((w == 0) & (j == 0))
    def _():
        zero_ref[...] = jnp.zeros_like(zero_ref)
        _fill_tail(misc_ref, zero_ref, h_ref, fill_sem)

    writer.retire_older(w, j, nj)

    @pl.when(wc_ref[w] > 0)
    def _():
        half = xs_ref.shape[1]

        @pl.when(j == 0)
        def _():
            u = xs_ref[...]
            xb_ref[:, :half] = _unpack_lo(u).astype(BF16)
            xb_ref[:, half:] = _unpack_hi(u).astype(BF16)

        tn = wg_ref.shape[2]
        wcat_ref[:, :tn] = wg_ref[0].astype(BF16)
        wcat_ref[:, tn:] = wu_ref[0].astype(BF16)
        col = pl.multiple_of(j * tn, tn)
        bg = b_ref[0, :, pl.ds(col, tn)]
        bu = b_ref[0, :, pl.ds(pl.multiple_of(b_ref.shape[2] // 2 + col, tn), tn)]
        def rows_block(rows):
            r = jnp.dot(xb_ref[rows, :], wcat_ref[...], preferred_element_type=F32)
            gate = r[:, :tn] + bg
            up = r[:, tn:] + bu
            gate = jnp.minimum(gate, SWIGLU_LIMIT)
            up = jnp.clip(up, -SWIGLU_LIMIT, SWIGLU_LIMIT)
            act = (up + 1.0) * (gate * jax.nn.sigmoid(SWIGLU_ALPHA * gate))
            stage_ref[slot, rows, :] = act.astype(stage_ref.dtype)

        def all_rows(n_rows):
            mc = n_rows // MOE_UP_ROW_CHUNKS
            for mi in range(MOE_UP_ROW_CHUNKS):
                rows_block(slice(mi * mc, (mi + 1) * mc))

        all_rows(xb_ref.shape[0])
        writer.start(slot, w, j)

    writer.drain(w, j, nw, nj)


def _moe_up(xs, w_gu, b_gu, we, wr, wc, misc, *, d_ff):
    R, half = xs.shape
    E, D, _ = w_gu.shape
    tm, tn = MOE_TM, MOE_TN_UP
    nj = d_ff // tn
    nw = we.shape[0]
    assert nj >= 2 and R % MOE_PAD == 0 and tm % (16 * MOE_UP_ROW_CHUNKS) == 0
    jeff = lambda w, j, wc: jnp.where(wc[w] > 0, j, nj - 1)
    return pl.pallas_call(
        _moe_up_kernel,
        grid_spec=pltpu.PrefetchScalarGridSpec(
            num_scalar_prefetch=4,
            grid=(misc[1], nj),
            in_specs=[pl.BlockSpec((pl.Element(tm), pl.Element(half)),
                                   lambda w, j, we, wr, wc, ms: (pl.multiple_of(wr[w], MOE_PAD), 0)),
                      pl.BlockSpec((1, D, tn), lambda w, j, we, wr, wc, ms: (we[w], 0, jeff(w, j, wc))),
                      pl.BlockSpec((1, D, tn), lambda w, j, we, wr, wc, ms: (we[w], 0, nj + jeff(w, j, wc))),
                      pl.BlockSpec((1, 1, 2 * nj * tn), lambda w, j, we, wr, wc, ms: (we[w], 0, 0))],
            out_specs=pl.BlockSpec(memory_space=pl.ANY),
            scratch_shapes=[pltpu.VMEM((tm, 2 * half), BF16),
                            pltpu.VMEM((D, 2 * tn), BF16),
                            pltpu.VMEM((2, tm, tn), BF16),
                            pltpu.VMEM((MOE_PAD, d_ff), BF16),
                            pltpu.SemaphoreType.DMA((2,)), pltpu.SemaphoreType.DMA(())]),
        out_shape=jax.ShapeDtypeStruct((R, d_ff), BF16),
        compiler_params=_cparams(("arbitrary", "arbitrary")),
        name="moe_up",
    )(we, wr, wc, misc, xs, w_gu, w_gu, b_gu)


def _moe_down_kernel(we_ref, wr_ref, wc_ref, misc_ref, h_ref, w_ref, b_ref,
                     o_ref, stage_ref, zero_ref, sem, fill_sem):
    w = pl.program_id(0)
    j = pl.program_id(1)
    nw = pl.num_programs(0)
    nj = pl.num_programs(1)
    slot = (w * nj + j) % 2
    writer = _TileWriter(wr_ref, wc_ref, stage_ref, o_ref, sem, stage_ref.shape[2])

    @pl.when((w == 0) & (j == 0))
    def _():
        zero_ref[...] = jnp.zeros_like(zero_ref)
        _fill_tail(misc_ref, zero_ref, o_ref, fill_sem)

    writer.retire_older(w, j, nj)

    @pl.when(wc_ref[w] > 0)
    def _():
        tn = stage_ref.shape[2]
        bias = b_ref[0, :, pl.ds(pl.multiple_of(j * 2 * tn, 2 * tn), 2 * tn)]

        def rows_block(rows):
            r = jnp.dot(h_ref[rows, :], w_ref[0].astype(BF16), preferred_element_type=F32) + bias
            stage_ref[slot, rows, :] = _pack_pairs(r[:, :tn], r[:, tn:])

        tm, main = h_ref.shape[0], min(MOE_MAIN_ROWS, h_ref.shape[0])
        if tm > main:
            fits = wc_ref[w] * MOE_PAD <= main
            pl.when(fits)(functools.partial(rows_block, slice(0, main)))
            pl.when(jnp.logical_not(fits))(functools.partial(rows_block, slice(0, tm)))
        else:
            rows_block(slice(0, tm))
        writer.start(slot, w, j)

    writer.drain(w, j, nw, nj)


def _moe_down(h, w_down, b_down, we, wr, wc, misc):
    R, d_ff = h.shape
    E, _, D = w_down.shape
    tm, tn = MOE_TM, MOE_TN_DOWN
    half = D // 2
    nj = half // tn
    nw = we.shape[0]
    assert nj >= 2 and R % MOE_PAD == 0
    jeff = lambda w, j, wc: jnp.where(wc[w] > 0, j, nj - 1)
    return pl.pallas_call(
        _moe_down_kernel,
        grid_spec=pltpu.PrefetchScalarGridSpec(
            num_scalar_prefetch=4,
            grid=(misc[1], nj),
            in_specs=[pl.BlockSpec((pl.Element(tm), pl.Element(d_ff)),
                                   lambda w, j, we, wr, wc, ms: (pl.multiple_of(wr[w], MOE_PAD), 0)),
                      pl.BlockSpec((1, d_ff, 2 * tn), lambda w, j, we, wr, wc, ms: (we[w], 0, jeff(w, j, wc))),
                      pl.BlockSpec((1, 1, 2 * nj * tn), lambda w, j, we, wr, wc, ms: (we[w], 0, 0))],
            out_specs=pl.BlockSpec(memory_space=pl.ANY),
            scratch_shapes=[pltpu.VMEM((2, tm, tn), U32),
                            pltpu.VMEM((MOE_PAD, half), U32),
                            pltpu.SemaphoreType.DMA((2,)), pltpu.SemaphoreType.DMA(())]),
        out_shape=jax.ShapeDtypeStruct((R, half), U32),
        compiler_params=_cparams(("arbitrary", "arbitrary")),
        name="moe_down",
    )(we, wr, wc, misc, h, w_down, b_down)


def _combine_kernel(dest_ref, dnext_ref, eo_ref, gate_ref, x1_ref, g_ref, b_ref, o_ref, buf_ref, sem,
                    *, tc):
    i = pl.program_id(0)
    n = tc * TOP_K
    slot = i % 2

    def gather(d_ref, slot):
        def issue(t, carry):
            for k in range(TOP_K):
                pltpu.make_async_copy(eo_ref.at[pl.ds(d_ref[0, 0, t * TOP_K + k], 1), :],
                                      buf_ref.at[slot, pl.ds(k * tc + t, 1), :],
                                      sem.at[slot]).start()
            return carry

        lax.fori_loop(0, tc, issue, 0, unroll=2)

    @pl.when(i == 0)
    def _():
        gather(dest_ref, 0)

    @pl.when(i + 1 < pl.num_programs(0))
    def _():
        gather(dnext_ref, 1 - slot)

    pltpu.make_async_copy(eo_ref.at[pl.ds(0, n), :], buf_ref.at[slot], sem.at[slot]).wait()

    gates = gate_ref[...]
    half = buf_ref.shape[2]
    cw = 2 * LANES
    row_sum = jnp.zeros((tc, 1), F32)
    for c in range(half // cw):
        tn = MOE_TN_DOWN
        u_cols = slice(c * cw, (c + 1) * cw)
        first = (c * cw // tn) * 2 * tn + c * cw % tn
        lo_cols = slice(first, first + cw)
        hi_cols = slice(first + tn, first + tn + cw)
        lo = DEEPNORM_ALPHA * x1_ref[:, lo_cols]
        hi = DEEPNORM_ALPHA * x1_ref[:, hi_cols]
        for k in range(TOP_K):
            u = buf_ref[slot, k * tc:(k + 1) * tc, u_cols]
            gk = gates[:, k:k + 1]
            lo = lo + gk * _unpack_lo(u)
            hi = hi + gk * _unpack_hi(u)
        o_ref[:, lo_cols] = lo
        o_ref[:, hi_cols] = hi
        row_sum = row_sum + jnp.sum(lo, axis=-1, keepdims=True) + jnp.sum(hi, axis=-1, keepdims=True)
    inv_d = 1.0 / (2 * half)
    mu = row_sum * inv_d
    sq_sum = jnp.zeros((tc, 1), F32)
    for c in range(2 * half // cw):
        d = o_ref[:, c * cw:(c + 1) * cw] - mu
        sq_sum = sq_sum + jnp.sum(d * d, axis=-1, keepdims=True)
    rstd = lax.rsqrt(sq_sum * inv_d + LN_EPS)
    for c in range(2 * half // cw):
        cols = slice(c * cw, (c + 1) * cw)
        o_ref[:, cols] = (o_ref[:, cols] - mu) * rstd * g_ref[:, cols] + b_ref[:, cols]


def _combine(dest, eo, gates, x1, g, b):
    T, D = x1.shape
    tc = COMBINE_TC
    n = tc * TOP_K
    nt = T // tc
    half = eo.shape[1]
    row = pl.BlockSpec((tc, D), lambda i: (i, 0))
    vec = pl.BlockSpec((1, D), lambda i: (0, 0))
    dest3 = dest.reshape(nt, 1, n)
    return pl.pallas_call(
        functools.partial(_combine_kernel, tc=tc),
        grid=(nt,),
        in_specs=[pl.BlockSpec((1, 1, n), lambda i: (i, 0, 0), memory_space=pltpu.SMEM),
                  pl.BlockSpec((1, 1, n), lambda i: (jnp.minimum(i + 1, nt - 1), 0, 0),
                               memory_space=pltpu.SMEM),
                  pl.BlockSpec(memory_space=pl.ANY),
                  pl.BlockSpec((tc, TOP_K), lambda i: (i, 0)),
                  row, vec, vec],
        out_specs=row,
        out_shape=jax.ShapeDtypeStruct((T, D), F32),
        scratch_shapes=[pltpu.VMEM((2, n, half), U32), pltpu.SemaphoreType.DMA((2,))],
        compiler_params=_cparams(("arbitrary",)),
        name="combine_rows",
    )(dest3, dest3, eo, gates, x1, g, b)


def _work_items(cnt, starts, n_items):
    chunks_per_pass = MOE_TM // MOE_PAD
    gchunks = (cnt + MOE_PAD - 1) // MOE_PAD
    npass = (gchunks + chunks_per_pass - 1) // chunks_per_pass
    cum = jnp.cumsum(npass)
    total = cum[-1]
    w = jnp.arange(n_items, dtype=I32)
    e = jnp.minimum(jnp.sum((cum[None, :] <= w[:, None]).astype(I32), axis=1), N_EXPERTS - 1)
    p = w - (cum - npass)[e]
    row = starts[e] + p * MOE_TM
    nch = jnp.clip(gchunks[e] - p * chunks_per_pass, 0, chunks_per_pass)
    valid = w < total
    last = jnp.maximum(total - 1, 0)
    e = jnp.where(valid, e, e[last]).astype(I32)
    row = jnp.where(valid, row, row[last]).astype(I32)
    nch = jnp.where(valid, nch, 0).astype(I32)
    misc = jnp.stack([jnp.sum(gchunks), total]).astype(I32)
    fillc = jnp.where(cnt % MOE_PAD != 0, starts // MOE_PAD + gchunks - 1, -1).astype(I32)
    return e, row, nch, misc, fillc


def kernel(x, mem, rel_table, w_in, w_mem_kv, w_o, lambda_q1, lambda_k1, lambda_q2, lambda_k2, subln_g, conv_w, conv_b, conv_ln_g, conv_ln_b, ln1_g, ln1_b, w_router, b_router, w_gate_up, b_gate_up, w_down, b_down, ln2_g, ln2_b):
    B, S, D = x.shape
    T = B * S
    n_heads = rel_table.shape[1]
    qk_w = n_heads * 2 * HEAD_DIM
    v_w = n_heads * V_DIM
    C = conv_w.shape[-1]
    mem_w = w_mem_kv.shape[-1] // 2
    mem_heads = 4
    mem_hd = mem_w // mem_heads
    in_cols = w_in.shape[-1]
    assert in_cols == 2 * qk_w + v_w + 2 * C + mem_w
    d_ff = w_down.shape[2]
    assert DEPTH == 1 and w_in.shape[0] == 1

    col_scale = jnp.concatenate([
        jnp.full((qk_w,), HEAD_DIM ** -0.5, F32),
        jnp.ones((in_cols - qk_w - mem_w,), F32),
        jnp.full((mem_w,), mem_hd ** -0.5, F32)]).reshape(1, in_cols)

    xt = x.reshape(T, D)
    proj = _proj_matmul(xt.astype(BF16), w_in[0], col_scale, tm=1024, tn=768, out_dtype=BF16)
    proj3 = proj.reshape(B, S, in_cols)

    bias = _bias_tiles(rel_table, ATT_T)
    lam_params = jnp.concatenate([lambda_q1, lambda_k1, lambda_q2, lambda_k2], axis=0).astype(F32)
    a_out = _diff_attention(proj3, bias, lam_params, subln_g, n_heads=n_heads,
                            k_col=qk_w // (2 * HEAD_DIM), v_col=2 * qk_w // V_DIM)

    conv_col = (2 * qk_w + v_w) // C
    c_out = _conformer_conv(proj3, conv_w[0, :, 0, :], conv_b, conv_ln_g, conv_ln_b,
                            a_col=conv_col, g_col=conv_col + 1)

    n_mem = mem.shape[1]
    kv = _proj_matmul(mem.reshape(B * n_mem, D).astype(BF16), w_mem_kv[0],
                      jnp.ones((1, 2 * mem_w), F32), tm=B * n_mem, tn=512, out_dtype=BF16)
    m_out = _memory_attention(proj3, kv.reshape(B, n_mem, 2 * mem_w), n_heads=mem_heads,
                              q_col=(in_cols - mem_w) // mem_hd, hd=mem_hd)

    mix = _out_projection(a_out.reshape(T, v_w), c_out.reshape(T, C), m_out.reshape(T, mem_w),
                          w_o[0], tm=1024, tn=512)

    x1, xp, topi, gates = _ln_router(xt, mix, ln1_g, ln1_b, w_router[0],
                                     b_router.reshape(1, -1), tm=256)

    dest, cnt8, starts8 = _rank_assignments(topi)
    cnt, starts = cnt8[0, :N_EXPERTS], starts8[0, :N_EXPERTS]

    n_assign = T * TOP_K
    n_items = n_assign // MOE_TM + N_EXPERTS
    max_rows = n_assign + N_EXPERTS * (MOE_PAD - 1) + MOE_TM
    n_rows = -(-max_rows // MOE_PAD) * MOE_PAD
    we, wr, wc, misc, fillc = _work_items(cnt, starts, n_items)

    xs = _dispatch(dest, xp, fillc, misc, n_rows)
    h = _moe_up(xs, w_gate_up[0], b_gate_up[0].reshape(N_EXPERTS, 1, -1), we, wr, wc, misc, d_ff=d_ff)
    eo = _moe_down(h, w_down[0], b_down[0].reshape(N_EXPERTS, 1, -1), we, wr, wc, misc)
    out = _combine(dest, eo, gates, x1, ln2_g, ln2_b)
    return out.reshape(B, S, D)
```

```python
import functools
import math

import jax
import jax.numpy as jnp
from jax import lax
from jax.experimental import pallas as pl
from jax.experimental.pallas import tpu as pltpu

F32 = jnp.float32
BF16 = jnp.bfloat16
I32 = jnp.int32
U32 = jnp.uint32

HEAD_DIM = 128
V_DIM = 2 * HEAD_DIM
CONV_TAPS = 31
NUM_BUCKETS = 32
MAX_EXACT = NUM_BUCKETS // 2
MAX_DISTANCE = 128
N_EXPERTS = 32
TOP_K = 4
SWIGLU_LIMIT = 7.0
SWIGLU_ALPHA = 1.702
LN_EPS = 1e-5
DEPTH = 1
DEEPNORM_ALPHA = (2 * DEPTH) ** 0.25
LAM_INIT = 0.8 - 0.6 * math.exp(-0.3 * 0)

LANES = 128
V7X_VMEM_BYTES = 64 * 1024 * 1024
VMEM_LIMIT = 58 * 1024 * 1024

ATT_T = 256
CONV_TS = 256
CONV_HALO = 32
MEM_TQ = 512
MOE_TM = 1088
MOE_MAIN_ROWS = 1024
MOE_PAD = 64
MOE_TN_UP = 256
MOE_UP_ROW_CHUNKS = 2
MOE_TN_DOWN = 512
RANK_TT = 512
DISPATCH_TA = 2048
COMBINE_TC = 256


def _cparams(sem, vmem=VMEM_LIMIT):
    return pltpu.CompilerParams(dimension_semantics=sem, vmem_limit_bytes=vmem)


def _proj_kernel(a_ref, w_ref, cs_ref, o_ref):
    acc = jnp.dot(a_ref[...], w_ref[...].astype(BF16), preferred_element_type=F32)
    o_ref[...] = (acc * cs_ref[...]).astype(o_ref.dtype)


def _proj_matmul(a, w, col_scale, *, tm, tn, out_dtype):
    M, K = a.shape
    N = w.shape[1]
    return pl.pallas_call(
        _proj_kernel,
        grid=(M // tm, N // tn),
        in_specs=[pl.BlockSpec((tm, K), lambda i, j: (i, 0)),
                  pl.BlockSpec((K, tn), lambda i, j: (0, j)),
                  pl.BlockSpec((1, tn), lambda i, j: (0, j))],
        out_specs=pl.BlockSpec((tm, tn), lambda i, j: (i, j)),
        out_shape=jax.ShapeDtypeStruct((M, N), out_dtype),
        compiler_params=_cparams(("parallel", "arbitrary")),
        name="proj_matmul",
    )(a, w, col_scale)


def _bias_kernel(tbl_ref, o_ref, *, t, n_heads):
    h = pl.program_id(0)
    j = lax.broadcasted_iota(I32, (t, t), 0)
    i = lax.broadcasted_iota(I32, (t, t), 1)
    for o in range(2):
        dist = o * t + i - j
        n = jnp.maximum(dist, 0)
        nf = jnp.maximum(n, 1).astype(F32)
        large = MAX_EXACT + (jnp.log(nf / MAX_EXACT) / math.log(MAX_DISTANCE / MAX_EXACT)
                             * (NUM_BUCKETS - MAX_EXACT)).astype(I32)
        large = jnp.minimum(large, NUM_BUCKETS - 1)
        bucket = jnp.where(n < MAX_EXACT, n, large)
        val = jnp.zeros((t, t), F32)
        for b in range(NUM_BUCKETS):
            val = jnp.where(bucket == b, tbl_ref[b * n_heads + h], val)
        if o == 0:
            val = jnp.where(dist >= 0, val, -jnp.inf)
        o_ref[0, o] = val
    o_ref[0, 2] = jnp.full((t, t), tbl_ref[(NUM_BUCKETS - 1) * n_heads + h], F32)


def _bias_tiles(rel_table, t):
    n_heads = rel_table.shape[1]
    assert t + 1 >= MAX_DISTANCE
    return pl.pallas_call(
        functools.partial(_bias_kernel, t=t, n_heads=n_heads),
        grid=(n_heads,),
        in_specs=[pl.BlockSpec(memory_space=pltpu.SMEM)],
        out_specs=pl.BlockSpec((1, 3, t, t), lambda h: (h, 0, 0, 0)),
        out_shape=jax.ShapeDtypeStruct((n_heads, 3, t, t), F32),
        compiler_params=_cparams(("arbitrary",)),
        name="bias_tiles",
    )(rel_table.reshape(-1))


def _attn_kernel(lam_ref, q_ref, k_ref, v_ref, bias_ref, g_ref, o_ref, vt_ref, acc_ref, m_ref, l_ref,
                 *, t):
    qi = pl.program_id(2)
    n_kt = v_ref.shape[1] // t

    @pl.when(qi == 0)
    def _():
        for kb in range(n_kt):
            vt_ref[:, kb * t:(kb + 1) * t] = v_ref[0, kb * t:(kb + 1) * t, :].astype(F32).T.astype(BF16)

    q = q_ref[0]
    qs = (q[:, :HEAD_DIM], q[:, HEAD_DIM:])
    dn = (((1,), (1,)), ((), ()))
    acc_ref[...] = jnp.zeros_like(acc_ref)
    m_ref[...] = jnp.full(m_ref.shape, -jnp.inf, F32)
    l_ref[...] = jnp.zeros_like(l_ref)

    def update(kb0, n_tiles):
        start = pl.multiple_of(kb0 * t, t)
        k = k_ref[0, pl.ds(start, n_tiles * t), :]
        vt = vt_ref[:, pl.ds(start, n_tiles * t)]
        bias = [bias_ref[0, jnp.minimum(qi - kb0 - i, 2)] for i in range(n_tiles)]
        for c in range(2):
            s = [lax.dot_general(k[i * t:(i + 1) * t, c * HEAD_DIM:(c + 1) * HEAD_DIM], qs[c], dn,
                                 preferred_element_type=F32) + bias[i]
                 for i in range(n_tiles)]
            m_old = m_ref[c]
            m_new = m_old
            for si in s:
                m_new = jnp.maximum(m_new, jnp.max(si, axis=0, keepdims=True))
            alpha = jnp.exp(m_old - m_new)
            p = [jnp.exp(si - m_new) for si in s]
            l_new = alpha * l_ref[c]
            for pi in p:
                l_new = l_new + jnp.sum(pi, axis=0, keepdims=True)
            pcat = p[0].astype(BF16) if n_tiles == 1 else jnp.concatenate(
                [pi.astype(BF16) for pi in p], axis=0)
            acc_ref[c] = alpha * acc_ref[c] + jnp.dot(vt, pcat, preferred_element_type=F32)
            m_ref[c] = m_new
            l_ref[c] = l_new

    n = qi + 1

    def pair(i, carry):
        update(2 * i, 2)
        return carry

    lax.fori_loop(0, n // 2, pair, 0)

    @pl.when(n % 2 == 1)
    def _():
        update(qi, 1)

    lp = lam_ref[...]
    lam = (jnp.exp(jnp.sum(lp[0:1] * lp[1:2], axis=-1, keepdims=True))
           - jnp.exp(jnp.sum(lp[2:3] * lp[3:4], axis=-1, keepdims=True)) + LAM_INIT)
    ot = acc_ref[0] * (1.0 / l_ref[0]) - acc_ref[1] * (lam / l_ref[1])
    ot = ot * lax.rsqrt(jnp.mean(jnp.square(ot), axis=0, keepdims=True) + LN_EPS)
    o = ot.T * g_ref[...]
    o_ref[0] = (o * (1.0 - LAM_INIT)).astype(o_ref.dtype)


def _diff_attention(proj3, bias, lam_params, subln_g, *, n_heads, k_col, v_col):
    B, S, _ = proj3.shape
    t = ATT_T
    nq = S // t
    w = 2 * HEAD_DIM
    return pl.pallas_call(
        functools.partial(_attn_kernel, t=t),
        grid=(B, n_heads, nq),
        in_specs=[pl.BlockSpec((4, HEAD_DIM), lambda b, h, i: (0, 0)),
                  pl.BlockSpec((1, t, w), lambda b, h, i: (b, i, h)),
                  pl.BlockSpec((1, S, w), lambda b, h, i: (b, 0, k_col + h)),
                  pl.BlockSpec((1, S, w), lambda b, h, i: (b, 0, v_col + h)),
                  pl.BlockSpec((1, 3, t, t), lambda b, h, i: (h, 0, 0, 0)),
                  pl.BlockSpec((1, V_DIM), lambda b, h, i: (0, 0))],
        out_specs=pl.BlockSpec((1, t, V_DIM), lambda b, h, i: (b, i, h)),
        out_shape=jax.ShapeDtypeStruct((B, S, n_heads * V_DIM), BF16),
        scratch_shapes=[pltpu.VMEM((V_DIM, S), BF16), pltpu.VMEM((2, V_DIM, t), F32),
                        pltpu.VMEM((2, 1, t), F32), pltpu.VMEM((2, 1, t), F32)],
        compiler_params=_cparams(("parallel", "parallel", "arbitrary")),
        name="diff_attention",
    )(lam_params, proj3, proj3, proj3, bias, subln_g)


def _conv_kernel(a_ref, g_ref, ap_ref, gp_ref, w_ref, cb_ref, lg_ref, lb_ref, o_ref,
                 hs_ref, cv_ref, *, ts, halo, taps, rc):
    i = pl.program_id(1)
    C = a_ref.shape[-1]
    hs_ref[halo:, :] = a_ref[0].astype(F32) * jax.nn.sigmoid(g_ref[0].astype(F32))
    hp = ap_ref[0].astype(F32) * jax.nn.sigmoid(gp_ref[0].astype(F32))
    hs_ref[:halo, :] = jnp.where(i > 0, hp, 0.0)
    base = halo - (taps - 1)
    for c in range(C // LANES):
        cs = slice(c * LANES, (c + 1) * LANES)
        for r in range(ts // rc):
            acc = jnp.zeros((rc, LANES), F32)
            for tp in range(taps):
                acc = acc + w_ref[tp:tp + 1, cs] * hs_ref[pl.ds(base + tp + r * rc, rc), cs]
            cv_ref[r * rc:(r + 1) * rc, cs] = acc + cb_ref[:, cs]
    h = cv_ref[...]
    mu = jnp.mean(h, axis=-1, keepdims=True)
    var = jnp.mean(jnp.square(h - mu), axis=-1, keepdims=True)
    y = (h - mu) * lax.rsqrt(var + LN_EPS) * lg_ref[...] + lb_ref[...]
    o_ref[0] = (y * jax.nn.sigmoid(y)).astype(o_ref.dtype)


def _conformer_conv(proj3, conv_w, conv_b, ln_g, ln_b, *, a_col, g_col):
    B, S, _ = proj3.shape
    taps, C = conv_w.shape
    ts, halo = CONV_TS, CONV_HALO
    assert halo >= taps - 1
    hb = ts // halo
    prev = lambda col: (lambda b, i: (b, jnp.maximum(i * hb - 1, 0), col))
    vec = pl.BlockSpec((1, C), lambda b, i: (0, 0))
    return pl.pallas_call(
        functools.partial(_conv_kernel, ts=ts, halo=halo, taps=taps, rc=128),
        grid=(B, S // ts),
        in_specs=[pl.BlockSpec((1, ts, C), lambda b, i: (b, i, a_col)),
                  pl.BlockSpec((1, ts, C), lambda b, i: (b, i, g_col)),
                  pl.BlockSpec((1, halo, C), prev(a_col)),
                  pl.BlockSpec((1, halo, C), prev(g_col)),
                  pl.BlockSpec((taps, C), lambda b, i: (0, 0)),
                  vec, vec, vec],
        out_specs=pl.BlockSpec((1, ts, C), lambda b, i: (b, i, 0)),
        out_shape=jax.ShapeDtypeStruct((B, S, C), BF16),
        scratch_shapes=[pltpu.VMEM((halo + ts, C), F32), pltpu.VMEM((ts, C), F32)],
        compiler_params=_cparams(("parallel", "arbitrary")),
        name="conformer_conv",
    )(proj3, proj3, proj3, proj3, conv_w, conv_b, ln_g, ln_b)


def _mem_attn_kernel(q_ref, k_ref, v_ref, o_ref):
    s = lax.dot_general(q_ref[0], k_ref[0], (((1,), (1,)), ((), ())), preferred_element_type=F32)
    m = jnp.max(s, axis=-1, keepdims=True)
    p = jnp.exp(s - m)
    p = p / jnp.sum(p, axis=-1, keepdims=True)
    o_ref[0] = jnp.dot(p.astype(BF16), v_ref[0], preferred_element_type=F32).astype(o_ref.dtype)


def _memory_attention(proj3, kv3, *, n_heads, q_col, hd):
    B, S, _ = proj3.shape
    M = kv3.shape[1]
    tq = MEM_TQ
    return pl.pallas_call(
        _mem_attn_kernel,
        grid=(B, n_heads, S // tq),
        in_specs=[pl.BlockSpec((1, tq, hd), lambda b, h, i: (b, i, q_col + h)),
                  pl.BlockSpec((1, M, hd), lambda b, h, i: (b, 0, h)),
                  pl.BlockSpec((1, M, hd), lambda b, h, i: (b, 0, n_heads + h))],
        out_specs=pl.BlockSpec((1, tq, hd), lambda b, h, i: (b, i, h)),
        out_shape=jax.ShapeDtypeStruct((B, S, n_heads * hd), BF16),
        compiler_params=_cparams(("parallel", "parallel", "arbitrary")),
        name="memory_attention",
    )(proj3, kv3, kv3)


def _oproj_kernel(a_ref, c_ref, m_ref, wa_ref, wc_ref, wm_ref, o_ref):
    acc = jnp.dot(a_ref[...], wa_ref[...].astype(BF16), preferred_element_type=F32)
    acc += jnp.dot(c_ref[...], wc_ref[...].astype(BF16), preferred_element_type=F32)
    acc += jnp.dot(m_ref[...], wm_ref[...].astype(BF16), preferred_element_type=F32)
    o_ref[...] = acc


def _out_projection(a, c, m, w_o, *, tm, tn):
    M, Ka = a.shape
    Kc, Km = c.shape[1], m.shape[1]
    assert Kc == Km and Ka % Kc == 0
    N = w_o.shape[1]
    return pl.pallas_call(
        _oproj_kernel,
        grid=(M // tm, N // tn),
        in_specs=[pl.BlockSpec((tm, Ka), lambda i, j: (i, 0)),
                  pl.BlockSpec((tm, Kc), lambda i, j: (i, 0)),
                  pl.BlockSpec((tm, Km), lambda i, j: (i, 0)),
                  pl.BlockSpec((Ka, tn), lambda i, j: (0, j)),
                  pl.BlockSpec((Kc, tn), lambda i, j: (Ka // Kc, j)),
                  pl.BlockSpec((Km, tn), lambda i, j: (Ka // Kc + 1, j))],
        out_specs=pl.BlockSpec((tm, tn), lambda i, j: (i, j)),
        out_shape=jax.ShapeDtypeStruct((M, N), F32),
        compiler_params=_cparams(("parallel", "arbitrary")),
        name="out_projection",
    )(a, c, m, w_o, w_o, w_o)


def _pack_pairs(lo, hi):
    lo_b = lax.bitcast_convert_type(lo.astype(BF16).astype(F32), U32)
    hi_b = lax.bitcast_convert_type(hi.astype(BF16).astype(F32), U32)
    return (lo_b >> 16) | (hi_b & jnp.uint32(0xFFFF0000))


def _unpack_lo(u):
    return lax.bitcast_convert_type(u << 16, F32)


def _unpack_hi(u):
    return lax.bitcast_convert_type(u & jnp.uint32(0xFFFF0000), F32)


def _layer_norm(y, g, b):
    mu = jnp.mean(y, axis=-1, keepdims=True)
    var = jnp.mean(jnp.square(y - mu), axis=-1, keepdims=True)
    return (y - mu) * lax.rsqrt(var + LN_EPS) * g + b


def _ln_router_kernel(x_ref, mix_ref, g_ref, b_ref, wr_ref, br_ref,
                      x1_ref, xp_ref, topi_ref, gate_ref):
    x1 = _layer_norm(DEEPNORM_ALPHA * x_ref[...] + mix_ref[...], g_ref[...], b_ref[...])
    x1_ref[...] = x1
    half = x1.shape[1] // 2
    xp_ref[...] = _pack_pairs(x1[:, :half], x1[:, half:])
    logits = jnp.dot(x1.astype(BF16), wr_ref[...].astype(BF16),
                     preferred_element_type=F32) + br_ref[...]
    tm, ne = logits.shape
    lane = lax.broadcasted_iota(I32, (tm, ne), 1)
    kk = lax.broadcasted_iota(I32, (tm, TOP_K), 1)
    idx = jnp.zeros((tm, TOP_K), I32)
    val = jnp.zeros((tm, TOP_K), F32)
    cur = logits
    for k in range(TOP_K):
        mx = jnp.max(cur, axis=-1, keepdims=True)
        ix = jnp.min(jnp.where(cur == mx, lane, ne), axis=-1, keepdims=True)
        idx = jnp.where(kk == k, ix, idx)
        val = jnp.where(kk == k, mx, val)
        cur = jnp.where(lane == ix, -jnp.inf, cur)
    e = jnp.exp(val - jnp.max(val, axis=-1, keepdims=True))
    topi_ref[...] = idx
    gate_ref[...] = e / jnp.sum(e, axis=-1, keepdims=True)


def _ln_router(x, mix, g, b, w_router, b_router, *, tm):
    T, D = x.shape
    ne = w_router.shape[1]
    row = pl.BlockSpec((tm, D), lambda i: (i, 0))
    vec = pl.BlockSpec((1, D), lambda i: (0, 0))
    return pl.pallas_call(
        _ln_router_kernel,
        grid=(T // tm,),
        in_specs=[row, row, vec, vec,
                  pl.BlockSpec((D, ne), lambda i: (0, 0)),
                  pl.BlockSpec((1, ne), lambda i: (0, 0))],
        out_specs=[row,
                   pl.BlockSpec((tm, D // 2), lambda i: (i, 0)),
                   pl.BlockSpec((tm, TOP_K), lambda i: (i, 0)),
                   pl.BlockSpec((tm, TOP_K), lambda i: (i, 0))],
        out_shape=[jax.ShapeDtypeStruct((T, D), F32),
                   jax.ShapeDtypeStruct((T, D // 2), U32),
                   jax.ShapeDtypeStruct((T, TOP_K), I32),
                   jax.ShapeDtypeStruct((T, TOP_K), F32)],
        compiler_params=_cparams(("parallel",)),
        name="ln_router",
    )(x, mix, g, b, w_router, b_router)


def _rank_kernel(topi_ref, dest_ref, cnt_ref, ps_ref, rank_ref, carry_ref, *, tt, pad):
    ph = pl.program_id(0)
    i = pl.program_id(1)
    lane = lax.broadcasted_iota(I32, (tt, LANES), 1)
    kk = lax.broadcasted_iota(I32, (tt, TOP_K), 1)
    ti = topi_ref[...]
    sel = [lane == ti[:, k:k + 1] for k in range(TOP_K)]
    rows = pl.ds(pl.multiple_of(i * tt, tt), tt)

    @pl.when(ph == 0)
    def _():
        @pl.when(i == 0)
        def _():
            carry_ref[...] = jnp.zeros_like(carry_ref)

        oh = jnp.zeros((tt, LANES), F32)
        for k in range(TOP_K):
            oh = oh + sel[k].astype(F32)
        r = lax.broadcasted_iota(I32, (tt, tt), 0)
        c = lax.broadcasted_iota(I32, (tt, tt), 1)
        lower = (r > c).astype(BF16)
        before = jnp.dot(lower, oh.astype(BF16), preferred_element_type=F32) + carry_ref[0:1, :]
        rank = jnp.zeros((tt, TOP_K), F32)
        for k in range(TOP_K):
            rk = jnp.sum(jnp.where(sel[k], before, 0.0), axis=-1, keepdims=True)
            rank = jnp.where(kk == k, rk, rank)
        rank_ref[rows, :] = rank
        carry_ref[...] = carry_ref[...] + jnp.sum(oh, axis=0, keepdims=True)

    @pl.when(ph == 1)
    def _():
        cnt = carry_ref[...].astype(I32)
        padded = (cnt + (pad - 1)) & jnp.int32(-pad)
        l8 = lax.broadcasted_iota(I32, (8, LANES), 1)
        scan = padded
        for sh in (1, 2, 4, 8, 16, 32, 64):
            scan = scan + jnp.where(l8 >= sh, pltpu.roll(scan, sh, axis=1), 0)
        starts = scan - padded
        cnt_ref[...] = cnt
        ps_ref[...] = starts
        st = starts[0:1, :].astype(F32)
        rank = rank_ref[rows, :]
        dest = jnp.zeros((tt, TOP_K), F32)
        for k in range(TOP_K):
            base = jnp.sum(jnp.where(sel[k], st, 0.0), axis=-1, keepdims=True)
            dest = jnp.where(kk == k, base, dest)
        dest_ref[...] = (dest + rank).astype(I32)


def _rank_assignments(topi):
    T = topi.shape[0]
    tt = RANK_TT
    return pl.pallas_call(
        functools.partial(_rank_kernel, tt=tt, pad=MOE_PAD),
        grid=(2, T // tt),
        in_specs=[pl.BlockSpec((tt, TOP_K), lambda p, i: (i, 0))],
        out_specs=[pl.BlockSpec((tt, TOP_K), lambda p, i: (i * p, 0)),
                   pl.BlockSpec((8, LANES), lambda p, i: (0, 0)),
                   pl.BlockSpec((8, LANES), lambda p, i: (0, 0))],
        out_shape=[jax.ShapeDtypeStruct((T, TOP_K), I32),
                   jax.ShapeDtypeStruct((8, LANES), I32),
                   jax.ShapeDtypeStruct((8, LANES), I32)],
        scratch_shapes=[pltpu.VMEM((T, TOP_K), F32), pltpu.VMEM((8, LANES), F32)],
        compiler_params=_cparams(("arbitrary", "arbitrary")),
        name="rank_assignments",
    )(topi)


def _dispatch_kernel(fillc_ref, misc_ref, dest_ref, xp_ref, xs_ref, zero_ref, fill_sem, row_sem, *, ta):
    i = pl.program_id(0)
    tt = xp_ref.shape[0]

    @pl.when(i == 0)
    def _():
        def fill(e):
            row = pl.multiple_of(fillc_ref[e] * MOE_PAD, MOE_PAD)
            return pltpu.make_async_copy(zero_ref, xs_ref.at[pl.ds(row, MOE_PAD), :], fill_sem)

        def each_group(fn):
            for e in range(N_EXPERTS):
                pl.when(fillc_ref[e] >= 0)(functools.partial(fn, e))

        zero_ref[...] = jnp.zeros_like(zero_ref)
        each_group(lambda e: fill(e).start())
        _fill_tail(misc_ref, zero_ref, xs_ref, fill_sem)
        each_group(lambda e: fill(e).wait())

    def issue(t, carry):
        for k in range(TOP_K):
            pltpu.make_async_copy(xp_ref.at[pl.ds(t, 1), :],
                                  xs_ref.at[pl.ds(dest_ref[0, 0, t * TOP_K + k], 1), :],
                                  row_sem).start()
        return carry

    lax.fori_loop(0, tt, issue, 0, unroll=2)
    for _ in range(ta // tt):
        pltpu.make_async_copy(xp_ref, xs_ref.at[pl.ds(0, tt), :], row_sem).wait()


def _dispatch(dest, xp, fillc, misc, n_rows):
    T, W = xp.shape
    ta = DISPATCH_TA
    na = dest.size // ta
    assert n_rows % MOE_PAD == 0 and ta % TOP_K == 0
    return pl.pallas_call(
        functools.partial(_dispatch_kernel, ta=ta),
        grid_spec=pltpu.PrefetchScalarGridSpec(
            num_scalar_prefetch=2,
            grid=(na,),
            in_specs=[pl.BlockSpec((1, 1, ta), lambda i, fc, ms: (i, 0, 0), memory_space=pltpu.SMEM),
                      pl.BlockSpec((ta // TOP_K, W), lambda i, fc, ms: (i, 0))],
            out_specs=pl.BlockSpec(memory_space=pl.ANY),
            scratch_shapes=[pltpu.VMEM((MOE_PAD, W), U32),
                            pltpu.SemaphoreType.DMA(()), pltpu.SemaphoreType.DMA(())]),
        out_shape=jax.ShapeDtypeStruct((n_rows, W), U32),
        compiler_params=_cparams(("arbitrary",)),
        name="dispatch_rows",
    )(fillc, misc, dest.reshape(na, 1, ta), xp)


class _TileWriter:
    def __init__(self, wr_ref, wc_ref, stage_ref, out_ref, sem, tn):
        self.wr_ref, self.wc_ref = wr_ref, wc_ref
        self.stage_ref, self.out_ref, self.sem, self.tn = stage_ref, out_ref, sem, tn
        n_chunks = stage_ref.shape[1] // MOE_PAD
        self.sizes = [1 << b for b in reversed(range(n_chunks.bit_length()))]

    def _copy(self, slot, w, j, size):
        n = self.wc_ref[w]
        first = pl.multiple_of((n & -(2 * size)) * MOE_PAD, MOE_PAD)
        row = pl.multiple_of(self.wr_ref[w] + first, MOE_PAD)
        col = pl.multiple_of(j * self.tn, self.tn)
        return pltpu.make_async_copy(
            self.stage_ref.at[slot, pl.ds(first, size * MOE_PAD), :],
            self.out_ref.at[pl.ds(row, size * MOE_PAD), pl.ds(col, self.tn)],
            self.sem.at[slot])

    def _each_group(self, w, fn):
        for size in self.sizes:
            pl.when((self.wc_ref[w] & size) != 0)(functools.partial(fn, size))

    def start(self, slot, w, j):
        self._each_group(w, lambda size: self._copy(slot, w, j, size).start())

    def wait(self, slot, w, j):
        self._each_group(w, lambda size: self._copy(slot, w, j, size).wait())

    def retire_older(self, w, j, nj):
        step = w * nj + j

        @pl.when(step >= 2)
        def _():
            wrap = j < 2
            self.wait(step % 2, jnp.where(wrap, w - 1, w), jnp.where(wrap, j + nj - 2, j - 2))

    def drain(self, w, j, nw, nj):
        step = w * nj + j

        @pl.when(step == nw * nj - 1)
        def _():
            self.wait((step + 1) % 2, w, j - 1)
            self.wait(step % 2, w, j)


def _fill_tail(misc_ref, zero_ref, out_ref, sem):
    n_total = out_ref.shape[0] // MOE_PAD

    def fill(c):
        row = pl.multiple_of(c * MOE_PAD, MOE_PAD)
        return pltpu.make_async_copy(zero_ref, out_ref.at[pl.ds(row, MOE_PAD), :], sem)

    def start(c, carry):
        fill(c).start()
        return carry

    def wait(c, carry):
        fill(c).wait()
        return carry

    lax.fori_loop(misc_ref[0], n_total, start, 0)
    lax.fori_loop(misc_ref[0], n_total, wait, 0)


def _moe_up_kernel(we_ref, wr_ref, wc_ref, misc_ref, xs_ref, wg_ref, wu_ref, b_ref,
                   h_ref, xb_ref, wcat_ref, stage_ref, zero_ref, sem, fill_sem):
    w = pl.program_id(0)
    j = pl.program_id(1)
    nw = pl.num_programs(0)
    nj = pl.num_programs(1)
    slot = (w * nj + j) % 2
    writer = _TileWriter(wr_ref, wc_ref, stage_ref, h_ref, sem, stage_ref.shape[2])

    @pl.when((w == 0) & (j == 0))
    def _():
        zero_ref[...] = jnp.zeros_like(zero_ref)
        _fill_tail(misc_ref, zero_ref, h_ref, fill_sem)

    writer.retire_older(w, j, nj)

    @pl.when(wc_ref[w] > 0)
    def _():
        half = xs_ref.shape[1]

        @pl.when(j == 0)
        def _():
            u = xs_ref[...]
            xb_ref[:, :half] = _unpack_lo(u).astype(BF16)
            xb_ref[:, half:] = _unpack_hi(u).astype(BF16)

        tn = wg_ref.shape[2]
        wcat_ref[:, :tn] = wg_ref[0].astype(BF16)
        wcat_ref[:, tn:] = wu_ref[0].astype(BF16)
        col = pl.multiple_of(j * tn, tn)
        bg = b_ref[0, :, pl.ds(col, tn)]
        bu = b_ref[0, :, pl.ds(pl.multiple_of(b_ref.shape[2] // 2 + col, tn), tn)]
        def rows_block(rows):
            r = jnp.dot(xb_ref[rows, :], wcat_ref[...], preferred_element_type=F32)
            gate = r[:, :tn] + bg
            up = r[:, tn:] + bu
            gate = jnp.minimum(gate, SWIGLU_LIMIT)
            up = jnp.clip(up, -SWIGLU_LIMIT, SWIGLU_LIMIT)
            act = (up + 1.0) * (gate * jax.nn.sigmoid(SWIGLU_ALPHA * gate))
            stage_ref[slot, rows, :] = act.astype(stage_ref.dtype)

        def all_rows(n_rows):
            mc = n_rows // MOE_UP_ROW_CHUNKS
            for mi in range(MOE_UP_ROW_CHUNKS):
                rows_block(slice(mi * mc, (mi + 1) * mc))

        all_rows(xb_ref.shape[0])
        writer.start(slot, w, j)

    writer.drain(w, j, nw, nj)


def _moe_up(xs, w_gu, b_gu, we, wr, wc, misc, *, d_ff):
    R, half = xs.shape
    E, D, _ = w_gu.shape
    tm, tn = MOE_TM, MOE_TN_UP
    nj = d_ff // tn
    nw = we.shape[0]
    assert nj >= 2 and R % MOE_PAD == 0 and tm % (16 * MOE_UP_ROW_CHUNKS) == 0
    jeff = lambda w, j, wc: jnp.where(wc[w] > 0, j, nj - 1)
    return pl.pallas_call(
        _moe_up_kernel,
        grid_spec=pltpu.PrefetchScalarGridSpec(
            num_scalar_prefetch=4,
            grid=(misc[1], nj),
            in_specs=[pl.BlockSpec((pl.Element(tm), pl.Element(half)),
                                   lambda w, j, we, wr, wc, ms: (pl.multiple_of(wr[w], MOE_PAD), 0)),
                      pl.BlockSpec((1, D, tn), lambda w, j, we, wr, wc, ms: (we[w], 0, jeff(w, j, wc))),
                      pl.BlockSpec((1, D, tn), lambda w, j, we, wr, wc, ms: (we[w], 0, nj + jeff(w, j, wc))),
                      pl.BlockSpec((1, 1, 2 * nj * tn), lambda w, j, we, wr, wc, ms: (we[w], 0, 0))],
            out_specs=pl.BlockSpec(memory_space=pl.ANY),
            scratch_shapes=[pltpu.VMEM((tm, 2 * half), BF16),
                            pltpu.VMEM((D, 2 * tn), BF16),
                            pltpu.VMEM((2, tm, tn), BF16),
                            pltpu.VMEM((MOE_PAD, d_ff), BF16),
                            pltpu.SemaphoreType.DMA((2,)), pltpu.SemaphoreType.DMA(())]),
        out_shape=jax.ShapeDtypeStruct((R, d_ff), BF16),
        compiler_params=_cparams(("arbitrary", "arbitrary")),
        name="moe_up",
    )(we, wr, wc, misc, xs, w_gu, w_gu, b_gu)


def _moe_down_kernel(we_ref, wr_ref, wc_ref, misc_ref, h_ref, w_ref, b_ref,
                     o_ref, stage_ref, zero_ref, sem, fill_sem):
    w = pl.program_id(0)
    j = pl.program_id(1)
    nw = pl.num_programs(0)
    nj = pl.num_programs(1)
    slot = (w * nj + j) % 2
    writer = _TileWriter(wr_ref, wc_ref, stage_ref, o_ref, sem, stage_ref.shape[2])

    @pl.when((w == 0) & (j == 0))
    def _():
        zero_ref[...] = jnp.zeros_like(zero_ref)
        _fill_tail(misc_ref, zero_ref, o_ref, fill_sem)

    writer.retire_older(w, j, nj)

    @pl.when(wc_ref[w] > 0)
    def _():
        tn = stage_ref.shape[2]
        bias = b_ref[0, :, pl.ds(pl.multiple_of(j * 2 * tn, 2 * tn), 2 * tn)]

        def rows_block(rows):
            r = jnp.dot(h_ref[rows, :], w_ref[0].astype(BF16), preferred_element_type=F32) + bias
            stage_ref[slot, rows, :] = _pack_pairs(r[:, :tn], r[:, tn:])

        tm, main = h_ref.shape[0], min(MOE_MAIN_ROWS, h_ref.shape[0])
        if tm > main:
            fits = wc_ref[w] * MOE_PAD <= main
            pl.when(fits)(functools.partial(rows_block, slice(0, main)))
            pl.when(jnp.logical_not(fits))(functools.partial(rows_block, slice(0, tm)))
        else:
            rows_block(slice(0, tm))
        writer.start(slot, w, j)

    writer.drain(w, j, nw, nj)


def _moe_down(h, w_down, b_down, we, wr, wc, misc):
    R, d_ff = h.shape
    E, _, D = w_down.shape
    tm, tn = MOE_TM, MOE_TN_DOWN
    half = D // 2
    nj = half // tn
    nw = we.shape[0]
    assert nj >= 2 and R % MOE_PAD == 0
    jeff = lambda w, j, wc: jnp.where(wc[w] > 0, j, nj - 1)
    return pl.pallas_call(
        _moe_down_kernel,
        grid_spec=pltpu.PrefetchScalarGridSpec(
            num_scalar_prefetch=4,
            grid=(misc[1], nj),
            in_specs=[pl.BlockSpec((pl.Element(tm), pl.Element(d_ff)),
                                   lambda w, j, we, wr, wc, ms: (pl.multiple_of(wr[w], MOE_PAD), 0)),
                      pl.BlockSpec((1, d_ff, 2 * tn), lambda w, j, we, wr, wc, ms: (we[w], 0, jeff(w, j, wc))),
                      pl.BlockSpec((1, 1, 2 * nj * tn), lambda w, j, we, wr, wc, ms: (we[w], 0, 0))],
            out_specs=pl.BlockSpec(memory_space=pl.ANY),
            scratch_shapes=[pltpu.VMEM((2, tm, tn), U32),
                            pltpu.VMEM((MOE_PAD, half), U32),
                            pltpu.SemaphoreType.DMA((2,)), pltpu.SemaphoreType.DMA(())]),
        out_shape=jax.ShapeDtypeStruct((R, half), U32),
        compiler_params=_cparams(("arbitrary", "arbitrary")),
        name="moe_down",
    )(we, wr, wc, misc, h, w_down, b_down)


def _combine_kernel(dest_ref, dnext_ref, eo_ref, gate_ref, x1_ref, g_ref, b_ref, o_ref, buf_ref, sem,
                    *, tc):
    i = pl.program_id(0)
    n = tc * TOP_K
    slot = i % 2

    def gather(d_ref, slot):
        def issue(t, carry):
            for k in range(TOP_K):
                pltpu.make_async_copy(eo_ref.at[pl.ds(d_ref[0, 0, t * TOP_K + k], 1), :],
                                      buf_ref.at[slot, pl.ds(k * tc + t, 1), :],
                                      sem.at[slot]).start()
            return carry

        lax.fori_loop(0, tc, issue, 0, unroll=2)

    @pl.when(i == 0)
    def _():
        gather(dest_ref, 0)

    @pl.when(i + 1 < pl.num_programs(0))
    def _():
        gather(dnext_ref, 1 - slot)

    pltpu.make_async_copy(eo_ref.at[pl.ds(0, n), :], buf_ref.at[slot], sem.at[slot]).wait()

    gates = gate_ref[...]
    half = buf_ref.shape[2]
    cw = 2 * LANES
    row_sum = jnp.zeros((tc, 1), F32)
    for c in range(half // cw):
        tn = MOE_TN_DOWN
        u_cols = slice(c * cw, (c + 1) * cw)
        first = (c * cw // tn) * 2 * tn + c * cw % tn
        lo_cols = slice(first, first + cw)
        hi_cols = slice(first + tn, first + tn + cw)
        lo = DEEPNORM_ALPHA * x1_ref[:, lo_cols]
        hi = DEEPNORM_ALPHA * x1_ref[:, hi_cols]
        for k in range(TOP_K):
            u = buf_ref[slot, k * tc:(k + 1) * tc, u_cols]
            gk = gates[:, k:k + 1]
            lo = lo + gk * _unpack_lo(u)
            hi = hi + gk * _unpack_hi(u)
        o_ref[:, lo_cols] = lo
        o_ref[:, hi_cols] = hi
        row_sum = row_sum + jnp.sum(lo, axis=-1, keepdims=True) + jnp.sum(hi, axis=-1, keepdims=True)
    inv_d = 1.0 / (2 * half)
    mu = row_sum * inv_d
    sq_sum = jnp.zeros((tc, 1), F32)
    for c in range(2 * half // cw):
        d = o_ref[:, c * cw:(c + 1) * cw] - mu
        sq_sum = sq_sum + jnp.sum(d * d, axis=-1, keepdims=True)
    rstd = lax.rsqrt(sq_sum * inv_d + LN_EPS)
    for c in range(2 * half // cw):
        cols = slice(c * cw, (c + 1) * cw)
        o_ref[:, cols] = (o_ref[:, cols] - mu) * rstd * g_ref[:, cols] + b_ref[:, cols]


def _combine(dest, eo, gates, x1, g, b):
    T, D = x1.shape
    tc = COMBINE_TC
    n = tc * TOP_K
    nt = T // tc
    half = eo.shape[1]
    row = pl.BlockSpec((tc, D), lambda i: (i, 0))
    vec = pl.BlockSpec((1, D), lambda i: (0, 0))
    dest3 = dest.reshape(nt, 1, n)
    return pl.pallas_call(
        functools.partial(_combine_kernel, tc=tc),
        grid=(nt,),
        in_specs=[pl.BlockSpec((1, 1, n), lambda i: (i, 0, 0), memory_space=pltpu.SMEM),
                  pl.BlockSpec((1, 1, n), lambda i: (jnp.minimum(i + 1, nt - 1), 0, 0),
                               memory_space=pltpu.SMEM),
                  pl.BlockSpec(memory_space=pl.ANY),
                  pl.BlockSpec((tc, TOP_K), lambda i: (i, 0)),
                  row, vec, vec],
        out_specs=row,
        out_shape=jax.ShapeDtypeStruct((T, D), F32),
        scratch_shapes=[pltpu.VMEM((2, n, half), U32), pltpu.SemaphoreType.DMA((2,))],
        compiler_params=_cparams(("arbitrary",)),
        name="combine_rows",
    )(dest3, dest3, eo, gates, x1, g, b)


def _work_items(cnt, starts, n_items):
    chunks_per_pass = MOE_TM // MOE_PAD
    gchunks = (cnt + MOE_PAD - 1) // MOE_PAD
    npass = (gchunks + chunks_per_pass - 1) // chunks_per_pass
    cum = jnp.cumsum(npass)
    total = cum[-1]
    w = jnp.arange(n_items, dtype=I32)
    e = jnp.minimum(jnp.sum((cum[None, :] <= w[:, None]).astype(I32), axis=1), N_EXPERTS - 1)
    p = w - (cum - npass)[e]
    row = starts[e] + p * MOE_TM
    nch = jnp.clip(gchunks[e] - p * chunks_per_pass, 0, chunks_per_pass)
    valid = w < total
    last = jnp.maximum(total - 1, 0)
    e = jnp.where(valid, e, e[last]).astype(I32)
    row = jnp.where(valid, row, row[last]).astype(I32)
    nch = jnp.where(valid, nch, 0).astype(I32)
    misc = jnp.stack([jnp.sum(gchunks), total]).astype(I32)
    fillc = jnp.where(cnt % MOE_PAD != 0, starts // MOE_PAD + gchunks - 1, -1).astype(I32)
    return e, row, nch, misc, fillc


def kernel(x, mem, rel_table, w_in, w_mem_kv, w_o, lambda_q1, lambda_k1, lambda_q2, lambda_k2, subln_g, conv_w, conv_b, conv_ln_g, conv_ln_b, ln1_g, ln1_b, w_router, b_router, w_gate_up, b_gate_up, w_down, b_down, ln2_g, ln2_b):
    B, S, D = x.shape
    T = B * S
    n_heads = rel_table.shape[1]
    qk_w = n_heads * 2 * HEAD_DIM
    v_w = n_heads * V_DIM
    C = conv_w.shape[-1]
    mem_w = w_mem_kv.shape[-1] // 2
    mem_heads = 4
    mem_hd = mem_w // mem_heads
    in_cols = w_in.shape[-1]
    assert in_cols == 2 * qk_w + v_w + 2 * C + mem_w
    d_ff = w_down.shape[2]
    assert DEPTH == 1 and w_in.shape[0] == 1

    col_scale = jnp.concatenate([
        jnp.full((qk_w,), HEAD_DIM ** -0.5, F32),
        jnp.ones((in_cols - qk_w - mem_w,), F32),
        jnp.full((mem_w,), mem_hd ** -0.5, F32)]).reshape(1, in_cols)

    xt = x.reshape(T, D)
    proj = _proj_matmul(xt.astype(BF16), w_in[0], col_scale, tm=1024, tn=768, out_dtype=BF16)
    proj3 = proj.reshape(B, S, in_cols)

    bias = _bias_tiles(rel_table, ATT_T)
    lam_params = jnp.concatenate([lambda_q1, lambda_k1, lambda_q2, lambda_k2], axis=0).astype(F32)
    a_out = _diff_attention(proj3, bias, lam_params, subln_g, n_heads=n_heads,
                            k_col=qk_w // (2 * HEAD_DIM), v_col=2 * qk_w // V_DIM)

    conv_col = (2 * qk_w + v_w) // C
    c_out = _conformer_conv(proj3, conv_w[0, :, 0, :], conv_b, conv_ln_g, conv_ln_b,
                            a_col=conv_col, g_col=conv_col + 1)

    n_mem = mem.shape[1]
    kv = _proj_matmul(mem.reshape(B * n_mem, D).astype(BF16), w_mem_kv[0],
                      jnp.ones((1, 2 * mem_w), F32), tm=B * n_mem, tn=512, out_dtype=BF16)
    m_out = _memory_attention(proj3, kv.reshape(B, n_mem, 2 * mem_w), n_heads=mem_heads,
                              q_col=(in_cols - mem_w) // mem_hd, hd=mem_hd)

    mix = _out_projection(a_out.reshape(T, v_w), c_out.reshape(T, C), m_out.reshape(T, mem_w),
                          w_o[0], tm=1024, tn=512)

    x1, xp, topi, gates = _ln_router(xt, mix, ln1_g, ln1_b, w_router[0],
                                     b_router.reshape(1, -1), tm=256)

    dest, cnt8, starts8 = _rank_assignments(topi)
    cnt, starts = cnt8[0, :N_EXPERTS], starts8[0, :N_EXPERTS]

    n_assign = T * TOP_K
    n_items = n_assign // MOE_TM + N_EXPERTS
    max_rows = n_assign + N_EXPERTS * (MOE_PAD - 1) + MOE_TM
    n_rows = -(-max_rows // MOE_PAD) * MOE_PAD
    we, wr, wc, misc, fillc = _work_items(cnt, starts, n_items)

    xs = _dispatch(dest, xp, fillc, misc, n_rows)
    h = _moe_up(xs, w_gate_up[0], b_gate_up[0].reshape(N_EXPERTS, 1, -1), we, wr, wc, misc, d_ff=d_ff)
    eo = _moe_down(h, w_down[0], b_down[0].reshape(N_EXPERTS, 1, -1), we, wr, wc, misc)
    out = _combine(dest, eo, gates, x1, ln2_g, ln2_b)
    return out.reshape(B, S, D)
```

```python
import functools
import math

import jax
import jax.numpy as jnp
from jax import lax
from jax.experimental import pallas as pl
from jax.experimental.pallas import tpu as pltpu

F32 = jnp.float32
BF16 = jnp.bfloat16
I32 = jnp.int32
U32 = jnp.uint32

HEAD_DIM = 128
V_DIM = 2 * HEAD_DIM
CONV_TAPS = 31
NUM_BUCKETS = 32
MAX_EXACT = NUM_BUCKETS // 2
MAX_DISTANCE = 128
N_EXPERTS = 32
TOP_K = 4
SWIGLU_LIMIT = 7.0
SWIGLU_ALPHA = 1.702
LN_EPS = 1e-5
DEPTH = 1
DEEPNORM_ALPHA = (2 * DEPTH) ** 0.25
LAM_INIT = 0.8 - 0.6 * math.exp(-0.3 * 0)

LANES = 128
V7X_VMEM_BYTES = 64 * 1024 * 1024
VMEM_LIMIT = 58 * 1024 * 1024

ATT_T = 256
CONV_TS = 256
CONV_HALO = 32
MEM_TQ = 512
MOE_TM = 1088
MOE_MAIN_ROWS = 1024
MOE_PAD = 64
MOE_TN_UP = 256
MOE_UP_ROW_CHUNKS = 2
MOE_TN_DOWN = 512
RANK_TT = 512
DISPATCH_TA = 2048
COMBINE_TC = 256


def _cparams(sem, vmem=VMEM_LIMIT):
    return pltpu.CompilerParams(dimension_semantics=sem, vmem_limit_bytes=vmem)


def _proj_kernel(a_ref, w_ref, cs_ref, o_ref):
    acc = jnp.dot(a_ref[...], w_ref[...].astype(BF16), preferred_element_type=F32)
    o_ref[...] = (acc * cs_ref[...]).astype(o_ref.dtype)


def _proj_matmul(a, w, col_scale, *, tm, tn, out_dtype):
    M, K = a.shape
    N = w.shape[1]
    return pl.pallas_call(
        _proj_kernel,
        grid=(M // tm, N // tn),
        in_specs=[pl.BlockSpec((tm, K), lambda i, j: (i, 0)),
                  pl.BlockSpec((K, tn), lambda i, j: (0, j)),
                  pl.BlockSpec((1, tn), lambda i, j: (0, j))],
        out_specs=pl.BlockSpec((tm, tn), lambda i, j: (i, j)),
        out_shape=jax.ShapeDtypeStruct((M, N), out_dtype),
        compiler_params=_cparams(("parallel", "arbitrary")),
        name="proj_matmul",
    )(a, w, col_scale)


def _bias_kernel(tbl_ref, o_ref, *, t, n_heads):
    h = pl.program_id(0)
    j = lax.broadcasted_iota(I32, (t, t), 0)
    i = lax.broadcasted_iota(I32, (t, t), 1)
    for o in range(2):
        dist = o * t + i - j
        n = jnp.maximum(dist, 0)
        nf = jnp.maximum(n, 1).astype(F32)
        large = MAX_EXACT + (jnp.log(nf / MAX_EXACT) / math.log(MAX_DISTANCE / MAX_EXACT)
                             * (NUM_BUCKETS - MAX_EXACT)).astype(I32)
        large = jnp.minimum(large, NUM_BUCKETS - 1)
        bucket = jnp.where(n < MAX_EXACT, n, large)
        val = jnp.zeros((t, t), F32)
        for b in range(NUM_BUCKETS):
            val = jnp.where(bucket == b, tbl_ref[b * n_heads + h], val)
        if o == 0:
            val = jnp.where(dist >= 0, val, -jnp.inf)
        o_ref[0, o] = val
    o_ref[0, 2] = jnp.full((t, t), tbl_ref[(NUM_BUCKETS - 1) * n_heads + h], F32)


def _bias_tiles(rel_table, t):
    n_heads = rel_table.shape[1]
    assert t + 1 >= MAX_DISTANCE
    return pl.pallas_call(
        functools.partial(_bias_kernel, t=t, n_heads=n_heads),
        grid=(n_heads,),
        in_specs=[pl.BlockSpec(memory_space=pltpu.SMEM)],
        out_specs=pl.BlockSpec((1, 3, t, t), lambda h: (h, 0, 0, 0)),
        out_shape=jax.ShapeDtypeStruct((n_heads, 3, t, t), F32),
        compiler_params=_cparams(("arbitrary",)),
        name="bias_tiles",
    )(rel_table.reshape(-1))


def _attn_kernel(lam_ref, q_ref, k_ref, v_ref, bias_ref, g_ref, o_ref, vt_ref, acc_ref, m_ref, l_ref,
                 *, t):
    qi = pl.program_id(2)
    n_kt = v_ref.shape[1] // t

    @pl.when(qi == 0)
    def _():
        for kb in range(n_kt):
            vt_ref[:, kb * t:(kb + 1) * t] = v_ref[0, kb * t:(kb + 1) * t, :].astype(F32).T.astype(BF16)

    q = q_ref[0]
    qs = (q[:, :HEAD_DIM], q[:, HEAD_DIM:])
    dn = (((1,), (1,)), ((), ()))
    acc_ref[...] = jnp.zeros_like(acc_ref)
    m_ref[...] = jnp.full(m_ref.shape, -jnp.inf, F32)
    l_ref[...] = jnp.zeros_like(l_ref)

    def update(kb0, n_tiles):
        start = pl.multiple_of(kb0 * t, t)
        k = k_ref[0, pl.ds(start, n_tiles * t), :]
        vt = vt_ref[:, pl.ds(start, n_tiles * t)]
        bias = [bias_ref[0, jnp.minimum(qi - kb0 - i, 2)] for i in range(n_tiles)]
        for c in range(2):
            s = [lax.dot_general(k[i * t:(i + 1) * t, c * HEAD_DIM:(c + 1) * HEAD_DIM], qs[c], dn,
                                 preferred_element_type=F32) + bias[i]
                 for i in range(n_tiles)]
            m_old = m_ref[c]
            m_new = m_old
            for si in s:
                m_new = jnp.maximum(m_new, jnp.max(si, axis=0, keepdims=True))
            alpha = jnp.exp(m_old - m_new)
            p = [jnp.exp(si - m_new) for si in s]
            l_new = alpha * l_ref[c]
            for pi in p:
                l_new = l_new + jnp.sum(pi, axis=0, keepdims=True)
            pcat = p[0].astype(BF16) if n_tiles == 1 else jnp.concatenate(
                [pi.astype(BF16) for pi in p], axis=0)
            acc_ref[c] = alpha * acc_ref[c] + jnp.dot(vt, pcat, preferred_element_type=F32)
            m_ref[c] = m_new
            l_ref[c] = l_new

    n = qi + 1

    def pair(i, carry):
        update(2 * i, 2)
        return carry

    lax.fori_loop(0, n // 2, pair, 0)

    @pl.when(n % 2 == 1)
    def _():
        update(qi, 1)

    lp = lam_ref[...]
    lam = (jnp.exp(jnp.sum(lp[0:1] * lp[1:2], axis=-1, keepdims=True))
           - jnp.exp(jnp.sum(lp[2:3] * lp[3:4], axis=-1, keepdims=True)) + LAM_INIT)
    ot = acc_ref[0] * (1.0 / l_ref[0]) - acc_ref[1] * (lam / l_ref[1])
    ot = ot * lax.rsqrt(jnp.mean(jnp.square(ot), axis=0, keepdims=True) + LN_EPS)
    o = ot.T * g_ref[...]
    o_ref[0] = (o * (1.0 - LAM_INIT)).astype(o_ref.dtype)


def _diff_attention(proj3, bias, lam_params, subln_g, *, n_heads, k_col, v_col):
    B, S, _ = proj3.shape
    t = ATT_T
    nq = S // t
    w = 2 * HEAD_DIM
    return pl.pallas_call(
        functools.partial(_attn_kernel, t=t),
        grid=(B, n_heads, nq),
        in_specs=[pl.BlockSpec((4, HEAD_DIM), lambda b, h, i: (0, 0)),
                  pl.BlockSpec((1, t, w), lambda b, h, i: (b, i, h)),
                  pl.BlockSpec((1, S, w), lambda b, h, i: (b, 0, k_col + h)),
                  pl.BlockSpec((1, S, w), lambda b, h, i: (b, 0, v_col + h)),
                  pl.BlockSpec((1, 3, t, t), lambda b, h, i: (h, 0, 0, 0)),
                  pl.BlockSpec((1, V_DIM), lambda b, h, i: (0, 0))],
        out_specs=pl.BlockSpec((1, t, V_DIM), lambda b, h, i: (b, i, h)),
        out_shape=jax.ShapeDtypeStruct((B, S, n_heads * V_DIM), BF16),
        scratch_shapes=[pltpu.VMEM((V_DIM, S), BF16), pltpu.VMEM((2, V_DIM, t), F32),
                        pltpu.VMEM((2, 1, t), F32), pltpu.VMEM((2, 1, t), F32)],
        compiler_params=_cparams(("parallel", "parallel", "arbitrary")),
        name="diff_attention",
    )(lam_params, proj3, proj3, proj3, bias, subln_g)


def _conv_kernel(a_ref, g_ref, ap_ref, gp_ref, w_ref, cb_ref, lg_ref, lb_ref, o_ref,
                 hs_ref, cv_ref, *, ts, halo, taps, rc):
    i = pl.program_id(1)
    C = a_ref.shape[-1]
    hs_ref[halo:, :] = a_ref[0].astype(F32) * jax.nn.sigmoid(g_ref[0].astype(F32))
    hp = ap_ref[0].astype(F32) * jax.nn.sigmoid(gp_ref[0].astype(F32))
    hs_ref[:halo, :] = jnp.where(i > 0, hp, 0.0)
    base = halo - (taps - 1)
    for c in range(C // LANES):
        cs = slice(c * LANES, (c + 1) * LANES)
        for r in range(ts // rc):
            acc = jnp.zeros((rc, LANES), F32)
            for tp in range(taps):
                acc = acc + w_ref[tp:tp + 1, cs] * hs_ref[pl.ds(base + tp + r * rc, rc), cs]
            cv_ref[r * rc:(r + 1) * rc, cs] = acc + cb_ref[:, cs]
    h = cv_ref[...]
    mu = jnp.mean(h, axis=-1, keepdims=True)
    var = jnp.mean(jnp.square(h - mu), axis=-1, keepdims=True)
    y = (h - mu) * lax.rsqrt(var + LN_EPS) * lg_ref[...] + lb_ref[...]
    o_ref[0] = (y * jax.nn.sigmoid(y)).astype(o_ref.dtype)


def _conformer_conv(proj3, conv_w, conv_b, ln_g, ln_b, *, a_col, g_col):
    B, S, _ = proj3.shape
    taps, C = conv_w.shape
    ts, halo = CONV_TS, CONV_HALO
    assert halo >= taps - 1
    hb = ts // halo
    prev = lambda col: (lambda b, i: (b, jnp.maximum(i * hb - 1, 0), col))
    vec = pl.BlockSpec((1, C), lambda b, i: (0, 0))
    return pl.pallas_call(
        functools.partial(_conv_kernel, ts=ts, halo=halo, taps=taps, rc=128),
        grid=(B, S // ts),
        in_specs=[pl.BlockSpec((1, ts, C), lambda b, i: (b, i, a_col)),
                  pl.BlockSpec((1, ts, C), lambda b, i: (b, i, g_col)),
                  pl.BlockSpec((1, halo, C), prev(a_col)),
                  pl.BlockSpec((1, halo, C), prev(g_col)),
                  pl.BlockSpec((taps, C), lambda b, i: (0, 0)),
                  vec, vec, vec],
        out_specs=pl.BlockSpec((1, ts, C), lambda b, i: (b, i, 0)),
        out_shape=jax.ShapeDtypeStruct((B, S, C), BF16),
        scratch_shapes=[pltpu.VMEM((halo + ts, C), F32), pltpu.VMEM((ts, C), F32)],
        compiler_params=_cparams(("parallel", "arbitrary")),
        name="conformer_conv",
    )(proj3, proj3, proj3, proj3, conv_w, conv_b, ln_g, ln_b)


def _mem_attn_kernel(q_ref, k_ref, v_ref, o_ref):
    s = lax.dot_general(q_ref[0], k_ref[0], (((1,), (1,)), ((), ())), preferred_element_type=F32)
    m = jnp.max(s, axis=-1, keepdims=True)
    p = jnp.exp(s - m)
    p = p / jnp.sum(p, axis=-1, keepdims=True)
    o_ref[0] = jnp.dot(p.astype(BF16), v_ref[0], preferred_element_type=F32).astype(o_ref.dtype)


def _memory_attention(proj3, kv3, *, n_heads, q_col, hd):
    B, S, _ = proj3.shape
    M = kv3.shape[1]
    tq = MEM_TQ
    return pl.pallas_call(
        _mem_attn_kernel,
        grid=(B, n_heads, S // tq),
        in_specs=[pl.BlockSpec((1, tq, hd), lambda b, h, i: (b, i, q_col + h)),
                  pl.BlockSpec((1, M, hd), lambda b, h, i: (b, 0, h)),
                  pl.BlockSpec((1, M, hd), lambda b, h, i: (b, 0, n_heads + h))],
        out_specs=pl.BlockSpec((1, tq, hd), lambda b, h, i: (b, i, h)),
        out_shape=jax.ShapeDtypeStruct((B, S, n_heads * hd), BF16),
        compiler_params=_cparams(("parallel", "parallel", "arbitrary")),
        name="memory_attention",
    )(proj3, kv3, kv3)


def _oproj_kernel(a_ref, c_ref, m_ref, wa_ref, wc_ref, wm_ref, o_ref):
    acc = jnp.dot(a_ref[...], wa_ref[...].astype(BF16), preferred_element_type=F32)
    acc += jnp.dot(c_ref[...], wc_ref[...].astype(BF16), preferred_element_type=F32)
    acc += jnp.dot(m_ref[...], wm_ref[...].astype(BF16), preferred_element_type=F32)
    o_ref[...] = acc


def _out_projection(a, c, m, w_o, *, tm, tn):
    M, Ka = a.shape
    Kc, Km = c.shape[1], m.shape[1]
    assert Kc == Km and Ka % Kc == 0
    N = w_o.shape[1]
    return pl.pallas_call(
        _oproj_kernel,
        grid=(M // tm, N // tn),
        in_specs=[pl.BlockSpec((tm, Ka), lambda i, j: (i, 0)),
                  pl.BlockSpec((tm, Kc), lambda i, j: (i, 0)),
                  pl.BlockSpec((tm, Km), lambda i, j: (i, 0)),
                  pl.BlockSpec((Ka, tn), lambda i, j: (0, j)),
                  pl.BlockSpec((Kc, tn), lambda i, j: (Ka // Kc, j)),
                  pl.BlockSpec((Km, tn), lambda i, j: (Ka // Kc + 1, j))],
        out_specs=pl.BlockSpec((tm, tn), lambda i, j: (i, j)),
        out_shape=jax.ShapeDtypeStruct((M, N), F32),
        compiler_params=_cparams(("parallel", "arbitrary")),
        name="out_projection",
    )(a, c, m, w_o, w_o, w_o)


def _pack_pairs(lo, hi):
    lo_b = lax.bitcast_convert_type(lo.astype(BF16).astype(F32), U32)
    hi_b = lax.bitcast_convert_type(hi.astype(BF16).astype(F32), U32)
    return (lo_b >> 16) | (hi_b & jnp.uint32(0xFFFF0000))


def _unpack_lo(u):
    return lax.bitcast_convert_type(u << 16, F32)


def _unpack_hi(u):
    return lax.bitcast_convert_type(u & jnp.uint32(0xFFFF0000), F32)


def _layer_norm(y, g, b):
    mu = jnp.mean(y, axis=-1, keepdims=True)
    var = jnp.mean(jnp.square(y - mu), axis=-1, keepdims=True)
    return (y - mu) * lax.rsqrt(var + LN_EPS) * g + b


def _ln_router_kernel(x_ref, mix_ref, g_ref, b_ref, wr_ref, br_ref,
                      x1_ref, xp_ref, topi_ref, gate_ref):
    x1 = _layer_norm(DEEPNORM_ALPHA * x_ref[...] + mix_ref[...], g_ref[...], b_ref[...])
    x1_ref[...] = x1
    half = x1.shape[1] // 2
    xp_ref[...] = _pack_pairs(x1[:, :half], x1[:, half:])
    logits = jnp.dot(x1.astype(BF16), wr_ref[...].astype(BF16),
                     preferred_element_type=F32) + br_ref[...]
    tm, ne = logits.shape
    lane = lax.broadcasted_iota(I32, (tm, ne), 1)
    kk = lax.broadcasted_iota(I32, (tm, TOP_K), 1)
    idx = jnp.zeros((tm, TOP_K), I32)
    val = jnp.zeros((tm, TOP_K), F32)
    cur = logits
    for k in range(TOP_K):
        mx = jnp.max(cur, axis=-1, keepdims=True)
        ix = jnp.min(jnp.where(cur == mx, lane, ne), axis=-1, keepdims=True)
        idx = jnp.where(kk == k, ix, idx)
        val = jnp.where(kk == k, mx, val)
        cur = jnp.where(lane == ix, -jnp.inf, cur)
    e = jnp.exp(val - jnp.max(val, axis=-1, keepdims=True))
    topi_ref[...] = idx
    gate_ref[...] = e / jnp.sum(e, axis=-1, keepdims=True)


def _ln_router(x, mix, g, b, w_router, b_router, *, tm):
    T, D = x.shape
    ne = w_router.shape[1]
    row = pl.BlockSpec((tm, D), lambda i: (i, 0))
    vec = pl.BlockSpec((1, D), lambda i: (0, 0))
    return pl.pallas_call(
        _ln_router_kernel,
        grid=(T // tm,),
        in_specs=[row, row, vec, vec,
                  pl.BlockSpec((D, ne), lambda i: (0, 0)),
                  pl.BlockSpec((1, ne), lambda i: (0, 0))],
        out_specs=[row,
                   pl.BlockSpec((tm, D // 2), lambda i: (i, 0)),
                   pl.BlockSpec((tm, TOP_K), lambda i: (i, 0)),
                   pl.BlockSpec((tm, TOP_K), lambda i: (i, 0))],
        out_shape=[jax.ShapeDtypeStruct((T, D), F32),
                   jax.ShapeDtypeStruct((T, D // 2), U32),
                   jax.ShapeDtypeStruct((T, TOP_K), I32),
                   jax.ShapeDtypeStruct((T, TOP_K), F32)],
        compiler_params=_cparams(("parallel",)),
        name="ln_router",
    )(x, mix, g, b, w_router, b_router)


def _rank_kernel(topi_ref, dest_ref, cnt_ref, ps_ref, rank_ref, carry_ref, *, tt, pad):
    ph = pl.program_id(0)
    i = pl.program_id(1)
    lane = lax.broadcasted_iota(I32, (tt, LANES), 1)
    kk = lax.broadcasted_iota(I32, (tt, TOP_K), 1)
    ti = topi_ref[...]
    sel = [lane == ti[:, k:k + 1] for k in range(TOP_K)]
    rows = pl.ds(pl.multiple_of(i * tt, tt), tt)

    @pl.when(ph == 0)
    def _():
        @pl.when(i == 0)
        def _():
            carry_ref[...] = jnp.zeros_like(carry_ref)

        oh = jnp.zeros((tt, LANES), F32)
        for k in range(TOP_K):
            oh = oh + sel[k].astype(F32)
        r = lax.broadcasted_iota(I32, (tt, tt), 0)
        c = lax.broadcasted_iota(I32, (tt, tt), 1)
        lower = (r > c).astype(BF16)
        before = jnp.dot(lower, oh.astype(BF16), preferred_element_type=F32) + carry_ref[0:1, :]
        rank = jnp.zeros((tt, TOP_K), F32)
        for k in range(TOP_K):
            rk = jnp.sum(jnp.where(sel[k], before, 0.0), axis=-1, keepdims=True)
            rank = jnp.where(kk == k, rk, rank)
        rank_ref[rows, :] = rank
        carry_ref[...] = carry_ref[...] + jnp.sum(oh, axis=0, keepdims=True)

    @pl.when(ph == 1)
    def _():
        cnt = carry_ref[...].astype(I32)
        padded = (cnt + (pad - 1)) & jnp.int32(-pad)
        l8 = lax.broadcasted_iota(I32, (8, LANES), 1)
        scan = padded
        for sh in (1, 2, 4, 8, 16, 32, 64):
            scan = scan + jnp.where(l8 >= sh, pltpu.roll(scan, sh, axis=1), 0)
        starts = scan - padded
        cnt_ref[...] = cnt
        ps_ref[...] = starts
        st = starts[0:1, :].astype(F32)
        rank = rank_ref[rows, :]
        dest = jnp.zeros((tt, TOP_K), F32)
        for k in range(TOP_K):
            base = jnp.sum(jnp.where(sel[k], st, 0.0), axis=-1, keepdims=True)
            dest = jnp.where(kk == k, base, dest)
        dest_ref[...] = (dest + rank).astype(I32)


def _rank_assignments(topi):
    T = topi.shape[0]
    tt = RANK_TT
    return pl.pallas_call(
        functools.partial(_rank_kernel, tt=tt, pad=MOE_PAD),
        grid=(2, T // tt),
        in_specs=[pl.BlockSpec((tt, TOP_K), lambda p, i: (i, 0))],
        out_specs=[pl.BlockSpec((tt, TOP_K), lambda p, i: (i * p, 0)),
                   pl.BlockSpec((8, LANES), lambda p, i: (0, 0)),
                   pl.BlockSpec((8, LANES), lambda p, i: (0, 0))],
        out_shape=[jax.ShapeDtypeStruct((T, TOP_K), I32),
                   jax.ShapeDtypeStruct((8, LANES), I32),
                   jax.ShapeDtypeStruct((8, LANES), I32)],
        scratch_shapes=[pltpu.VMEM((T, TOP_K), F32), pltpu.VMEM((8, LANES), F32)],
        compiler_params=_cparams(("arbitrary", "arbitrary")),
        name="rank_assignments",
    )(topi)


def _dispatch_kernel(fillc_ref, misc_ref, dest_ref, xp_ref, xs_ref, zero_ref, fill_sem, row_sem, *, ta):
    i = pl.program_id(0)
    tt = xp_ref.shape[0]

    @pl.when(i == 0)
    def _():
        def fill(e):
            row = pl.multiple_of(fillc_ref[e] * MOE_PAD, MOE_PAD)
            return pltpu.make_async_copy(zero_ref, xs_ref.at[pl.ds(row, MOE_PAD), :], fill_sem)

        def each_group(fn):
            for e in range(N_EXPERTS):
                pl.when(fillc_ref[e] >= 0)(functools.partial(fn, e))

        zero_ref[...] = jnp.zeros_like(zero_ref)
        each_group(lambda e: fill(e).start())
        _fill_tail(misc_ref, zero_ref, xs_ref, fill_sem)
        each_group(lambda e: fill(e).wait())

    def issue(t, carry):
        for k in range(TOP_K):
            pltpu.make_async_copy(xp_ref.at[pl.ds(t, 1), :],
                                  xs_ref.at[pl.ds(dest_ref[0, 0, t * TOP_K + k], 1), :],
                                  row_sem).start(priority=k % 2)
        return carry

    lax.fori_loop(0, tt, issue, 0, unroll=2)
    for _ in range(ta // tt):
        pltpu.make_async_copy(xp_ref, xs_ref.at[pl.ds(0, tt), :], row_sem).wait()


def _dispatch(dest, xp, fillc, misc, n_rows):
    T, W = xp.shape
    ta = DISPATCH_TA
    na = dest.size // ta
    assert n_rows % MOE_PAD == 0 and ta % TOP_K == 0
    return pl.pallas_call(
        functools.partial(_dispatch_kernel, ta=ta),
        grid_spec=pltpu.PrefetchScalarGridSpec(
            num_scalar_prefetch=2,
            grid=(na,),
            in_specs=[pl.BlockSpec((1, 1, ta), lambda i, fc, ms: (i, 0, 0), memory_space=pltpu.SMEM),
                      pl.BlockSpec((ta // TOP_K, W), lambda i, fc, ms: (i, 0))],
            out_specs=pl.BlockSpec(memory_space=pl.ANY),
            scratch_shapes=[pltpu.VMEM((MOE_PAD, W), U32),
                            pltpu.SemaphoreType.DMA(()), pltpu.SemaphoreType.DMA(())]),
        out_shape=jax.ShapeDtypeStruct((n_rows, W), U32),
        compiler_params=_cparams(("arbitrary",)),
        name="dispatch_rows",
    )(fillc, misc, dest.reshape(na, 1, ta), xp)


class _TileWriter:
    def __init__(self, wr_ref, wc_ref, stage_ref, out_ref, sem, tn):
        self.wr_ref, self.wc_ref = wr_ref, wc_ref
        self.stage_ref, self.out_ref, self.sem, self.tn = stage_ref, out_ref, sem, tn
        n_chunks = stage_ref.shape[1] // MOE_PAD
        self.sizes = [1 << b for b in reversed(range(n_chunks.bit_length()))]

    def _copy(self, slot, w, j, size):
        n = self.wc_ref[w]
        first = pl.multiple_of((n & -(2 * size)) * MOE_PAD, MOE_PAD)
        row = pl.multiple_of(self.wr_ref[w] + first, MOE_PAD)
        col = pl.multiple_of(j * self.tn, self.tn)
        return pltpu.make_async_copy(
            self.stage_ref.at[slot, pl.ds(first, size * MOE_PAD), :],
            self.out_ref.at[pl.ds(row, size * MOE_PAD), pl.ds(col, self.tn)],
            self.sem.at[slot])

    def _each_group(self, w, fn):
        for size in self.sizes:
            pl.when((self.wc_ref[w] & size) != 0)(functools.partial(fn, size))

    def start(self, slot, w, j):
        self._each_group(w, lambda size: self._copy(slot, w, j, size).start())

    def wait(self, slot, w, j):
        self._each_group(w, lambda size: self._copy(slot, w, j, size).wait())

    def retire_older(self, w, j, nj):
        step = w * nj + j

        @pl.when(step >= 2)
        def _():
            wrap = j < 2
            self.wait(step % 2, jnp.where(wrap, w - 1, w), jnp.where(wrap, j + nj - 2, j - 2))

    def drain(self, w, j, nw, nj):
        step = w * nj + j

        @pl.when(step == nw * nj - 1)
        def _():
            self.wait((step + 1) % 2, w, j - 1)
            self.wait(step % 2, w, j)


def _fill_tail(misc_ref, zero_ref, out_ref, sem):
    n_total = out_ref.shape[0] // MOE_PAD

    def fill(c):
        row = pl.multiple_of(c * MOE_PAD, MOE_PAD)
        return pltpu.make_async_copy(zero_ref, out_ref.at[pl.ds(row, MOE_PAD), :], sem)

    def start(c, carry):
        fill(c).start()
        return carry

    def wait(c, carry):
        fill(c).wait()
        return carry

    lax.fori_loop(misc_ref[0], n_total, start, 0)
    lax.fori_loop(misc_ref[0], n_total, wait, 0)


def _moe_up_kernel(we_ref, wr_ref, wc_ref, misc_ref, xs_ref, wg_ref, wu_ref, b_ref,
                   h_ref, xb_ref, wcat_ref, stage_ref, zero_ref, sem, fill_sem):
    w = pl.program_id(0)
    j = pl.program_id(1)
    nw = pl.num_programs(0)
    nj = pl.num_programs(1)
    slot = (w * nj + j) % 2
    writer = _TileWriter(wr_ref, wc_ref, stage_ref, h_ref, sem, stage_ref.shape[2])

    @pl.when((w == 0) & (j == 0))
    def _():
        zero_ref[...] = jnp.zeros_like(zero_ref)
        _fill_tail(misc_ref, zero_ref, h_ref, fill_sem)

    writer.retire_older(w, j, nj)

    @pl.when(wc_ref[w] > 0)
    def _():
        half = xs_ref.shape[1]

        @pl.when(j == 0)
        def _():
            u = xs_ref[...]
            xb_ref[:, :half] = _unpack_lo(u).astype(BF16)
            xb_ref[:, half:] = _unpack_hi(u).astype(BF16)

        tn = wg_ref.shape[2]
        wcat_ref[:, :tn] = wg_ref[0].astype(BF16)
        wcat_ref[:, tn:] = wu_ref[0].astype(BF16)
        col = pl.multiple_of(j * tn, tn)
        bg = b_ref[0, :, pl.ds(col, tn)]
        bu = b_ref[0, :, pl.ds(pl.multiple_of(b_ref.shape[2] // 2 + col, tn), tn)]
        def rows_block(rows):
            r = jnp.dot(xb_ref[rows, :], wcat_ref[...], preferred_element_type=F32)
            gate = r[:, :tn] + bg
            up = r[:, tn:] + bu
            gate = jnp.minimum(gate, SWIGLU_LIMIT)
            up = jnp.clip(up, -SWIGLU_LIMIT, SWIGLU_LIMIT)
            act = (up + 1.0) * (gate * jax.nn.sigmoid(SWIGLU_ALPHA * gate))
            stage_ref[slot, rows, :] = act.astype(stage_ref.dtype)

        def all_rows(n_rows):
            mc = n_rows // MOE_UP_ROW_CHUNKS
            for mi in range(MOE_UP_ROW_CHUNKS):
                rows_block(slice(mi * mc, (mi + 1) * mc))

        all_rows(xb_ref.shape[0])
        writer.start(slot, w, j)

    writer.drain(w, j, nw, nj)


def _moe_up(xs, w_gu, b_gu, we, wr, wc, misc, *, d_ff):
    R, half = xs.shape
    E, D, _ = w_gu.shape
    tm, tn = MOE_TM, MOE_TN_UP
    nj = d_ff // tn
    nw = we.shape[0]
    assert nj >= 2 and R % MOE_PAD == 0 and tm % (16 * MOE_UP_ROW_CHUNKS) == 0
    jeff = lambda w, j, wc: jnp.where(wc[w] > 0, j, nj - 1)
    return pl.pallas_call(
        _moe_up_kernel,
        grid_spec=pltpu.PrefetchScalarGridSpec(
            num_scalar_prefetch=4,
            grid=(misc[1], nj),
            in_specs=[pl.BlockSpec((pl.Element(tm), pl.Element(half)),
                                   lambda w, j, we, wr, wc, ms: (pl.multiple_of(wr[w], MOE_PAD), 0)),
                      pl.BlockSpec((1, D, tn), lambda w, j, we, wr, wc, ms: (we[w], 0, jeff(w, j, wc))),
                      pl.BlockSpec((1, D, tn), lambda w, j, we, wr, wc, ms: (we[w], 0, nj + jeff(w, j, wc))),
                      pl.BlockSpec((1, 1, 2 * nj * tn), lambda w, j, we, wr, wc, ms: (we[w], 0, 0))],
            out_specs=pl.BlockSpec(memory_space=pl.ANY),
            scratch_shapes=[pltpu.VMEM((tm, 2 * half), BF16),
                            pltpu.VMEM((D, 2 * tn), BF16),
                            pltpu.VMEM((2, tm, tn), BF16),
                            pltpu.VMEM((MOE_PAD, d_ff), BF16),
                            pltpu.SemaphoreType.DMA((2,)), pltpu.SemaphoreType.DMA(())]),
        out_shape=jax.ShapeDtypeStruct((R, d_ff), BF16),
        compiler_params=_cparams(("arbitrary", "arbitrary")),
        name="moe_up",
    )(we, wr, wc, misc, xs, w_gu, w_gu, b_gu)


def _moe_down_kernel(we_ref, wr_ref, wc_ref, misc_ref, h_ref, w_ref, b_ref,
                     o_ref, stage_ref, zero_ref, sem, fill_sem):
    w = pl.program_id(0)
    j = pl.program_id(1)
    nw = pl.num_programs(0)
    nj = pl.num_programs(1)
    slot = (w * nj + j) % 2
    writer = _TileWriter(wr_ref, wc_ref, stage_ref, o_ref, sem, stage_ref.shape[2])

    @pl.when((w == 0) & (j == 0))
    def _():
        zero_ref[...] = jnp.zeros_like(zero_ref)
        _fill_tail(misc_ref, zero_ref, o_ref, fill_sem)

    writer.retire_older(w, j, nj)

    @pl.when(wc_ref[w] > 0)
    def _():
        tn = stage_ref.shape[2]
        bias = b_ref[0, :, pl.ds(pl.multiple_of(j * 2 * tn, 2 * tn), 2 * tn)]

        def rows_block(rows):
            r = jnp.dot(h_ref[rows, :], w_ref[0].astype(BF16), preferred_element_type=F32) + bias
            stage_ref[slot, rows, :] = _pack_pairs(r[:, :tn], r[:, tn:])

        tm, main = h_ref.shape[0], min(MOE_MAIN_ROWS, h_ref.shape[0])
        if tm > main:
            fits = wc_ref[w] * MOE_PAD <= main
            pl.when(fits)(functools.partial(rows_block, slice(0, main)))
            pl.when(jnp.logical_not(fits))(functools.partial(rows_block, slice(0, tm)))
        else:
            rows_block(slice(0, tm))
        writer.start(slot, w, j)

    writer.drain(w, j, nw, nj)


def _moe_down(h, w_down, b_down, we, wr, wc, misc):
    R, d_ff = h.shape
    E, _, D = w_down.shape
    tm, tn = MOE_TM, MOE_TN_DOWN
    half = D // 2
    nj = half // tn
    nw = we.shape[0]
    assert nj >= 2 and R % MOE_PAD == 0
    jeff = lambda w, j, wc: jnp.where(wc[w] > 0, j, nj - 1)
    return pl.pallas_call(
        _moe_down_kernel,
        grid_spec=pltpu.PrefetchScalarGridSpec(
            num_scalar_prefetch=4,
            grid=(misc[1], nj),
            in_specs=[pl.BlockSpec((pl.Element(tm), pl.Element(d_ff)),
                                   lambda w, j, we, wr, wc, ms: (pl.multiple_of(wr[w], MOE_PAD), 0)),
                      pl.BlockSpec((1, d_ff, 2 * tn), lambda w, j, we, wr, wc, ms: (we[w], 0, jeff(w, j, wc))),
                      pl.BlockSpec((1, 1, 2 * nj * tn), lambda w, j, we, wr, wc, ms: (we[w], 0, 0))],
            out_specs=pl.BlockSpec(memory_space=pl.ANY),
            scratch_shapes=[pltpu.VMEM((2, tm, tn), U32),
                            pltpu.VMEM((MOE_PAD, half), U32),
                            pltpu.SemaphoreType.DMA((2,)), pltpu.SemaphoreType.DMA(())]),
        out_shape=jax.ShapeDtypeStruct((R, half), U32),
        compiler_params=_cparams(("arbitrary", "arbitrary")),
        name="moe_down",
    )(we, wr, wc, misc, h, w_down, b_down)


def _combine_kernel(dest_ref, dnext_ref, eo_ref, gate_ref, x1_ref, g_ref, b_ref, o_ref, buf_ref, sem,
                    *, tc):
    i = pl.program_id(0)
    n = tc * TOP_K
    slot = i % 2

    def gather(d_ref, slot):
        def issue(t, carry):
            for k in range(TOP_K):
                pltpu.make_async_copy(eo_ref.at[pl.ds(d_ref[0, 0, t * TOP_K + k], 1), :],
                                      buf_ref.at[slot, pl.ds(k * tc + t, 1), :],
                                      sem.at[slot]).start(priority=k % 2)
            return carry

        lax.fori_loop(0, tc, issue, 0, unroll=2)

    @pl.when(i == 0)
    def _():
        gather(dest_ref, 0)

    @pl.when(i + 1 < pl.num_programs(0))
    def _():
        gather(dnext_ref, 1 - slot)

    pltpu.make_async_copy(eo_ref.at[pl.ds(0, n), :], buf_ref.at[slot], sem.at[slot]).wait()

    gates = gate_ref[...]
    half = buf_ref.shape[2]
    cw = 2 * LANES
    row_sum = jnp.zeros((tc, 1), F32)
    for c in range(half // cw):
        tn = MOE_TN_DOWN
        u_cols = slice(c * cw, (c + 1) * cw)
        first = (c * cw // tn) * 2 * tn + c * cw % tn
        lo_cols = slice(first, first + cw)
        hi_cols = slice(first + tn, first + tn + cw)
        lo = DEEPNORM_ALPHA * x1_ref[:, lo_cols]
        hi = DEEPNORM_ALPHA * x1_ref[:, hi_cols]
        for k in range(TOP_K):
            u = buf_ref[slot, k * tc:(k + 1) * tc, u_cols]
            gk = gates[:, k:k + 1]
            lo = lo + gk * _unpack_lo(u)
            hi = hi + gk * _unpack_hi(u)
        o_ref[:, lo_cols] = lo
        o_ref[:, hi_cols] = hi
        row_sum = row_sum + jnp.sum(lo, axis=-1, keepdims=True) + jnp.sum(hi, axis=-1, keepdims=True)
    inv_d = 1.0 / (2 * half)
    mu = row_sum * inv_d
    sq_sum = jnp.zeros((tc, 1), F32)
    for c in range(2 * half // cw):
        d = o_ref[:, c * cw:(c + 1) * cw] - mu
        sq_sum = sq_sum + jnp.sum(d * d, axis=-1, keepdims=True)
    rstd = lax.rsqrt(sq_sum * inv_d + LN_EPS)
    for c in range(2 * half // cw):
        cols = slice(c * cw, (c + 1) * cw)
        o_ref[:, cols] = (o_ref[:, cols] - mu) * rstd * g_ref[:, cols] + b_ref[:, cols]


def _combine(dest, eo, gates, x1, g, b):
    T, D = x1.shape
    tc = COMBINE_TC
    n = tc * TOP_K
    nt = T // tc
    half = eo.shape[1]
    row = pl.BlockSpec((tc, D), lambda i: (i, 0))
    vec = pl.BlockSpec((1, D), lambda i: (0, 0))
    dest3 = dest.reshape(nt, 1, n)
    return pl.pallas_call(
        functools.partial(_combine_kernel, tc=tc),
        grid=(nt,),
        in_specs=[pl.BlockSpec((1, 1, n), lambda i: (i, 0, 0), memory_space=pltpu.SMEM),
                  pl.BlockSpec((1, 1, n), lambda i: (jnp.minimum(i + 1, nt - 1), 0, 0),
                               memory_space=pltpu.SMEM),
                  pl.BlockSpec(memory_space=pl.ANY),
                  pl.BlockSpec((tc, TOP_K), lambda i: (i, 0)),
                  row, vec, vec],
        out_specs=row,
        out_shape=jax.ShapeDtypeStruct((T, D), F32),
        scratch_shapes=[pltpu.VMEM((2, n, half), U32), pltpu.SemaphoreType.DMA((2,))],
        compiler_params=_cparams(("arbitrary",)),
        name="combine_rows",
    )(dest3, dest3, eo, gates, x1, g, b)


def _work_items(cnt, starts, n_items):
    chunks_per_pass = MOE_TM // MOE_PAD
    gchunks = (cnt + MOE_PAD - 1) // MOE_PAD
    npass = (gchunks + chunks_per_pass - 1) // chunks_per_pass
    cum = jnp.cumsum(npass)
    total = cum[-1]
    w = jnp.arange(n_items, dtype=I32)
    e = jnp.minimum(jnp.sum((cum[None, :] <= w[:, None]).astype(I32), axis=1), N_EXPERTS - 1)
    p = w - (cum - npass)[e]
    row = starts[e] + p * MOE_TM
    nch = jnp.clip(gchunks[e] - p * chunks_per_pass, 0, chunks_per_pass)
    valid = w < total
    last = jnp.maximum(total - 1, 0)
    e = jnp.where(valid, e, e[last]).astype(I32)
    row = jnp.where(valid, row, row[last]).astype(I32)
    nch = jnp.where(valid, nch, 0).astype(I32)
    misc = jnp.stack([jnp.sum(gchunks), total]).astype(I32)
    fillc = jnp.where(cnt % MOE_PAD != 0, starts // MOE_PAD + gchunks - 1, -1).astype(I32)
    return e, row, nch, misc, fillc


def kernel(x, mem, rel_table, w_in, w_mem_kv, w_o, lambda_q1, lambda_k1, lambda_q2, lambda_k2, subln_g, conv_w, conv_b, conv_ln_g, conv_ln_b, ln1_g, ln1_b, w_router, b_router, w_gate_up, b_gate_up, w_down, b_down, ln2_g, ln2_b):
    B, S, D = x.shape
    T = B * S
    n_heads = rel_table.shape[1]
    qk_w = n_heads * 2 * HEAD_DIM
    v_w = n_heads * V_DIM
    C = conv_w.shape[-1]
    mem_w = w_mem_kv.shape[-1] // 2
    mem_heads = 4
    mem_hd = mem_w // mem_heads
    in_cols = w_in.shape[-1]
    assert in_cols == 2 * qk_w + v_w + 2 * C + mem_w
    d_ff = w_down.shape[2]
    assert DEPTH == 1 and w_in.shape[0] == 1

    col_scale = jnp.concatenate([
        jnp.full((qk_w,), HEAD_DIM ** -0.5, F32),
        jnp.ones((in_cols - qk_w - mem_w,), F32),
        jnp.full((mem_w,), mem_hd ** -0.5, F32)]).reshape(1, in_cols)

    xt = x.reshape(T, D)
    proj = _proj_matmul(xt.astype(BF16), w_in[0], col_scale, tm=1024, tn=768, out_dtype=BF16)
    proj3 = proj.reshape(B, S, in_cols)

    bias = _bias_tiles(rel_table, ATT_T)
    lam_params = jnp.concatenate([lambda_q1, lambda_k1, lambda_q2, lambda_k2], axis=0).astype(F32)
    a_out = _diff_attention(proj3, bias, lam_params, subln_g, n_heads=n_heads,
                            k_col=qk_w // (2 * HEAD_DIM), v_col=2 * qk_w // V_DIM)

    conv_col = (2 * qk_w + v_w) // C
    c_out = _conformer_conv(proj3, conv_w[0, :, 0, :], conv_b, conv_ln_g, conv_ln_b,
                            a_col=conv_col, g_col=conv_col + 1)

    n_mem = mem.shape[1]
    kv = _proj_matmul(mem.reshape(B * n_mem, D).astype(BF16), w_mem_kv[0],
                      jnp.ones((1, 2 * mem_w), F32), tm=B * n_mem, tn=512, out_dtype=BF16)
    m_out = _memory_attention(proj3, kv.reshape(B, n_mem, 2 * mem_w), n_heads=mem_heads,
                              q_col=(in_cols - mem_w) // mem_hd, hd=mem_hd)

    mix = _out_projection(a_out.reshape(T, v_w), c_out.reshape(T, C), m_out.reshape(T, mem_w),
                          w_o[0], tm=1024, tn=512)

    x1, xp, topi, gates = _ln_router(xt, mix, ln1_g, ln1_b, w_router[0],
                                     b_router.reshape(1, -1), tm=256)

    dest, cnt8, starts8 = _rank_assignments(topi)
    cnt, starts = cnt8[0, :N_EXPERTS], starts8[0, :N_EXPERTS]

    n_assign = T * TOP_K
    n_items = n_assign // MOE_TM + N_EXPERTS
    max_rows = n_assign + N_EXPERTS * (MOE_PAD - 1) + MOE_TM
    n_rows = -(-max_rows // MOE_PAD) * MOE_PAD
    we, wr, wc, misc, fillc = _work_items(cnt, starts, n_items)

    xs = _dispatch(dest, xp, fillc, misc, n_rows)
    h = _moe_up(xs, w_gate_up[0], b_gate_up[0].reshape(N_EXPERTS, 1, -1), we, wr, wc, misc, d_ff=d_ff)
    eo = _moe_down(h, w_down[0], b_down[0].reshape(N_EXPERTS, 1, -1), we, wr, wc, misc)
    out = _combine(dest, eo, gates, x1, ln2_g, ln2_b)
    return out.reshape(B, S, D)
```
